```python
import math
import jax, jax.numpy as jnp
from jax import lax
import numpy as np

D_MODEL = 1024
BATCH = 8
SEQ = 4096
DEPTH = 2
DEC_BATCH = 32
DEC_SEQ = 1
PAST_LEN = 16384
PAGE_SIZE = 128

HEAD_DIM = 128
N_HEADS_A = D_MODEL // 256
N_HEADS_B = D_MODEL // 256
N_HEADS_ATT = N_HEADS_A + N_HEADS_B
D_ATT = N_HEADS_ATT * HEAD_DIM
HALF_A = HEAD_DIM // 2
MOBA_BLOCK = 256
MOBA_TOPK = 3
ATT_Q_BLOCK = 128
MOBA_Q_BLOCK = 32
N_HEADS_C = D_MODEL // 256
HEAD_DIM_C = 128
D_C = N_HEADS_C * HEAD_DIM_C
D_RG = D_MODEL // 2
N_BLOCKS_RG = 8
BLOCK_RG = D_RG // N_BLOCKS_RG
CONV_WIDTH = 4
RG_C = 8.0
MLSTM_CHUNK = 128
D_REC = D_C + D_RG
REC_SPLITS = (D_C, 2 * D_C, 3 * D_C, 4 * D_C, 4 * D_C + N_HEADS_C, 4 * D_C + 2 * N_HEADS_C,
              4 * D_C + 2 * N_HEADS_C + D_RG)
REC_IN = 4 * D_C + 2 * N_HEADS_C + 2 * D_RG
D_FF = 128 * ((8 * D_MODEL // 3 + 127) // 128)
N_ATT_LAYERS = (DEPTH + 1) // 2
N_REC_LAYERS = DEPTH // 2
RMS_EPS = 1e-6

kernel_name = 'hybrid_diffmoba_mlstm_rglru_step'


def rmsnorm(x, g):
    xf = x.astype(jnp.float32)
    y = xf * lax.rsqrt(jnp.mean(xf * xf, axis=-1, keepdims=True) + RMS_EPS)
    return (y * g.astype(jnp.float32)).astype(x.dtype)


def swiglu(x, w_gate, w_up, w_down):
    return (jax.nn.silu(x @ w_gate) * (x @ w_up)) @ w_down


def alibi_slopes():
    h = jnp.arange(N_HEADS_ATT, dtype=jnp.float32)
    s = 2.0 ** (-8.0 * (h + 1.0) / N_HEADS_ATT)
    return s[0::2], s[1::2]


def att_project(h, w_in):
    B, S, _ = h.shape
    qkv = (h @ w_in).reshape(B, S, 3, N_HEADS_ATT, HEAD_DIM)
    return qkv[:, :, 0], qkv[:, :, 1], qkv[:, :, 2]


def diff_lambda(lq1, lk1, lq2, lk2, lam_init):
    f = jnp.float32
    return (jnp.exp(jnp.sum(lq1.astype(f) * lk1.astype(f)))
            - jnp.exp(jnp.sum(lq2.astype(f) * lk2.astype(f))) + lam_init)


def diff_attn_block(q, pos, k, v, kpos, slopes, lam):
    scale = HALF_A ** -0.5
    dist = (pos[:, None] - kpos[None, :]).astype(jnp.float32)
    bias = jnp.where(dist >= 0, -slopes[:, None, None] * dist, -jnp.inf)
    s1 = jnp.einsum('bqhd,bkhd->bhqk', q[..., :HALF_A], k[..., :HALF_A]).astype(jnp.float32) * scale + bias
    s2 = jnp.einsum('bqhd,bkhd->bhqk', q[..., HALF_A:], k[..., HALF_A:]).astype(jnp.float32) * scale + bias
    p = jax.nn.softmax(s1, axis=-1) - lam * jax.nn.softmax(s2, axis=-1)
    return jnp.einsum('bhqk,bkhd->bqhd', p.astype(v.dtype), v)


def to_blocks(x):
    B, L, H, D = x.shape
    nb = -(-L // MOBA_BLOCK)
    x = jnp.pad(x, ((0, 0), (0, nb * MOBA_BLOCK - L), (0, 0), (0, 0)))
    return x.reshape(B, nb, MOBA_BLOCK, H, D)


def moba_block(q, pos, k_blocks, v_blocks, kbar, slopes):
    B, Q, H, D = q.shape
    NB = k_blocks.shape[1]
    n_sel = min(MOBA_TOPK, NB)
    own = pos // MOBA_BLOCK
    gate = jnp.einsum('bqhd,bnhd->bhqn', q.astype(jnp.float32), kbar)
    past = jnp.arange(NB)[None, :] < own[:, None]
    gate = jnp.where(past, gate, -jnp.inf)
    _, top = lax.top_k(gate, n_sel)
    idx = jnp.concatenate([top, jnp.broadcast_to(own[None, None, :, None], (B, H, Q, 1))], axis=-1)
    ok = jnp.concatenate([jnp.arange(n_sel)[None, :] < own[:, None],
                          jnp.ones((Q, 1), dtype=bool)], axis=-1)
    bi = jnp.arange(B)[:, None, None, None]
    hi = jnp.arange(H)[None, :, None, None]
    ks = k_blocks[bi, idx, :, hi]
    vs = v_blocks[bi, idx, :, hi]
    kpos = idx[..., None] * MOBA_BLOCK + jnp.arange(MOBA_BLOCK)
    dist = (pos[None, None, :, None, None] - kpos).astype(jnp.float32)
    allowed = ok[None, None, :, :, None] & (dist >= 0)
    s = (jnp.einsum('bqhd,bhqjsd->bhqjs', q, ks).astype(jnp.float32) * (D ** -0.5)
         - slopes[None, :, None, None, None] * dist)
    s = jnp.where(allowed, s, -jnp.inf)
    J = n_sel + 1
    p = jax.nn.softmax(s.reshape(B, H, Q, J * MOBA_BLOCK), axis=-1).reshape(B, H, Q, J, MOBA_BLOCK)
    return jnp.einsum('bhqjs,bhqjsd->bqhd', p.astype(vs.dtype), vs)


def att_merge(oa, ob, sub_g, lam_init, w_out):
    B, S = oa.shape[:2]
    oa = rmsnorm(oa, sub_g) * (1.0 - lam_init)
    return jnp.concatenate([oa, ob], axis=2).reshape(B, S, D_ATT) @ w_out


def att_mixer_prompt(h, w_in, w_out, lam, lam_init, sub_g):
    B, S, _ = h.shape
    q, k, v = att_project(h, w_in)
    sa, sb = alibi_slopes()
    NA = N_HEADS_A
    qa, ka, va = q[:, :, :NA], k[:, :, :NA], v[:, :, :NA]
    qb = q[:, :, NA:]
    kpos = jnp.arange(S)

    def a_block(i):
        pos = i * ATT_Q_BLOCK + jnp.arange(ATT_Q_BLOCK)
        qi = lax.dynamic_slice_in_dim(qa, i * ATT_Q_BLOCK, ATT_Q_BLOCK, axis=1)
        return diff_attn_block(qi, pos, ka, va, kpos, sa, lam)

    oa = lax.map(a_block, jnp.arange(S // ATT_Q_BLOCK))
    oa = jnp.moveaxis(oa, 0, 1).reshape(B, S, NA, HEAD_DIM)
    kbl, vbl = to_blocks(k[:, :, NA:]), to_blocks(v[:, :, NA:])
    kbar = jnp.mean(kbl.astype(jnp.float32), axis=2)

    def b_block(i):
        pos = i * MOBA_Q_BLOCK + jnp.arange(MOBA_Q_BLOCK)
        qi = lax.dynamic_slice_in_dim(qb, i * MOBA_Q_BLOCK, MOBA_Q_BLOCK, axis=1)
        return moba_block(qi, pos, kbl, vbl, kbar, sb)

    ob = lax.map(b_block, jnp.arange(S // MOBA_Q_BLOCK))
    ob = jnp.moveaxis(ob, 0, 1).reshape(B, S, N_HEADS_B, HEAD_DIM)
    return att_merge(oa, ob, sub_g, lam_init, w_out), k, v


def att_mixer_sample(h, cache_k, cache_v, j, page_table, w_in, w_out, lam, lam_init, sub_g):
    B, T, _ = h.shape
    P = page_table.shape[1] * PAGE_SIZE
    q, k, v = att_project(h, w_in)
    sa, sb = alibi_slopes()
    NA = N_HEADS_A

    def past(cache, lo, hi):
        return cache[j, page_table, :, lo:hi].reshape(B, P, hi - lo, HEAD_DIM)

    pos = P + jnp.arange(T)
    ka = jnp.concatenate([past(cache_k, 0, NA), k[:, :, :NA]], axis=1)
    va = jnp.concatenate([past(cache_v, 0, NA), v[:, :, :NA]], axis=1)
    oa = diff_attn_block(q[:, :, :NA], pos, ka, va, jnp.arange(P + T), sa, lam)
    kbl = to_blocks(jnp.concatenate([past(cache_k, NA, N_HEADS_ATT), k[:, :, NA:]], axis=1))
    vbl = to_blocks(jnp.concatenate([past(cache_v, NA, N_HEADS_ATT), v[:, :, NA:]], axis=1))
    kbar = jnp.mean(kbl.astype(jnp.float32), axis=2)
    ob = moba_block(q[:, :, NA:], pos, kbl, vbl, kbar, sb)
    return att_merge(oa, ob, sub_g, lam_init, w_out), k, v


def mlstm_chunked(q, k, v, ig, lf, c0, n0, m0):
    B, H, S, D = q.shape
    L = MLSTM_CHUNK if S % MLSTM_CHUNK == 0 else S
    nc = S // L
    k = k * (D ** -0.5)

    def chunks(a):
        return jnp.moveaxis(a.reshape(a.shape[:2] + (nc, L) + a.shape[3:]), 2, 0)

    causal = jnp.tril(jnp.ones((L, L), dtype=bool))

    def step(carry, xs):
        c, n, m = carry
        qc, kc, vc, ic, fc = xs
        b = jnp.cumsum(fc, axis=-1)
        dmat = jnp.where(causal, b[..., :, None] - b[..., None, :] + ic[..., None, :], -jnp.inf)
        inter = b + m[..., None]
        m_t = jnp.maximum(inter, jnp.max(dmat, axis=-1))
        w_inter = jnp.exp(inter - m_t)
        s = jnp.einsum('bhtd,bhsd->bhts', qc, kc) * jnp.exp(dmat - m_t[..., None])
        num = w_inter[..., None] * jnp.einsum('bhtd,bhde->bhte', qc, c) + jnp.einsum('bhts,bhse->bhte', s, vc)
        den = w_inter * jnp.einsum('bhtd,bhd->bht', qc, n) + jnp.sum(s, axis=-1)
        h = num / jnp.maximum(jnp.abs(den), jnp.exp(-m_t))[..., None]
        m_new = m_t[..., -1]
        w_old = jnp.exp(b[..., -1] + m - m_new)
        w_new = jnp.exp(b[..., -1:] - b + ic - m_new[..., None])
        c_new = w_old[..., None, None] * c + jnp.einsum('bhs,bhsd,bhse->bhde', w_new, kc, vc)
        n_new = w_old[..., None] * n + jnp.einsum('bhs,bhsd->bhd', w_new, kc)
        return (c_new, n_new, m_new), h

    (c1, n1, m1), hs = lax.scan(step, (c0, n0, m0),
                                (chunks(q), chunks(k), chunks(v), chunks(ig), chunks(lf)))
    hs = jnp.moveaxis(hs, 0, 2).reshape(B, H, S, D)
    return hs, c1, n1, m1


def causal_conv(x, buf, w, b):
    S = x.shape[1]
    xp = jnp.concatenate([buf.astype(x.dtype), x], axis=1)
    y = b
    for j in range(CONV_WIDTH):
        y = y + xp[:, j:j + S] * w[j]
    return y, xp[:, S:]


def _lin_combine(c1, c2):
    a1, b1 = c1
    a2, b2 = c2
    return a1 * a2, a2 * b1 + b2


def rglru(x, h0, w_a, b_a, w_x, b_x, lam):
    B, S, _ = x.shape
    f32 = jnp.float32
    xf = x.astype(f32)
    xb = xf.reshape(B, S, N_BLOCKS_RG, BLOCK_RG)
    r = jax.nn.sigmoid(jnp.einsum('bsnk,nkj->bsnj', xb, w_a.astype(f32)).reshape(B, S, D_RG) + b_a.astype(f32))
    i = jax.nn.sigmoid(jnp.einsum('bsnk,nkj->bsnj', xb, w_x.astype(f32)).reshape(B, S, D_RG) + b_x.astype(f32))
    log_a = -RG_C * r * jax.nn.softplus(-lam.astype(f32))
    a = jnp.exp(log_a)
    u = jnp.sqrt(-jnp.expm1(2.0 * log_a)) * (i * xf)
    A, Bc = lax.associative_scan(_lin_combine, (a, u), axis=1)
    hs = A * h0.astype(f32)[:, None, :] + Bc
    return hs, hs[:, -1]


def rec_mixer(h, c0, n0, m0, hrg0, conv0, w_in, w_out, b_i, b_f, norm_g, conv_w, conv_b,
              w_a, b_a, w_x, b_x, lam_rg):
    B, S, _ = h.shape
    f32 = jnp.float32
    z = h @ w_in
    q, k, v, o, ig, fg, xr, gr = jnp.split(z, REC_SPLITS, axis=-1)

    def heads(a):
        return jnp.moveaxis(a.reshape(B, S, N_HEADS_C, HEAD_DIM_C), 1, 2).astype(f32)

    ig = jnp.moveaxis((ig + b_i).astype(f32), 1, 2)
    lf = jax.nn.log_sigmoid(jnp.moveaxis((fg + b_f).astype(f32), 1, 2))
    hc, c1, n1, m1 = mlstm_chunked(heads(q), heads(k), heads(v), ig, lf,
                                   c0.astype(f32), n0.astype(f32), m0.astype(f32))
    hc = rmsnorm(jnp.moveaxis(hc, 1, 2), norm_g.reshape(N_HEADS_C, HEAD_DIM_C))
    hc = hc * jax.nn.sigmoid(o.astype(f32)).reshape(B, S, N_HEADS_C, HEAD_DIM_C)
    xc, conv1 = causal_conv(xr, conv0, conv_w, conv_b)
    y, hrg1 = rglru(xc, hrg0, w_a, b_a, w_x, b_x, lam_rg)
    y = y * jax.nn.gelu(gr.astype(f32))
    out = jnp.concatenate([hc.reshape(B, S, D_C), y], axis=-1).astype(h.dtype) @ w_out
    return (out, c1.astype(c0.dtype), n1.astype(n0.dtype), m1.astype(m0.dtype),
            hrg1.astype(hrg0.dtype), conv1.astype(conv0.dtype))


def setup_inputs(seed: int = 0) -> dict:
    key = jax.random.key(seed)
    keys = jax.random.split(key, 48)
    f32 = jnp.float32
    counter = [0]

    def nk():
        counter[0] += 1
        return keys[counter[0]]

    def nrm(shape, scale):
        return jax.random.normal(nk(), shape, f32) * scale

    n_pages = PAST_LEN // PAGE_SIZE
    n_used = DEC_BATCH * n_pages
    n_pool = n_used + max(1, n_used // 4)
    page_table = jax.random.permutation(keys[0], n_pool)[:n_used].reshape(DEC_BATCH, n_pages).astype(jnp.int32)
    u = jax.random.uniform(nk(), (N_REC_LAYERS, D_RG), f32, 0.9, 0.999)
    a0 = u ** (1.0 / RG_C)
    rg_lambda = jnp.log(a0) - jnp.log1p(-a0)
    return {
        'x_prompt': nrm((BATCH, SEQ, D_MODEL), 1.0),
        'x_sample': nrm((DEC_BATCH, DEC_SEQ, D_MODEL), 1.0),
        'cache_k': nrm((N_ATT_LAYERS, n_pool, PAGE_SIZE, N_HEADS_ATT, HEAD_DIM), 1.0),
        'cache_v': nrm((N_ATT_LAYERS, n_pool, PAGE_SIZE, N_HEADS_ATT, HEAD_DIM), 1.0),
        'state_mlstm_c': nrm((N_REC_LAYERS, DEC_BATCH, N_HEADS_C, HEAD_DIM_C, HEAD_DIM_C), 0.5),
        'state_mlstm_n': nrm((N_REC_LAYERS, DEC_BATCH, N_HEADS_C, HEAD_DIM_C), 0.5),
        'state_mlstm_m': nrm((N_REC_LAYERS, DEC_BATCH, N_HEADS_C), 1.0),
        'state_rglru_h': nrm((N_REC_LAYERS, DEC_BATCH, D_RG), 0.5),
        'state_rglru_conv': nrm((N_REC_LAYERS, DEC_BATCH, CONV_WIDTH - 1, D_RG), 1.0),
        'page_table': page_table,
        'norm_g': 1.0 + nrm((DEPTH, 3, D_MODEL), 0.02),
        'ffn_w_gate': nrm((DEPTH, 2, D_MODEL, D_FF), D_MODEL ** -0.5),
        'ffn_w_up': nrm((DEPTH, 2, D_MODEL, D_FF), D_MODEL ** -0.5),
        'ffn_w_down': nrm((DEPTH, 2, D_FF, D_MODEL), D_FF ** -0.5),
        'att_w_in': nrm((N_ATT_LAYERS, D_MODEL, 3 * D_ATT), D_MODEL ** -0.5),
        'att_w_out': nrm((N_ATT_LAYERS, D_ATT, D_MODEL), D_ATT ** -0.5),
        'diff_lambda_q1': nrm((N_ATT_LAYERS, HALF_A), 0.1),
        'diff_lambda_k1': nrm((N_ATT_LAYERS, HALF_A), 0.1),
        'diff_lambda_q2': nrm((N_ATT_LAYERS, HALF_A), 0.1),
        'diff_lambda_k2': nrm((N_ATT_LAYERS, HALF_A), 0.1),
        'diff_subln_g': 1.0 + nrm((N_ATT_LAYERS, HEAD_DIM), 0.02),
        'rec_w_in': nrm((N_REC_LAYERS, D_MODEL, REC_IN), D_MODEL ** -0.5),
        'rec_w_out': nrm((N_REC_LAYERS, D_REC, D_MODEL), D_REC ** -0.5),
        'mlstm_b_i': nrm((N_REC_LAYERS, N_HEADS_C), 0.1),
        'mlstm_b_f': 3.0 + nrm((N_REC_LAYERS, N_HEADS_C), 0.5),
        'mlstm_norm_g': 1.0 + nrm((N_REC_LAYERS, D_C), 0.02),
        'rg_conv_w': nrm((N_REC_LAYERS, CONV_WIDTH, D_RG), 0.5),
        'rg_conv_b': nrm((N_REC_LAYERS, D_RG), 0.02),
        'rg_w_a': nrm((N_REC_LAYERS, N_BLOCKS_RG, BLOCK_RG, BLOCK_RG), BLOCK_RG ** -0.5),
        'rg_b_a': nrm((N_REC_LAYERS, D_RG), 0.02),
        'rg_w_x': nrm((N_REC_LAYERS, N_BLOCKS_RG, BLOCK_RG, BLOCK_RG), BLOCK_RG ** -0.5),
        'rg_b_x': nrm((N_REC_LAYERS, D_RG), 0.02),
        'rg_lambda': rg_lambda,
        'final_norm_g': 1.0 + nrm((D_MODEL,), 0.02),
    }


def reference(x_prompt, x_sample, cache_k, cache_v, state_mlstm_c, state_mlstm_n, state_mlstm_m,
              state_rglru_h, state_rglru_conv, page_table, norm_g, ffn_w_gate, ffn_w_up, ffn_w_down,
              att_w_in, att_w_out, diff_lambda_q1, diff_lambda_k1, diff_lambda_q2, diff_lambda_k2,
              diff_subln_g, rec_w_in, rec_w_out, mlstm_b_i, mlstm_b_f, mlstm_norm_g, rg_conv_w,
              rg_conv_b, rg_w_a, rg_b_a, rg_w_x, rg_b_x, rg_lambda, final_norm_g):
    xp, xs = x_prompt, x_sample
    B = xp.shape[0]
    dt = xp.dtype
    k_p, v_p, k_s, v_s = [], [], [], []
    rec_p, rec_s = [], []
    for l in range(DEPTH):
        j = l // 2
        ffn_pre = (ffn_w_gate[l, 0], ffn_w_up[l, 0], ffn_w_down[l, 0])
        ffn_post = (ffn_w_gate[l, 1], ffn_w_up[l, 1], ffn_w_down[l, 1])
        xp = xp + 0.5 * swiglu(rmsnorm(xp, norm_g[l, 0]), *ffn_pre)
        xs = xs + 0.5 * swiglu(rmsnorm(xs, norm_g[l, 0]), *ffn_pre)
        hp = rmsnorm(xp, norm_g[l, 1])
        hs = rmsnorm(xs, norm_g[l, 1])
        if l % 2 == 0:
            lam_init = 0.8 - 0.6 * math.exp(-0.3 * l)
            lam = diff_lambda(diff_lambda_q1[j], diff_lambda_k1[j], diff_lambda_q2[j], diff_lambda_k2[j], lam_init)
            op, kp, vp = att_mixer_prompt(hp, att_w_in[j], att_w_out[j], lam, lam_init, diff_subln_g[j])
            os_, ks_, vs_ = att_mixer_sample(hs, cache_k, cache_v, j, page_table, att_w_in[j], att_w_out[j],
                                             lam, lam_init, diff_subln_g[j])
            k_p.append(kp)
            v_p.append(vp)
            k_s.append(ks_)
            v_s.append(vs_)
        else:
            w = (rec_w_in[j], rec_w_out[j], mlstm_b_i[j], mlstm_b_f[j], mlstm_norm_g[j], rg_conv_w[j],
                 rg_conv_b[j], rg_w_a[j], rg_b_a[j], rg_w_x[j], rg_b_x[j], rg_lambda[j])
            c0 = jnp.zeros((B, N_HEADS_C, HEAD_DIM_C, HEAD_DIM_C), dt)
            n0 = jnp.zeros((B, N_HEADS_C, HEAD_DIM_C), dt)
            m0 = jnp.zeros((B, N_HEADS_C), dt)
            h0 = jnp.zeros((B, D_RG), dt)
            conv0 = jnp.zeros((B, CONV_WIDTH - 1, D_RG), dt)
            op, *st_p = rec_mixer(hp, c0, n0, m0, h0, conv0, *w)
            os_, *st_s = rec_mixer(hs, state_mlstm_c[j], state_mlstm_n[j], state_mlstm_m[j],
                                   state_rglru_h[j], state_rglru_conv[j], *w)
            rec_p.append(st_p)
            rec_s.append(st_s)
        xp = xp + op
        xs = xs + os_
        xp = xp + 0.5 * swiglu(rmsnorm(xp, norm_g[l, 2]), *ffn_post)
        xs = xs + 0.5 * swiglu(rmsnorm(xs, norm_g[l, 2]), *ffn_post)
    y_prompt = rmsnorm(xp, final_norm_g)
    y_sample = rmsnorm(xs, final_norm_g)
    new_k_prompt = jnp.stack(k_p, axis=0)
    new_v_prompt = jnp.stack(v_p, axis=0)
    new_k_sample = jnp.stack(k_s, axis=0)
    new_v_sample = jnp.stack(v_s, axis=0)
    new_mlstm_c_prompt = jnp.stack([st[0] for st in rec_p], axis=0)
    new_mlstm_n_prompt = jnp.stack([st[1] for st in rec_p], axis=0)
    new_mlstm_m_prompt = jnp.stack([st[2] for st in rec_p], axis=0)
    new_rglru_h_prompt = jnp.stack([st[3] for st in rec_p], axis=0)
    new_rglru_conv_prompt = jnp.stack([st[4] for st in rec_p], axis=0)
    new_mlstm_c_sample = jnp.stack([st[0] for st in rec_s], axis=0)
    new_mlstm_n_sample = jnp.stack([st[1] for st in rec_s], axis=0)
    new_mlstm_m_sample = jnp.stack([st[2] for st in rec_s], axis=0)
    new_rglru_h_sample = jnp.stack([st[3] for st in rec_s], axis=0)
    new_rglru_conv_sample = jnp.stack([st[4] for st in rec_s], axis=0)
    return (y_prompt, y_sample, new_k_prompt, new_v_prompt, new_k_sample, new_v_sample,
            new_mlstm_c_prompt, new_mlstm_n_prompt, new_mlstm_m_prompt, new_rglru_h_prompt,
            new_rglru_conv_prompt, new_mlstm_c_sample, new_mlstm_n_sample, new_mlstm_m_sample,
            new_rglru_h_sample, new_rglru_conv_sample)
```

```python
import functools
import math

import jax
import jax.numpy as jnp
from jax import lax
from jax.experimental import pallas as pl
from jax.experimental.pallas import tpu as pltpu

F32 = jnp.float32
BF16 = jnp.bfloat16

HEAD_DIM = 128
HALF_A = HEAD_DIM // 2
MOBA_BLOCK = 256
MOBA_TOPK = 3
MLSTM_CHUNK = 128
RG_C = 8.0
N_BLOCKS_RG = 8
RMS_EPS = 1e-6
NEG = -1e30

V7X_VMEM_BYTES = 64 * 1024 * 1024
LANES = 128
SUBLANES = 8

TOKEN_TILE = 512
FF_CHUNK = 256
ATT_TILE = 256
RG_TILE = 512
PAGES_PER_STEP = 16


def _vmem_limit(est_bytes):
    return int(min(max(2 * est_bytes, 32 * 1024 * 1024), V7X_VMEM_BYTES - 8 * 1024 * 1024))


def _rms(x, g):
    return x * lax.rsqrt(jnp.mean(x * x, axis=-1, keepdims=True) + RMS_EPS) * g


def _dot(a, b):
    return jnp.dot(a, b, preferred_element_type=F32)


def _dot_nt(a, b):
    return lax.dot_general(a, b, (((1,), (1,)), ((), ())), preferred_element_type=F32)


def _dot_tn(a, b):
    return lax.dot_general(a, b, (((0,), (0,)), ((), ())), preferred_element_type=F32)


def _split_bf16(x):
    hi = x.astype(BF16)
    lo = (x - hi.astype(F32)).astype(BF16)
    return hi, lo


def _full(shape):
    nd = len(shape)
    return pl.BlockSpec(shape, lambda *_: (0,) * nd)


def _ffn_body(*refs, has_mix, has_final, ff_chunk):
    refs = list(refs)
    x_ref = refs.pop(0)
    if has_mix:
        a_ref, b_ref, wa_ref, wb_ref = refs[:4]
        refs = refs[4:]
    g_ref, wg_ref, wu_ref, wd_ref = refs[:4]
    refs = refs[4:]
    if has_final:
        fg_ref = refs.pop(0)
    o_ref, act_ref = refs

    x = x_ref[...]
    if has_mix:
        x = x + _dot(a_ref[...], wa_ref[...]) + _dot(b_ref[...], wb_ref[...])
    h = _rms(x, g_ref[...]).astype(BF16)
    ff = wg_ref.shape[1]
    for c in range(ff // ff_chunk):
        sl = slice(c * ff_chunk, (c + 1) * ff_chunk)
        gate = _dot(h, wg_ref[:, sl])
        up = _dot(h, wu_ref[:, sl])
        act_ref[:, sl] = (gate * jax.nn.sigmoid(gate) * up).astype(BF16)
    y = x + 0.5 * _dot(act_ref[...], wd_ref[...])
    if has_final:
        y = _rms(y, fg_ref[...])
    o_ref[...] = y


def _ffn_call(x, g, wg, wu, wd, mix=None, final_g=None):
    n, d = x.shape
    ff = wg.shape[1]
    tm = min(n, TOKEN_TILE)
    assert n % tm == 0 and ff % FF_CHUNK == 0
    row = lambda w: pl.BlockSpec((tm, w), lambda i: (i, 0))
    args, specs = [x], [row(d)]
    if mix is not None:
        a, b, wa, wb = mix
        args += [a, b, wa, wb]
        specs += [row(a.shape[1]), row(b.shape[1]), _full(wa.shape), _full(wb.shape)]
    args += [g.reshape(1, d), wg, wu, wd]
    specs += [_full((1, d)), _full(wg.shape), _full(wu.shape), _full(wd.shape)]
    if final_g is not None:
        args.append(final_g.reshape(1, d))
        specs.append(_full((1, d)))
    est = 2 * 3 * d * ff * 2 + 4 * tm * d * 4 + tm * ff * 2 + 4 * tm * FF_CHUNK * 4 + 2 * tm * d * 4
    return pl.pallas_call(
        functools.partial(_ffn_body, has_mix=mix is not None, has_final=final_g is not None,
                          ff_chunk=FF_CHUNK),
        out_shape=jax.ShapeDtypeStruct((n, d), F32),
        grid=(n // tm,),
        in_specs=specs,
        out_specs=row(d),
        scratch_shapes=[pltpu.VMEM((tm, ff), BF16)],
        compiler_params=pltpu.CompilerParams(dimension_semantics=("parallel",),
                                             vmem_limit_bytes=_vmem_limit(est)),
        name="ffn",
    )(*args)


def _qkv_body(x_ref, g_ref, w_ref, *out_refs, prompt):
    x = x_ref[...]
    d = x.shape[1]
    h = _rms(x, g_ref[...]).astype(BF16)
    q = _dot(h, w_ref[:, 0:d])
    k = _dot(h, w_ref[:, d:2 * d])
    v = _dot(h, w_ref[:, 2 * d:3 * d])
    if not prompt:
        q_ref, k_ref, v_ref = out_refs
        q_ref[...] = q
        k_ref[...] = k
        v_ref[...] = v
        return
    q_ref, k_ref, v_ref, kb_ref, vb_ref, kbar_ref = out_refs
    q_ref[...] = q.astype(BF16)
    k_ref[...] = k
    v_ref[...] = v
    lane = lax.broadcasted_iota(jnp.int32, (1, d), 1)
    scale = jnp.where(lane < d // 2, HALF_A ** -0.5, HEAD_DIM ** -0.5).astype(F32)
    kb_ref[...] = (k * scale).astype(BF16)
    vb_ref[...] = v.astype(BF16)
    tm = x.shape[0]
    for r in range(tm // MOBA_BLOCK):
        blk = k[r * MOBA_BLOCK:(r + 1) * MOBA_BLOCK, d // 2:]
        kbar_ref[0, r:r + 1, :] = jnp.mean(blk, axis=0, keepdims=True)


def _qkv_call(x, g, w_in, prompt):
    n, d = x.shape
    tm = min(n, TOKEN_TILE)
    assert n % tm == 0
    row = lambda w: pl.BlockSpec((tm, w), lambda i: (i, 0))
    if prompt:
        assert tm % MOBA_BLOCK == 0
        nb = tm // MOBA_BLOCK
        out_shape = [jax.ShapeDtypeStruct((n, d), BF16), jax.ShapeDtypeStruct((n, d), F32),
                     jax.ShapeDtypeStruct((n, d), F32), jax.ShapeDtypeStruct((n, d), BF16),
                     jax.ShapeDtypeStruct((n, d), BF16),
                     jax.ShapeDtypeStruct((n // tm, nb, d // 2), F32)]
        out_specs = [row(d)] * 5 + [pl.BlockSpec((1, nb, d // 2), lambda i: (i, 0, 0))]
    else:
        out_shape = [jax.ShapeDtypeStruct((n, d), F32)] * 3
        out_specs = [row(d)] * 3
    est = 2 * d * 3 * d * 2 + 2 * tm * d * 4 * 4 + 3 * tm * d * 4
    return pl.pallas_call(
        functools.partial(_qkv_body, prompt=prompt),
        out_shape=out_shape,
        grid=(n // tm,),
        in_specs=[row(d), _full((1, d)), _full(w_in.shape)],
        out_specs=out_specs,
        compiler_params=pltpu.CompilerParams(dimension_semantics=("parallel",),
                                             vmem_limit_bytes=_vmem_limit(est)),
        name="qkv_prompt" if prompt else "qkv_sample",
    )(x, g.reshape(1, d), w_in)


def _softmax_tile(s, shift, v_tile, m_ref, l_ref, acc_ref):
    m_old = m_ref[...]
    m_new = jnp.maximum(m_old, jnp.max(s, axis=-1, keepdims=True) + shift)
    p = jnp.exp(s - (m_new - shift))
    alpha = jnp.exp(m_old - m_new)
    l_ref[...] = alpha * l_ref[...] + jnp.sum(p, axis=-1, keepdims=True)
    acc_ref[...] = alpha * acc_ref[...] + _dot(p.astype(BF16), v_tile)
    m_ref[...] = m_new


def _tile_shift(slope, j, i, t):
    dj = (jnp.zeros((1, 1), jnp.int32) + (j - i) * t).astype(F32)
    return slope * dj


def _diff_lambda(lq1_ref, lk1_ref, lq2_ref, lk2_ref, lam_init):
    a = jnp.exp(jnp.sum(lq1_ref[...] * lk1_ref[...], axis=-1, keepdims=True))
    b = jnp.exp(jnp.sum(lq2_ref[...] * lk2_ref[...], axis=-1, keepdims=True))
    return a - b + lam_init


def _diff_body(slope_ref, q_ref, k_ref, v_ref, lq1_ref, lk1_ref, lq2_ref, lk2_ref, subg_ref,
               o_ref, qa_ref, m_ref, l_ref, acc_ref, *, t, lam_init):
    i = pl.program_id(2)
    slope = slope_ref[0][:, 0:1]
    q = q_ref[...]
    lane = lax.broadcasted_iota(jnp.int32, (t, HEAD_DIM), 1)
    sub = lax.broadcasted_iota(jnp.int32, (t, HEAD_DIM), 0)
    zero = jnp.zeros_like(q)
    ones_col = jnp.where(lane == 0, 1.0, 0.0).astype(BF16)
    qa_ref[0:t, 0:HEAD_DIM] = jnp.where(lane < HALF_A, q, zero)
    qa_ref[t:2 * t, 0:HEAD_DIM] = jnp.where(lane >= HALF_A, q, zero)
    qa_ref[0:t, HEAD_DIM:] = ones_col
    qa_ref[t:2 * t, HEAD_DIM:] = ones_col
    kext = jnp.where(lane == 0, slope * sub.astype(F32), 0.0).astype(BF16)

    m_ref[...] = jnp.full(m_ref.shape, NEG, F32)
    l_ref[...] = jnp.zeros(l_ref.shape, F32)
    acc_ref[...] = jnp.zeros(acc_ref.shape, F32)

    def tile(j, causal):
        start = pl.multiple_of(j * t, t)
        ka = jnp.concatenate([k_ref[pl.ds(start, t), :], kext], axis=1)
        s = _dot_nt(qa_ref[...], ka)
        if causal:
            r2 = lax.broadcasted_iota(jnp.int32, (2 * t, t), 0)
            c2 = lax.broadcasted_iota(jnp.int32, (2 * t, t), 1)
            r2 = jnp.where(r2 >= t, r2 - t, r2)
            s = jnp.where(c2 <= r2, s, NEG)
        _softmax_tile(s, _tile_shift(slope, j, i, t), v_ref[pl.ds(start, t), :], m_ref, l_ref, acc_ref)

    def body(j, carry):
        tile(j, False)
        return carry

    lax.fori_loop(0, i, body, 0)
    tile(i, True)

    o = acc_ref[...] / l_ref[...]
    lam = _diff_lambda(lq1_ref, lk1_ref, lq2_ref, lk2_ref, lam_init)
    oa = o[0:t] - lam * o[t:2 * t]
    o_ref[...] = (_rms(oa, subg_ref[...]) * (1.0 - lam_init)).astype(BF16)


def _diff_call(q, kb, vb, slopes, lam_vecs, subg, batch, seq, lam_init):
    n, d = q.shape
    nh = d // 2 // HEAD_DIM
    t = min(ATT_TILE, seq)
    assert seq % t == 0 and t <= 256
    nq = seq // t
    vec = lambda w: _full((1, w))
    est = 2 * 2 * seq * HEAD_DIM * 2 + 2 * t * 2 * HEAD_DIM * 2 + 2 * t * (HEAD_DIM + 2 * LANES) * 4 + 6 * 2 * t * t * 4
    return pl.pallas_call(
        functools.partial(_diff_body, t=t, lam_init=lam_init),
        out_shape=jax.ShapeDtypeStruct((n, d // 2), BF16),
        grid=(batch, nh, nq),
        in_specs=[pl.BlockSpec((1, 1, LANES), lambda b, h, i: (h, 0, 0)),
                  pl.BlockSpec((t, HEAD_DIM), lambda b, h, i: (b * nq + i, h)),
                  pl.BlockSpec((seq, HEAD_DIM), lambda b, h, i: (b, h)),
                  pl.BlockSpec((seq, HEAD_DIM), lambda b, h, i: (b, h)),
                  vec(HALF_A), vec(HALF_A), vec(HALF_A), vec(HALF_A), vec(HEAD_DIM)],
        out_specs=pl.BlockSpec((t, HEAD_DIM), lambda b, h, i: (b * nq + i, h)),
        scratch_shapes=[pltpu.VMEM((2 * t, 2 * HEAD_DIM), BF16), pltpu.VMEM((2 * t, 1), F32),
                        pltpu.VMEM((2 * t, 1), F32), pltpu.VMEM((2 * t, HEAD_DIM), F32)],
        compiler_params=pltpu.CompilerParams(dimension_semantics=("parallel", "parallel", "parallel"),
                                             vmem_limit_bytes=_vmem_limit(est)),
        name="diff_attn_prompt",
    )(slopes, q, kb, vb, *lam_vecs, subg)


def _moba_body(slope_ref, q_ref, k_ref, v_ref, kbar_ref, o_ref, qa_ref, m_ref, l_ref, acc_ref,
               *, t, n_blocks):
    i = pl.program_id(2)
    slope = slope_ref[0][:, 0:1]
    q = q_ref[...]
    lane = lax.broadcasted_iota(jnp.int32, (t, LANES), 1)
    sub = lax.broadcasted_iota(jnp.int32, (t, LANES), 0)

    kbar_hi, kbar_lo = _split_bf16(kbar_ref[0])
    gate = _dot_nt(q, kbar_hi) + _dot_nt(q, kbar_lo)
    gate = jnp.where(lane < i, gate, NEG)
    cnt = jnp.zeros((t, LANES), jnp.int32)
    for n2 in range(n_blocks):
        gn = gate[:, n2:n2 + 1]
        beats = (gn > gate) | ((gn == gate) & (lane > n2))
        cnt = cnt + jnp.where(beats & (i > n2), 1, 0)
    keep = ((lane < i) & (cnt < MOBA_TOPK)) | (lane == i)
    qa_ref[:, 0:HEAD_DIM] = q
    qa_ref[:, HEAD_DIM:] = jnp.where(lane == LANES - 1, 1.0, jnp.where(keep, 0.0, NEG)).astype(BF16)
    alibi_col = jnp.where(lane == LANES - 1, slope * sub.astype(F32), 0.0)

    m_ref[...] = jnp.full(m_ref.shape, NEG, F32)
    l_ref[...] = jnp.zeros(l_ref.shape, F32)
    acc_ref[...] = jnp.zeros(acc_ref.shape, F32)

    def tile(n, causal):
        start = pl.multiple_of(n * t, t)
        kext = jnp.where(lane == n, 1.0, alibi_col).astype(BF16)
        ka = jnp.concatenate([k_ref[pl.ds(start, t), :], kext], axis=1)
        s = _dot_nt(qa_ref[...], ka)
        if causal:
            r2 = lax.broadcasted_iota(jnp.int32, (t, t), 0)
            c2 = lax.broadcasted_iota(jnp.int32, (t, t), 1)
            s = jnp.where(c2 <= r2, s, NEG)
        _softmax_tile(s, _tile_shift(slope, n, i, t), v_ref[pl.ds(start, t), :], m_ref, l_ref, acc_ref)

    def body(n, carry):
        tile(n, False)
        return carry

    lax.fori_loop(0, i, body, 0)
    tile(i, True)
    o_ref[...] = (acc_ref[...] / l_ref[...]).astype(BF16)


def _moba_call(q, kb, vb, kbar, slopes, batch, seq):
    n, d = q.shape
    nh = d // 2 // HEAD_DIM
    t = MOBA_BLOCK
    assert seq % t == 0
    nq = seq // t
    assert nq < LANES
    est = 2 * 2 * seq * HEAD_DIM * 2 + 2 * t * 2 * HEAD_DIM * 2 + t * (HEAD_DIM + 2 * LANES) * 4 + 8 * t * t * 4
    return pl.pallas_call(
        functools.partial(_moba_body, t=t, n_blocks=nq),
        out_shape=jax.ShapeDtypeStruct((n, d // 2), BF16),
        grid=(batch, nh, nq),
        in_specs=[pl.BlockSpec((1, 1, LANES), lambda b, h, i: (h, 0, 0)),
                  pl.BlockSpec((t, HEAD_DIM), lambda b, h, i: (b * nq + i, nh + h)),
                  pl.BlockSpec((seq, HEAD_DIM), lambda b, h, i: (b, nh + h)),
                  pl.BlockSpec((seq, HEAD_DIM), lambda b, h, i: (b, nh + h)),
                  pl.BlockSpec((1, LANES, HEAD_DIM), lambda b, h, i: (b, 0, h))],
        out_specs=pl.BlockSpec((t, HEAD_DIM), lambda b, h, i: (b * nq + i, h)),
        scratch_shapes=[pltpu.VMEM((t, 2 * HEAD_DIM), BF16), pltpu.VMEM((t, 1), F32),
                        pltpu.VMEM((t, 1), F32), pltpu.VMEM((t, HEAD_DIM), F32)],
        compiler_params=pltpu.CompilerParams(dimension_semantics=("parallel", "parallel", "parallel"),
                                             vmem_limit_bytes=_vmem_limit(est)),
        name="moba_attn_prompt",
    )(slopes, q, kb, vb, kbar)


QROWS = 16


def _sample_a_body(pt_ref, *refs, pps, n_pages, nh, lam_init, layer_off):
    del pt_ref, layer_off
    kp = refs[:pps]
    vp = refs[pps:2 * pps]
    (q_ref, kn_ref, vn_ref, slope_ref, lq1_ref, lk1_ref, lq2_ref, lk2_ref, subg_ref,
     oa_ref, sel_ref, w_ref, m_ref, l_ref, acc_ref, gate_ref) = refs[2 * pps:]
    c = pl.program_id(1)
    d = q_ref.shape[-1]
    dh = d // 2
    past = n_pages * kp[0].shape[1]
    page = kp[0].shape[1]
    row = lax.broadcasted_iota(jnp.int32, (QROWS, 1), 0)
    slope_rows = jnp.zeros((QROWS, 1), F32)
    for h in range(nh):
        slope_rows = jnp.where((row == 2 * h) | (row == 2 * h + 1), slope_ref[h][:, 0:1], slope_rows)

    @pl.when(c == 0)
    def _():
        q = q_ref[0]
        r = lax.broadcasted_iota(jnp.int32, (QROWS, d), 0)
        ln = lax.broadcasted_iota(jnp.int32, (QROWS, d), 1)
        seg = ln // HALF_A
        is_diff = (r < 2 * nh) & (seg == r)
        hb = jnp.where(r >= 3 * nh, r - 3 * nh, r - 2 * nh)
        is_moba = (r >= 2 * nh) & (ln // HEAD_DIM == nh + hb)
        qb = jnp.broadcast_to(q, (QROWS, d))
        q_hi = qb.astype(BF16).astype(F32)
        w = jnp.where(is_diff, qb * (HALF_A ** -0.5),
                      jnp.where(is_moba, jnp.where(r < 3 * nh, q_hi, qb - q_hi), 0.0))
        w_ref[...] = w.astype(BF16)
        m_ref[...] = jnp.full(m_ref.shape, NEG, F32)
        l_ref[...] = jnp.zeros(l_ref.shape, F32)
        acc_ref[...] = jnp.zeros(acc_ref.shape, F32)
        gate_ref[...] = jnp.zeros(gate_ref.shape, F32)

    w = w_ref[...]
    lane = lax.broadcasted_iota(jnp.int32, (QROWS, page), 1)
    glane = lax.broadcasted_iota(jnp.int32, (QROWS, LANES), 1)
    scores = []
    gate_add = jnp.zeros((QROWS, LANES), F32)
    for p in range(pps):
        s = _dot_nt(w, kp[p][0].astype(BF16))
        g_page = c * pps + p
        blk = (g_page * page) // MOBA_BLOCK
        gate_add = gate_add + jnp.where(glane == blk, jnp.sum(s, axis=-1, keepdims=True), 0.0)
        dist = (past - g_page * page - lane).astype(F32)
        scores.append(s - slope_rows * dist)
    gate_ref[...] += gate_add
    s = jnp.concatenate(scores, axis=1)
    m_old = m_ref[...]
    m_new = jnp.maximum(m_old, jnp.max(s, axis=-1, keepdims=True))
    pr = jnp.exp(s - m_new)
    alpha = jnp.exp(m_old - m_new)
    l_ref[...] = alpha * l_ref[...] + jnp.sum(pr, axis=-1, keepdims=True)
    pv = jnp.zeros(acc_ref.shape, F32)
    for p in range(pps):
        pv = pv + _dot(pr[:, p * page:(p + 1) * page].astype(BF16), vp[p][0].astype(BF16))
    acc_ref[...] = alpha * acc_ref[...] + pv
    m_ref[...] = m_new

    @pl.when(c == pl.num_programs(1) - 1)
    def _():
        wf = w_ref[...].astype(F32)
        s_new = jnp.sum(wf * kn_ref[0], axis=-1, keepdims=True)
        m_old = m_ref[...]
        m_new = jnp.maximum(m_old, s_new)
        p_new = jnp.exp(s_new - m_new)
        alpha = jnp.exp(m_old - m_new)
        l = alpha * l_ref[...] + p_new
        acc = alpha * acc_ref[...] + p_new * vn_ref[0][:, 0:dh]
        o = acc / l
        lam = _diff_lambda(lq1_ref, lk1_ref, lq2_ref, lk2_ref, lam_init)
        for h in range(nh):
            cs = slice(h * HEAD_DIM, (h + 1) * HEAD_DIM)
            oa = o[2 * h:2 * h + 1, cs] - lam * o[2 * h + 1:2 * h + 2, cs]
            oa_ref[0, :, cs] = _rms(oa, subg_ref[...]) * (1.0 - lam_init)
        n_past = past // MOBA_BLOCK
        g = (gate_ref[2 * nh:3 * nh, :] + gate_ref[3 * nh:4 * nh, :]) * (1.0 / MOBA_BLOCK)
        gl = lax.broadcasted_iota(jnp.int32, g.shape, 1)
        g = jnp.where(gl < n_past, g, NEG)
        cnt = jnp.zeros(g.shape, jnp.int32)
        for n2 in range(n_past):
            gn = g[:, n2:n2 + 1]
            cnt = cnt + jnp.where((gn > g) | ((gn == g) & (gl > n2)), 1, 0)
        sel = jnp.zeros(g.shape, jnp.int32)
        for j in range(MOBA_TOPK):
            idx = jnp.sum(jnp.where((cnt == j) & (gl < n_past), gl, 0), axis=-1, keepdims=True)
            sel = jnp.where(gl == j, idx, sel)
        sel_ref[0] = sel


def _sample_a_call(page_table, cache_k, cache_v, layer, q, k_new, v_new, slopes, lam_vecs, subg, lam_init):
    db, n_pages = page_table.shape
    n_layers, n_pool, page, n_heads, hd = cache_k.shape
    d = n_heads * hd
    nh = n_heads // 2
    assert (n_pages * page) % MOBA_BLOCK == 0 and MOBA_BLOCK % page == 0
    assert n_pages * page // MOBA_BLOCK <= LANES
    pps = math.gcd(PAGES_PER_STEP, n_pages)
    ck = cache_k.reshape(n_layers * n_pool, page, d)
    cv = cache_v.reshape(n_layers * n_pool, page, d)
    off = layer * n_pool

    def k_spec(p):
        return pl.BlockSpec((1, page, d), lambda b, c, pt: (pt[b, c * pps + p] + off, 0, 0))

    def v_spec(p):
        return pl.BlockSpec((1, page, d // 2), lambda b, c, pt: (pt[b, c * pps + p] + off, 0, 0))

    tok = lambda w: pl.BlockSpec((1, 1, w), lambda b, c, pt: (b, 0, 0))
    fixed = lambda shape: pl.BlockSpec(shape, lambda b, c, pt: (0,) * len(shape))
    in_specs = ([k_spec(p) for p in range(pps)] + [v_spec(p) for p in range(pps)]
                + [tok(d), tok(d), tok(d), fixed((nh, 1, LANES)),
                   fixed((1, HALF_A)), fixed((1, HALF_A)), fixed((1, HALF_A)), fixed((1, HALF_A)),
                   fixed((1, HEAD_DIM))])
    est = 2 * pps * page * (d + d // 2) * 4 + pps * page * QROWS * 4 * 4 + QROWS * d * 8
    oa, sel = pl.pallas_call(
        functools.partial(_sample_a_body, pps=pps, n_pages=n_pages, nh=nh, lam_init=lam_init,
                          layer_off=off),
        out_shape=[jax.ShapeDtypeStruct((db, 1, d // 2), F32),
                   jax.ShapeDtypeStruct((db, nh, LANES), jnp.int32)],
        grid_spec=pltpu.PrefetchScalarGridSpec(
            num_scalar_prefetch=1,
            grid=(db, n_pages // pps),
            in_specs=in_specs,
            out_specs=[pl.BlockSpec((1, 1, d // 2), lambda b, c, pt: (b, 0, 0)),
                       pl.BlockSpec((1, nh, LANES), lambda b, c, pt: (b, 0, 0))],
            scratch_shapes=[pltpu.VMEM((QROWS, d), BF16), pltpu.VMEM((QROWS, 1), F32),
                            pltpu.VMEM((QROWS, 1), F32), pltpu.VMEM((QROWS, d // 2), F32),
                            pltpu.VMEM((QROWS, LANES), F32)]),
        compiler_params=pltpu.CompilerParams(dimension_semantics=("parallel", "arbitrary"),
                                             vmem_limit_bytes=_vmem_limit(est)),
        name="sample_attn_a",
    )(page_table, *([ck] * pps), *([cv] * pps), q, k_new, v_new, slopes, *lam_vecs, subg)
    return oa, sel


def _sample_b_body(pt_ref, sel_ref, *refs, n_sel, ppb, n_pages, nh):
    del pt_ref
    n_op = n_sel * ppb
    kp = refs[:n_op]
    vp = refs[n_op:2 * n_op]
    q_ref, kn_ref, vn_ref, slope_ref, o_ref = refs[2 * n_op:]
    b = pl.program_id(0)
    h = pl.program_id(1)
    page = kp[0].shape[1]
    past = n_pages * page
    slope = slope_ref[0][:, 0:1]
    scale = HEAD_DIM ** -0.5
    q = jnp.broadcast_to(q_ref[0], (SUBLANES, HEAD_DIM)).astype(BF16)
    lane = lax.broadcasted_iota(jnp.int32, (SUBLANES, page), 1)
    scores = []
    for j in range(n_sel):
        blk = sel_ref[b, h * n_sel + j]
        for r in range(ppb):
            s = _dot_nt(q, kp[j * ppb + r][0].astype(BF16)) * scale
            dist = (past - (blk * MOBA_BLOCK + r * page) - lane).astype(F32)
            scores.append(s - slope * dist)
    s_new = jnp.sum(q_ref[0] * kn_ref[0], axis=-1, keepdims=True) * scale
    m = s_new
    if scores:
        s = jnp.concatenate(scores, axis=1)
        m = jnp.maximum(jnp.max(s, axis=-1, keepdims=True)[0:1], s_new)
        pr = jnp.exp(s - m)
        l = jnp.sum(pr, axis=-1, keepdims=True)[0:1]
        acc = jnp.zeros((SUBLANES, HEAD_DIM), F32)
        for j in range(n_op):
            acc = acc + _dot(pr[:, j * page:(j + 1) * page].astype(BF16), vp[j][0].astype(BF16))
        acc = acc[0:1]
    else:
        l = jnp.zeros((1, 1), F32)
        acc = jnp.zeros((1, HEAD_DIM), F32)
    p_new = jnp.exp(s_new - m)
    o_ref[0] = (acc + p_new * vn_ref[0]) / (l + p_new)


def _sample_b_call(page_table, sel, cache_k, cache_v, layer, q, k_new, v_new, slopes):
    db, n_pages = page_table.shape
    n_layers, n_pool, page, n_heads, hd = cache_k.shape
    d = n_heads * hd
    nh = n_heads // 2
    ppb = MOBA_BLOCK // page
    n_past = n_pages * page // MOBA_BLOCK
    n_sel = min(MOBA_TOPK, n_past)
    ck = cache_k.reshape(n_layers * n_pool, page, d)
    cv = cache_v.reshape(n_layers * n_pool, page, d)
    off = layer * n_pool

    def page_spec(j, r):
        return pl.BlockSpec((1, page, hd),
                            lambda b, h, pt, sl: (pt[b, sl[b, h * n_sel + j] * ppb + r] + off, 0, nh + h))

    tok = pl.BlockSpec((1, 1, hd), lambda b, h, pt, sl: (b, 0, nh + h))
    specs = [page_spec(j, r) for j in range(n_sel) for r in range(ppb)]
    in_specs = specs + specs + [tok, tok, tok, pl.BlockSpec((1, 1, LANES), lambda b, h, pt, sl: (h, 0, 0))]
    n_op = n_sel * ppb
    est = 2 * 2 * n_op * page * hd * 4 + 64 * 1024
    return pl.pallas_call(
        functools.partial(_sample_b_body, n_sel=n_sel, ppb=ppb, n_pages=n_pages, nh=nh),
        out_shape=jax.ShapeDtypeStruct((db, 1, d // 2), F32),
        grid_spec=pltpu.PrefetchScalarGridSpec(
            num_scalar_prefetch=2,
            grid=(db, nh),
            in_specs=in_specs,
            out_specs=pl.BlockSpec((1, 1, hd), lambda b, h, pt, sl: (b, 0, h))),
        compiler_params=pltpu.CompilerParams(dimension_semantics=("parallel", "parallel"),
                                             vmem_limit_bytes=_vmem_limit(est)),
        name="sample_attn_b",
    )(page_table, sel[:, :, :n_sel].reshape(db, nh * n_sel), *([ck] * n_op), *([cv] * n_op),
      q, k_new, v_new, slopes)


def _rec_in_body(x_ref, g_ref, w_ref, qkvo_ref, gates_ref, xr_ref, gr_ref, *, dc, drg):
    x = x_ref[...]
    h = _rms(x, g_ref[...]).astype(BF16)
    kscale = HEAD_DIM ** -0.5
    qkvo_ref[:, 0:dc] = _dot(h, w_ref[:, 0:dc]).astype(qkvo_ref.dtype)
    qkvo_ref[:, dc:2 * dc] = (_dot(h, w_ref[:, dc:2 * dc]) * kscale).astype(qkvo_ref.dtype)
    qkvo_ref[:, 2 * dc:3 * dc] = _dot(h, w_ref[:, 2 * dc:3 * dc]).astype(qkvo_ref.dtype)
    qkvo_ref[:, 3 * dc:4 * dc] = _dot(h, w_ref[:, 3 * dc:4 * dc]).astype(qkvo_ref.dtype)
    xr_ref[...] = _dot(h, w_ref[:, 4 * dc:4 * dc + drg])
    gr_ref[...] = _dot(h, w_ref[:, 4 * dc + drg:4 * dc + 2 * drg])
    gates_ref[...] = _dot(h, w_ref[:, 4 * dc + 2 * drg:])


def _rec_in_call(x, g, w_packed, dc, drg, qkvo_dtype):
    n, d = x.shape
    tm = min(n, TOKEN_TILE)
    assert n % tm == 0
    row = lambda w: pl.BlockSpec((tm, w), lambda i: (i, 0))
    est = 2 * w_packed.size * 2 + 2 * tm * d * 4 + 2 * tm * (4 * dc + 2 * drg + LANES) * 4
    return pl.pallas_call(
        functools.partial(_rec_in_body, dc=dc, drg=drg),
        out_shape=[jax.ShapeDtypeStruct((n, 4 * dc), qkvo_dtype), jax.ShapeDtypeStruct((n, LANES), F32),
                   jax.ShapeDtypeStruct((n, drg), F32), jax.ShapeDtypeStruct((n, drg), F32)],
        grid=(n // tm,),
        in_specs=[row(d), _full((1, d)), _full(w_packed.shape)],
        out_specs=[row(4 * dc), row(LANES), row(drg), row(drg)],
        compiler_params=pltpu.CompilerParams(dimension_semantics=("parallel",),
                                             vmem_limit_bytes=_vmem_limit(est)),
        name="rec_in",
    )(x, g.reshape(1, d), w_packed)


def _log_sigmoid(x):
    return jnp.minimum(x, 0.0) - jnp.log1p(jnp.exp(-jnp.abs(x)))


def _softplus(x):
    return jnp.maximum(x, 0.0) + jnp.log1p(jnp.exp(-jnp.abs(x)))


def _mlstm_body(qkvo_ref, gates_ref, bias_ref, ng_ref, hc_ref, c_out, n_out, m_out,
                c_ref, n_ref, m_ref, *, nh):
    ch = pl.program_id(1)
    L = qkvo_ref.shape[0]
    dc = nh * HEAD_DIM

    @pl.when(ch == 0)
    def _():
        c_ref[...] = jnp.zeros(c_ref.shape, F32)
        n_ref[...] = jnp.zeros(n_ref.shape, F32)
        m_ref[...] = jnp.zeros(m_ref.shape, F32)

    lane = lax.broadcasted_iota(jnp.int32, (L, LANES), 1)
    g = gates_ref[...] + bias_ref[...]
    x = jnp.where(lane < nh, g, jnp.where(lane < 2 * nh, _log_sigmoid(g), 0.0))
    tt = lax.broadcasted_iota(jnp.int32, (L, L), 0)
    ss = lax.broadcasted_iota(jnp.int32, (L, L), 1)
    causal = ss <= tt
    tril = jnp.where(causal, 1.0, 0.0).astype(BF16)
    x_hi, x_lo = _split_bf16(x)
    x_mid, x_lo = _split_bf16(x - x_hi.astype(F32))
    cum = _dot(tril, x_hi) + _dot(tril, x_mid) + _dot(tril, x_lo)
    colv = jnp.where(lane < nh, x, cum)
    rowv = colv.T

    for h in range(nh):
        cs = slice(h * HEAD_DIM, (h + 1) * HEAD_DIM)
        q = qkvo_ref[:, cs]
        k = qkvo_ref[:, dc + h * HEAD_DIM:dc + (h + 1) * HEAD_DIM]
        v = qkvo_ref[:, 2 * dc + h * HEAD_DIM:2 * dc + (h + 1) * HEAD_DIM]
        o = qkvo_ref[:, 3 * dc + h * HEAD_DIM:3 * dc + (h + 1) * HEAD_DIM].astype(F32)
        ig_col = colv[:, h:h + 1]
        b_col = colv[:, nh + h:nh + h + 1]
        ig_row = rowv[h:h + 1, :]
        b_row = rowv[nh + h:nh + h + 1, :]
        m_prev = m_ref[h][:, 0:1]
        c_prev = c_ref[h]
        n_prev = n_ref[h]

        dmat = jnp.where(causal, b_col - b_row + ig_row, NEG)
        inter = b_col + m_prev
        m_t = jnp.maximum(inter, jnp.max(dmat, axis=-1, keepdims=True))
        w_inter = jnp.exp(inter - m_t)
        s = _dot_nt(q, k) * jnp.exp(dmat - m_t)
        qf = q.astype(F32)
        num = w_inter * _dot(q, c_prev.astype(BF16)) + _dot(s.astype(BF16), v)
        den = w_inter * jnp.sum(qf * n_prev, axis=-1, keepdims=True) + jnp.sum(s, axis=-1, keepdims=True)
        hh = num / jnp.maximum(jnp.abs(den), jnp.exp(-m_t))
        m_new = m_t[L - 1:L, :]
        b_last = b_col[L - 1:L, :]
        w_old = jnp.exp(b_last + m_prev - m_new)
        w_new = jnp.exp(b_last - b_col + ig_col - m_new)
        kw = k.astype(F32) * w_new
        c_ref[h] = w_old * c_prev + _dot_tn(kw.astype(BF16), v)
        n_ref[h] = w_old * n_prev + jnp.sum(kw, axis=0, keepdims=True)
        m_ref[h] = jnp.broadcast_to(m_new, (1, LANES))
        hc_ref[:, cs] = (_rms(hh, ng_ref[:, cs]) * jax.nn.sigmoid(o)).astype(hc_ref.dtype)

    @pl.when(ch == pl.num_programs(1) - 1)
    def _():
        c_out[0] = c_ref[...]
        n_out[0] = n_ref[...]
        m_out[0] = m_ref[...]


def _mlstm_call(qkvo, gates, bias, norm_g, batch, seq):
    n = qkvo.shape[0]
    dc = qkvo.shape[1] // 4
    nh = dc // HEAD_DIM
    L = MLSTM_CHUNK if seq % MLSTM_CHUNK == 0 else seq
    assert L % SUBLANES == 0 and L == LANES, "prompt mLSTM kernel needs 128-token chunks"
    nc = seq // L
    est = 2 * L * 4 * dc * 2 + 2 * L * LANES * 4 + 2 * L * dc * 2 + 3 * nh * HEAD_DIM * HEAD_DIM * 4 + 16 * L * L * 4
    return pl.pallas_call(
        functools.partial(_mlstm_body, nh=nh),
        out_shape=[jax.ShapeDtypeStruct((n, dc), BF16),
                   jax.ShapeDtypeStruct((batch, nh, HEAD_DIM, HEAD_DIM), F32),
                   jax.ShapeDtypeStruct((batch, nh, 1, HEAD_DIM), F32),
                   jax.ShapeDtypeStruct((batch, nh, 1, LANES), F32)],
        grid=(batch, nc),
        in_specs=[pl.BlockSpec((L, 4 * dc), lambda b, c: (b * nc + c, 0)),
                  pl.BlockSpec((L, LANES), lambda b, c: (b * nc + c, 0)),
                  _full((1, LANES)), _full((1, dc))],
        out_specs=[pl.BlockSpec((L, dc), lambda b, c: (b * nc + c, 0)),
                   pl.BlockSpec((1, nh, HEAD_DIM, HEAD_DIM), lambda b, c: (b, 0, 0, 0)),
                   pl.BlockSpec((1, nh, 1, HEAD_DIM), lambda b, c: (b, 0, 0, 0)),
                   pl.BlockSpec((1, nh, 1, LANES), lambda b, c: (b, 0, 0, 0))],
        scratch_shapes=[pltpu.VMEM((nh, HEAD_DIM, HEAD_DIM), F32), pltpu.VMEM((nh, 1, HEAD_DIM), F32),
                        pltpu.VMEM((nh, 1, LANES), F32)],
        compiler_params=pltpu.CompilerParams(dimension_semantics=("parallel", "arbitrary"),
                                             vmem_limit_bytes=_vmem_limit(est)),
        name="mlstm_prompt",
    )(qkvo, gates, bias, norm_g)


def _rg_gates(xc, wa_ref, ba_ref, wx_ref, bx_ref, lam_ref):
    xb = xc.astype(BF16)
    r = jax.nn.sigmoid(_dot(xb, wa_ref[...]) + ba_ref[...])
    i = jax.nn.sigmoid(_dot(xb, wx_ref[...]) + bx_ref[...])
    log_a = -RG_C * r * _softplus(-lam_ref[...])
    a = jnp.exp(log_a)
    u = jnp.sqrt(-jnp.tanh(log_a) * (a * a + 1.0)) * (i * xc)
    return a, u


def _rglru_body(xr_ref, gr_ref, cw_ref, cb_ref, wa_ref, ba_ref, wx_ref, bx_ref, lam_ref,
                y_ref, h_out, xbuf_ref, a_ref, u_ref, h_ref, *, width):
    tstep = pl.program_id(1)
    T, C = xr_ref.shape
    pad = SUBLANES

    @pl.when(tstep == 0)
    def _():
        xbuf_ref[0:pad, :] = jnp.zeros((pad, C), F32)
        h_ref[...] = jnp.zeros(h_ref.shape, F32)

    xbuf_ref[pad:pad + T, :] = xr_ref[...]
    xc = cb_ref[...]
    for j in range(width):
        xc = xc + xbuf_ref[pl.ds(pad - (width - 1) + j, T), :] * cw_ref[j:j + 1, :]
    xbuf_ref[0:pad, :] = xbuf_ref[T:T + pad, :]

    a, u = _rg_gates(xc, wa_ref, ba_ref, wx_ref, bx_ref, lam_ref)
    a_ref[...] = a
    u_ref[...] = u
    sub = lax.broadcasted_iota(jnp.int32, (SUBLANES, C), 0)

    def group(gi, h):
        start = pl.multiple_of(gi * SUBLANES, SUBLANES)
        ag = a_ref[pl.ds(start, SUBLANES), :]
        ug = u_ref[pl.ds(start, SUBLANES), :]
        for sh in (1, 2, 4):
            ap = jnp.where(sub >= sh, pltpu.roll(ag, sh, 0), 1.0)
            up = jnp.where(sub >= sh, pltpu.roll(ug, sh, 0), 0.0)
            ug = ag * up + ug
            ag = ag * ap
        hg = ag * h + ug
        u_ref[pl.ds(start, SUBLANES), :] = hg
        return hg[SUBLANES - 1:SUBLANES, :]

    h_last = lax.fori_loop(0, T // SUBLANES, group, h_ref[...])
    h_ref[...] = h_last
    y_ref[...] = (u_ref[...] * jax.nn.gelu(gr_ref[...], approximate=True)).astype(y_ref.dtype)

    @pl.when(tstep == pl.num_programs(1) - 1)
    def _():
        h_out[0] = h_last


def _rglru_call(xr, gr, conv_w, conv_b, wa, ba, wx, bx, lam, batch, seq):
    n, c = xr.shape
    width = conv_w.shape[0]
    t = min(RG_TILE, seq)
    assert seq % t == 0 and t % SUBLANES == 0 and width - 1 <= SUBLANES
    nt = seq // t
    row = pl.BlockSpec((t, c), lambda b, s: (b * nt + s, 0))
    est = 6 * t * c * 4 + 2 * c * c * 2 * 2 + 8 * t * c * 4
    return pl.pallas_call(
        functools.partial(_rglru_body, width=width),
        out_shape=[jax.ShapeDtypeStruct((n, c), BF16), jax.ShapeDtypeStruct((batch, 1, c), F32)],
        grid=(batch, nt),
        in_specs=[row, row, _full(conv_w.shape), _full((1, c)), _full(wa.shape), _full((1, c)),
                  _full(wx.shape), _full((1, c)), _full((1, c))],
        out_specs=[row, pl.BlockSpec((1, 1, c), lambda b, s: (b, 0, 0))],
        scratch_shapes=[pltpu.VMEM((t + SUBLANES, c), F32), pltpu.VMEM((t, c), F32),
                        pltpu.VMEM((t, c), F32), pltpu.VMEM((1, c), F32)],
        compiler_params=pltpu.CompilerParams(dimension_semantics=("parallel", "arbitrary"),
                                             vmem_limit_bytes=_vmem_limit(est)),
        name="rglru_prompt",
    )(xr, gr, conv_w, conv_b, wa, ba, wx, bx, lam)


def _to_column(row_vec):
    n = row_vec.shape[1]
    r = lax.broadcasted_iota(jnp.int32, (n, n), 0)
    c = lax.broadcasted_iota(jnp.int32, (n, n), 1)
    return jnp.sum(jnp.where(r == c, jnp.broadcast_to(row_vec, (n, n)), 0.0), axis=-1, keepdims=True)


def _rec_sample_body(qkvo_ref, gates_ref, xr_ref, gr_ref, c0_ref, n0_ref, m0_ref, h0_ref, conv0_ref,
                     bias_ref, ng_ref, cw_ref, cb_ref, wa_ref, ba_ref, wx_ref, bx_ref, lam_ref,
                     hc_ref, y_ref, c_out, n_out, m_out, h_out, conv_out, *, nh, width):
    dc = nh * HEAD_DIM
    g = gates_ref[0] + bias_ref[...]
    m0 = m0_ref[0]
    for h in range(nh):
        cs = slice(h * HEAD_DIM, (h + 1) * HEAD_DIM)
        q = qkvo_ref[0][:, cs]
        k = qkvo_ref[0][:, dc + h * HEAD_DIM:dc + (h + 1) * HEAD_DIM]
        v = qkvo_ref[0][:, 2 * dc + h * HEAD_DIM:2 * dc + (h + 1) * HEAD_DIM]
        o = qkvo_ref[0][:, 3 * dc + h * HEAD_DIM:3 * dc + (h + 1) * HEAD_DIM]
        ig = g[:, h:h + 1]
        lf = _log_sigmoid(g[:, nh + h:nh + h + 1])
        m_prev = m0[:, h:h + 1]
        c_prev = c0_ref[0, h]
        n_prev = n0_ref[0, h:h + 1, :]
        inter = lf + m_prev
        m_t = jnp.maximum(inter, ig)
        w_inter = jnp.exp(inter - m_t)
        s = jnp.sum(q * k, axis=-1, keepdims=True) * jnp.exp(ig - m_t)
        q_col = _to_column(q)
        qc = jnp.sum(q_col * c_prev, axis=0, keepdims=True)
        num = w_inter * qc + s * v
        den = w_inter * jnp.sum(q * n_prev, axis=-1, keepdims=True) + s
        hh = num / jnp.maximum(jnp.abs(den), jnp.exp(-m_t))
        w_old = jnp.exp(lf + m_prev - m_t)
        w_new = jnp.exp(ig - m_t)
        c_out[0, h] = w_old * c_prev + (_to_column(k) * w_new) * v
        n_out[0, h:h + 1, :] = w_old * n_prev + w_new * k
        m_out[0, :, h:h + 1] = m_t
        hc_ref[0, :, cs] = (_rms(hh, ng_ref[:, cs]) * jax.nn.sigmoid(o)).astype(hc_ref.dtype)

    x = xr_ref[0]
    xc = cb_ref[...]
    for j in range(width - 1):
        xc = xc + conv0_ref[0, j:j + 1, :] * cw_ref[j:j + 1, :]
    xc = xc + x * cw_ref[width - 1:width, :]
    a, u = _rg_gates(jnp.broadcast_to(xc, (SUBLANES, xc.shape[1])), wa_ref, ba_ref, wx_ref, bx_ref, lam_ref)
    h_new = a[0:1] * h0_ref[0] + u[0:1]
    h_out[0] = h_new
    y_ref[0] = (h_new * jax.nn.gelu(gr_ref[0], approximate=True)).astype(y_ref.dtype)
    for j in range(width - 2):
        conv_out[0, j:j + 1, :] = conv0_ref[0, j + 1:j + 2, :]
    conv_out[0, width - 2:width - 1, :] = x


def _rec_sample_call(qkvo, gates, xr, gr, c0, n0, m0, h0, conv0, bias, norm_g, conv_w, conv_b,
                     wa, ba, wx, bx, lam):
    db = qkvo.shape[0]
    dc = qkvo.shape[1] // 4
    nh = dc // HEAD_DIM
    c = xr.shape[1]
    width = conv_w.shape[0]
    tok = lambda w: pl.BlockSpec((1, 1, w), lambda b: (b, 0, 0))
    in_specs = [tok(4 * dc), tok(LANES), tok(c), tok(c),
                pl.BlockSpec((1, nh, HEAD_DIM, HEAD_DIM), lambda b: (b, 0, 0, 0)),
                pl.BlockSpec((1, nh, HEAD_DIM), lambda b: (b, 0, 0)),
                tok(nh), tok(c),
                pl.BlockSpec((1, width - 1, c), lambda b: (b, 0, 0)),
                _full((1, LANES)), _full((1, dc)), _full(conv_w.shape), _full((1, c)),
                _full(wa.shape), _full((1, c)), _full(wx.shape), _full((1, c)), _full((1, c))]
    out_shape = [jax.ShapeDtypeStruct((db, 1, dc), BF16), jax.ShapeDtypeStruct((db, 1, c), BF16),
                 jax.ShapeDtypeStruct((db, nh, HEAD_DIM, HEAD_DIM), F32),
                 jax.ShapeDtypeStruct((db, nh, HEAD_DIM), F32),
                 jax.ShapeDtypeStruct((db, 1, nh), F32),
                 jax.ShapeDtypeStruct((db, 1, c), F32),
                 jax.ShapeDtypeStruct((db, width - 1, c), F32)]
    out_specs = [tok(dc), tok(c),
                 pl.BlockSpec((1, nh, HEAD_DIM, HEAD_DIM), lambda b: (b, 0, 0, 0)),
                 pl.BlockSpec((1, nh, HEAD_DIM), lambda b: (b, 0, 0)),
                 tok(nh), tok(c),
                 pl.BlockSpec((1, width - 1, c), lambda b: (b, 0, 0))]
    est = 4 * nh * HEAD_DIM * HEAD_DIM * 4 + 2 * c * c * 2 * 2 + 1024 * 1024
    return pl.pallas_call(
        functools.partial(_rec_sample_body, nh=nh, width=width),
        out_shape=out_shape,
        grid=(db,),
        in_specs=in_specs,
        out_specs=out_specs,
        compiler_params=pltpu.CompilerParams(dimension_semantics=("parallel",),
                                             vmem_limit_bytes=_vmem_limit(est)),
        name="rec_sample",
    )(qkvo.reshape(db, 1, 4 * dc), gates.reshape(db, 1, LANES), xr.reshape(db, 1, c), gr.reshape(db, 1, c),
      c0, n0, m0.reshape(db, 1, nh), h0.reshape(db, 1, c), conv0, bias, norm_g, conv_w, conv_b,
      wa, ba, wx, bx, lam)


def _block_diag(w):
    n, k, j = w.shape
    eye = jnp.eye(n, dtype=w.dtype)
    return (eye[:, None, :, None] * w[:, :, None, :]).reshape(n * k, n * j)


def _lane_rows(vals):
    return jnp.broadcast_to(vals.astype(F32)[:, None, None], (vals.shape[0], 1, LANES))


def kernel(x_prompt, x_sample, cache_k, cache_v, state_mlstm_c, state_mlstm_n, state_mlstm_m, state_rglru_h, state_rglru_conv, page_table, norm_g, ffn_w_gate, ffn_w_up, ffn_w_down, att_w_in, att_w_out, diff_lambda_q1, diff_lambda_k1, diff_lambda_q2, diff_lambda_k2, diff_subln_g, rec_w_in, rec_w_out, mlstm_b_i, mlstm_b_f, mlstm_norm_g, rg_conv_w, rg_conv_b, rg_w_a, rg_b_a, rg_w_x, rg_b_x, rg_lambda, final_norm_g):
    batch, seq, d = x_prompt.shape
    db, dseq, _ = x_sample.shape
    assert dseq == 1, "the sample group holds one new token per sequence"
    depth = norm_g.shape[0]
    n_heads = cache_k.shape[3]
    nh = n_heads // 2
    dc = state_mlstm_c.shape[2] * HEAD_DIM
    nhc = dc // HEAD_DIM
    drg = state_rglru_h.shape[2]
    width = state_rglru_conv.shape[2] + 1

    xp = x_prompt.reshape(batch * seq, d)
    xs = x_sample.reshape(db, d)

    hidx = jnp.arange(n_heads, dtype=F32)
    slopes = 2.0 ** (-8.0 * (hidx + 1.0) / n_heads)
    slopes_a, slopes_b = _lane_rows(slopes[0::2]), _lane_rows(slopes[1::2])

    k_p, v_p, k_s, v_s = [], [], [], []
    rec_p, rec_s = [], []
    mix_p = mix_s = None
    for l in range(depth):
        j = l // 2
        wg, wu, wd = (w[l].astype(BF16) for w in (ffn_w_gate, ffn_w_up, ffn_w_down))
        xp = _ffn_call(xp, norm_g[l, 0], wg[0], wu[0], wd[0])
        xs = _ffn_call(xs, norm_g[l, 0], wg[0], wu[0], wd[0])
        if l % 2 == 0:
            lam_init = 0.8 - 0.6 * math.exp(-0.3 * l)
            w_in = att_w_in[j].astype(BF16)
            w_out = att_w_out[j].astype(BF16)
            lam_vecs = [v[j].reshape(1, HALF_A) for v in
                        (diff_lambda_q1, diff_lambda_k1, diff_lambda_q2, diff_lambda_k2)]
            subg = diff_subln_g[j].reshape(1, HEAD_DIM)
            q, k, v, kb, vb, kbar = _qkv_call(xp, norm_g[l, 1], w_in, prompt=True)
            kbar = kbar.reshape(batch, seq // MOBA_BLOCK, d // 2)
            kbar = jnp.pad(kbar, ((0, 0), (0, LANES - seq // MOBA_BLOCK), (0, 0)))
            oa = _diff_call(q, kb, vb, slopes_a, lam_vecs, subg, batch, seq, lam_init)
            ob = _moba_call(q, kb, vb, kbar, slopes_b, batch, seq)
            mix_p = (oa, ob, w_out[:d // 2], w_out[d // 2:])
            k_p.append(k.reshape(batch, seq, n_heads, HEAD_DIM))
            v_p.append(v.reshape(batch, seq, n_heads, HEAD_DIM))
            qs, ks, vs = _qkv_call(xs, norm_g[l, 1], w_in, prompt=False)
            qs3, ks3, vs3 = (a.reshape(db, 1, d) for a in (qs, ks, vs))
            oa_s, sel = _sample_a_call(page_table, cache_k, cache_v, j, qs3, ks3, vs3, slopes_a,
                                       lam_vecs, subg, lam_init)
            ob_s = _sample_b_call(page_table, sel, cache_k, cache_v, j, qs3, ks3, vs3, slopes_b)
            mix_s = (oa_s.reshape(db, d // 2).astype(BF16), ob_s.reshape(db, d // 2).astype(BF16),
                     w_out[:d // 2], w_out[d // 2:])
            k_s.append(ks.reshape(db, 1, n_heads, HEAD_DIM))
            v_s.append(vs.reshape(db, 1, n_heads, HEAD_DIM))
        else:
            w = rec_w_in[j]
            n_gate = 2 * nhc
            w_packed = jnp.concatenate(
                [w[:, :4 * dc], w[:, 4 * dc + n_gate:],
                 jnp.pad(w[:, 4 * dc:4 * dc + n_gate], ((0, 0), (0, LANES - n_gate)))], axis=1).astype(BF16)
            w_out = rec_w_out[j].astype(BF16)
            bias = jnp.pad(jnp.concatenate([mlstm_b_i[j], mlstm_b_f[j]]), (0, LANES - n_gate)).reshape(1, LANES)
            ng = mlstm_norm_g[j].reshape(1, dc)
            conv_w, conv_b = rg_conv_w[j], rg_conv_b[j].reshape(1, drg)
            wa, wx = _block_diag(rg_w_a[j]).astype(BF16), _block_diag(rg_w_x[j]).astype(BF16)
            ba, bx, lam = (a[j].reshape(1, drg) for a in (rg_b_a, rg_b_x, rg_lambda))
            qkvo, gates, xr, gr = _rec_in_call(xp, norm_g[l, 1], w_packed, dc, drg, BF16)
            hc, c1, n1, m1 = _mlstm_call(qkvo, gates, bias, ng, batch, seq)
            y, h1 = _rglru_call(xr, gr, conv_w, conv_b, wa, ba, wx, bx, lam, batch, seq)
            mix_p = (hc, y, w_out[:dc], w_out[dc:])
            conv1 = xr.reshape(batch, seq, drg)[:, seq - (width - 1):, :]
            rec_p.append((c1, n1.reshape(batch, nhc, HEAD_DIM), m1[:, :, 0, 0], h1.reshape(batch, drg), conv1))
            qkvo_s, gates_s, xr_s, gr_s = _rec_in_call(xs, norm_g[l, 1], w_packed, dc, drg, F32)
            hc_s, y_s, c1s, n1s, m1s, h1s, conv1s = _rec_sample_call(
                qkvo_s, gates_s, xr_s, gr_s, state_mlstm_c[j], state_mlstm_n[j], state_mlstm_m[j],
                state_rglru_h[j], state_rglru_conv[j], bias, ng, conv_w, conv_b, wa, ba, wx, bx, lam)
            mix_s = (hc_s.reshape(db, dc), y_s.reshape(db, drg), w_out[:dc], w_out[dc:])
            rec_s.append((c1s, n1s, m1s.reshape(db, nhc), h1s.reshape(db, drg), conv1s))
        fg = final_norm_g if l == depth - 1 else None
        xp = _ffn_call(xp, norm_g[l, 2], wg[1], wu[1], wd[1], mix=mix_p, final_g=fg)
        xs = _ffn_call(xs, norm_g[l, 2], wg[1], wu[1], wd[1], mix=mix_s, final_g=fg)

    stack = lambda items: jnp.stack(items, axis=0)
    return (xp.reshape(batch, seq, d), xs.reshape(db, 1, d),
            stack(k_p), stack(v_p), stack(k_s), stack(v_s),
            *(stack([st[i] for st in rec_p]) for i in range(5)),
            *(stack([st[i] for st in rec_s]) for i in range(5)))
```

```python
import functools
import math

import jax
import jax.numpy as jnp
from jax import lax
from jax.experimental import pallas as pl
from jax.experimental.pallas import tpu as pltpu

F32 = jnp.float32
BF16 = jnp.bfloat16

HEAD_DIM = 128
HALF_A = HEAD_DIM // 2
MOBA_BLOCK = 256
MOBA_TOPK = 3
MLSTM_CHUNK = 128
RG_C = 8.0
N_BLOCKS_RG = 8
RMS_EPS = 1e-6
NEG = -1e30

V7X_VMEM_BYTES = 64 * 1024 * 1024
LANES = 128
SUBLANES = 8

TOKEN_TILE = 512
FF_CHUNK = 256
ATT_TILE = 256
ATT_HEADS_PER_STEP = 4
RG_TILE = 512
PAGES_PER_STEP = 16


def _vmem_limit(est_bytes):
    return int(min(max(2 * est_bytes, 32 * 1024 * 1024), V7X_VMEM_BYTES - 8 * 1024 * 1024))


def _rms(x, g):
    return x * lax.rsqrt(jnp.mean(x * x, axis=-1, keepdims=True) + RMS_EPS) * g


def _dot(a, b):
    return jnp.dot(a, b, preferred_element_type=F32)


def _dot_nt(a, b):
    return lax.dot_general(a, b, (((1,), (1,)), ((), ())), preferred_element_type=F32)


def _dot_tn(a, b):
    return lax.dot_general(a, b, (((0,), (0,)), ((), ())), preferred_element_type=F32)


def _split_bf16(x):
    hi = x.astype(BF16)
    lo = (x - hi.astype(F32)).astype(BF16)
    return hi, lo


def _full(shape):
    nd = len(shape)
    return pl.BlockSpec(shape, lambda *_: (0,) * nd)


def _ffn_body(*refs, has_mix, has_final, ff_chunk):
    refs = list(refs)
    x_ref = refs.pop(0)
    if has_mix:
        a_ref, b_ref, wa_ref, wb_ref = refs[:4]
        refs = refs[4:]
    g_ref, wg_ref, wu_ref, wd_ref = refs[:4]
    refs = refs[4:]
    if has_final:
        fg_ref = refs.pop(0)
    o_ref, act_ref = refs

    x = x_ref[...]
    if has_mix:
        x = x + _dot(a_ref[...], wa_ref[...]) + _dot(b_ref[...], wb_ref[...])
    h = _rms(x, g_ref[...]).astype(BF16)
    ff = wg_ref.shape[1]
    for c in range(ff // ff_chunk):
        sl = slice(c * ff_chunk, (c + 1) * ff_chunk)
        gate = _dot(h, wg_ref[:, sl])
        up = _dot(h, wu_ref[:, sl])
        act_ref[:, sl] = (gate * jax.nn.sigmoid(gate) * up).astype(BF16)
    y = x + 0.5 * _dot(act_ref[...], wd_ref[...])
    if has_final:
        y = _rms(y, fg_ref[...])
    o_ref[...] = y


def _ffn_call(x, g, wg, wu, wd, mix=None, final_g=None):
    n, d = x.shape
    ff = wg.shape[1]
    tm = min(n, TOKEN_TILE)
    assert n % tm == 0 and ff % FF_CHUNK == 0
    row = lambda w: pl.BlockSpec((tm, w), lambda i: (i, 0))
    args, specs = [x], [row(d)]
    if mix is not None:
        a, b, wa, wb = mix
        args += [a, b, wa, wb]
        specs += [row(a.shape[1]), row(b.shape[1]), _full(wa.shape), _full(wb.shape)]
    args += [g.reshape(1, d), wg, wu, wd]
    specs += [_full((1, d)), _full(wg.shape), _full(wu.shape), _full(wd.shape)]
    if final_g is not None:
        args.append(final_g.reshape(1, d))
        specs.append(_full((1, d)))
    est = 2 * 3 * d * ff * 2 + 4 * tm * d * 4 + tm * ff * 2 + 4 * tm * FF_CHUNK * 4 + 2 * tm * d * 4
    return pl.pallas_call(
        functools.partial(_ffn_body, has_mix=mix is not None, has_final=final_g is not None,
                          ff_chunk=FF_CHUNK),
        out_shape=jax.ShapeDtypeStruct((n, d), F32),
        grid=(n // tm,),
        in_specs=specs,
        out_specs=row(d),
        scratch_shapes=[pltpu.VMEM((tm, ff), BF16)],
        compiler_params=pltpu.CompilerParams(dimension_semantics=("parallel",),
                                             vmem_limit_bytes=_vmem_limit(est)),
        name="ffn",
    )(*args)


def _qkv_body(x_ref, g_ref, w_ref, *out_refs, prompt):
    x = x_ref[...]
    d = x.shape[1]
    h = _rms(x, g_ref[...]).astype(BF16)
    q = _dot(h, w_ref[:, 0:d])
    k = _dot(h, w_ref[:, d:2 * d])
    v = _dot(h, w_ref[:, 2 * d:3 * d])
    if not prompt:
        q_ref, k_ref, v_ref = out_refs
        q_ref[...] = q
        k_ref[...] = k
        v_ref[...] = v
        return
    q_ref, k_ref, v_ref, kb_ref, vb_ref, kbar_ref = out_refs
    q_ref[...] = q.astype(BF16)
    k_ref[...] = k
    v_ref[...] = v
    lane = lax.broadcasted_iota(jnp.int32, (1, d), 1)
    scale = jnp.where(lane < d // 2, HALF_A ** -0.5, HEAD_DIM ** -0.5).astype(F32)
    kb_ref[...] = (k * scale).astype(BF16)
    vb_ref[...] = v.astype(BF16)
    tm = x.shape[0]
    for r in range(tm // MOBA_BLOCK):
        blk = k[r * MOBA_BLOCK:(r + 1) * MOBA_BLOCK, d // 2:]
        kbar_ref[0, r:r + 1, :] = jnp.mean(blk, axis=0, keepdims=True)


def _qkv_call(x, g, w_in, prompt):
    n, d = x.shape
    tm = min(n, TOKEN_TILE)
    assert n % tm == 0
    row = lambda w: pl.BlockSpec((tm, w), lambda i: (i, 0))
    if prompt:
        assert tm % MOBA_BLOCK == 0
        nb = tm // MOBA_BLOCK
        out_shape = [jax.ShapeDtypeStruct((n, d), BF16), jax.ShapeDtypeStruct((n, d), F32),
                     jax.ShapeDtypeStruct((n, d), F32), jax.ShapeDtypeStruct((n, d), BF16),
                     jax.ShapeDtypeStruct((n, d), BF16),
                     jax.ShapeDtypeStruct((n // tm, nb, d // 2), F32)]
        out_specs = [row(d)] * 5 + [pl.BlockSpec((1, nb, d // 2), lambda i: (i, 0, 0))]
    else:
        out_shape = [jax.ShapeDtypeStruct((n, d), F32)] * 3
        out_specs = [row(d)] * 3
    est = 2 * d * 3 * d * 2 + 2 * tm * d * 4 * 4 + 3 * tm * d * 4
    return pl.pallas_call(
        functools.partial(_qkv_body, prompt=prompt),
        out_shape=out_shape,
        grid=(n // tm,),
        in_specs=[row(d), _full((1, d)), _full(w_in.shape)],
        out_specs=out_specs,
        compiler_params=pltpu.CompilerParams(dimension_semantics=("parallel",),
                                             vmem_limit_bytes=_vmem_limit(est)),
        name="qkv_prompt" if prompt else "qkv_sample",
    )(x, g.reshape(1, d), w_in)


def _lanes(x, width):
    return x if width == LANES else jnp.concatenate([x] * (width // LANES), axis=1)


def _softmax_tile(s, shift, v_tile, ones_col, m_ref, acc_ref):
    m_old = m_ref[...]
    m_new = jnp.maximum(m_old, jnp.max(s, axis=-1, keepdims=True) + shift)
    p = jnp.exp(s - _lanes(m_new - shift, s.shape[1]))
    alpha = jnp.exp(m_old - m_new)
    va = jnp.concatenate([v_tile, ones_col], axis=1)
    acc_ref[...] = _lanes(alpha, 2 * LANES) * acc_ref[...] + _dot(p.astype(BF16), va)
    m_ref[...] = m_new


def _softmax_result(acc_ref):
    acc = acc_ref[...]
    return acc[:, 0:HEAD_DIM] / acc[:, HEAD_DIM:HEAD_DIM + 1]


def _chunk_shift(slope, delta):
    return slope * (jnp.zeros((1, 1), jnp.int32) + delta).astype(F32)


def _offset_cols(tk, lo_lane, hi_lane):
    lane = lax.broadcasted_iota(jnp.int32, (tk, LANES), 1)
    c = lax.broadcasted_iota(jnp.int32, (tk, LANES), 0)
    c_lo = jnp.bitwise_and(c, 255)
    return jnp.where(lane == lo_lane, c_lo, jnp.where(lane == hi_lane, c - c_lo, 0)).astype(F32)


def _key_chunk(seq):
    return next(tk for tk in (1024, 512, 256) if seq % tk == 0)


def _diff_lambda(lq1_ref, lk1_ref, lq2_ref, lk2_ref, lam_init):
    a = jnp.exp(jnp.sum(lq1_ref[...] * lk1_ref[...], axis=-1, keepdims=True))
    b = jnp.exp(jnp.sum(lq2_ref[...] * lk2_ref[...], axis=-1, keepdims=True))
    return a - b + lam_init


def _diff_body(slope_ref, q_ref, k_ref, v_ref, lq1_ref, lk1_ref, lq2_ref, lk2_ref, subg_ref,
               o_ref, qa_ref, m_ref, acc_ref, *, t, tk, lam_init):
    i = pl.program_id(2)
    hp = qa_ref.shape[0]
    lane = lax.broadcasted_iota(jnp.int32, (t, HEAD_DIM), 1)
    lane_k = lax.broadcasted_iota(jnp.int32, (tk, LANES), 1)
    ones_col = jnp.where(lane_k == 0, 1.0, 0.0).astype(BF16)
    kext = _offset_cols(tk, 0, 1).astype(BF16)
    slopes = []
    for hh in range(hp):
        cs_h = slice(hh * HEAD_DIM, (hh + 1) * HEAD_DIM)
        q = q_ref[:, cs_h]
        zero = jnp.zeros_like(q)
        slopes.append(slope_ref[hh][:, 0:1])
        q_ext = jnp.where(lane < 2, slopes[hh], 0.0).astype(BF16)
        qa_ref[hh, 0:t, 0:HEAD_DIM] = jnp.where(lane < HALF_A, q, zero)
        qa_ref[hh, t:2 * t, 0:HEAD_DIM] = jnp.where(lane >= HALF_A, q, zero)
        qa_ref[hh, 0:t, HEAD_DIM:] = q_ext
        qa_ref[hh, t:2 * t, HEAD_DIM:] = q_ext

    m_ref[...] = jnp.full(m_ref.shape, NEG, F32)
    acc_ref[...] = jnp.zeros(acc_ref.shape, F32)
    q0 = i * t

    def chunk(cs, causal):
        for hh in range(hp):
            cs_h = slice(hh * HEAD_DIM, (hh + 1) * HEAD_DIM)
            ka = jnp.concatenate([k_ref[pl.ds(cs, tk), cs_h], kext], axis=1)
            s = _dot_nt(qa_ref[hh], ka)
            if causal:
                r2 = lax.broadcasted_iota(jnp.int32, (2 * t, tk), 0)
                c2 = lax.broadcasted_iota(jnp.int32, (2 * t, tk), 1)
                r2 = jnp.where(r2 >= t, r2 - t, r2)
                s = jnp.where(c2 + (cs - q0) <= r2, s, NEG)
            _softmax_tile(s, _chunk_shift(slopes[hh], cs - q0), v_ref[pl.ds(cs, tk), cs_h], ones_col,
                          m_ref.at[hh], acc_ref.at[hh])

    def body(j, carry):
        chunk(pl.multiple_of(j * tk, tk), False)
        return carry

    n_full = q0 // tk
    lax.fori_loop(0, n_full, body, 0)
    chunk(pl.multiple_of(n_full * tk, tk), True)

    lam = _diff_lambda(lq1_ref, lk1_ref, lq2_ref, lk2_ref, lam_init)
    for hh in range(hp):
        o = _softmax_result(acc_ref.at[hh])
        oa = o[0:t] - lam * o[t:2 * t]
        o_ref[:, hh * HEAD_DIM:(hh + 1) * HEAD_DIM] = (
            _rms(oa, subg_ref[...]) * (1.0 - lam_init)).astype(BF16)


def _diff_call(q, kb, vb, slopes, lam_vecs, subg, batch, seq, lam_init):
    n, d = q.shape
    nh = d // 2 // HEAD_DIM
    t = ATT_TILE
    tk = _key_chunk(seq)
    assert seq % t == 0 and tk % t == 0
    nq = seq // t
    hp = ATT_HEADS_PER_STEP
    assert nh % hp == 0
    vec = lambda w: _full((1, w))
    est = hp * (2 * 2 * seq * HEAD_DIM * 2 + 2 * t * 2 * HEAD_DIM * 2 + 2 * t * 3 * LANES * 4 + 6 * 2 * t * tk * 4)
    return pl.pallas_call(
        functools.partial(_diff_body, t=t, tk=tk, lam_init=lam_init),
        out_shape=jax.ShapeDtypeStruct((n, d // 2), BF16),
        grid=(batch, nh // hp, nq),
        in_specs=[pl.BlockSpec((hp, 1, LANES), lambda b, h, i: (h, 0, 0)),
                  pl.BlockSpec((t, hp * HEAD_DIM), lambda b, h, i: (b * nq + i, h)),
                  pl.BlockSpec((seq, hp * HEAD_DIM), lambda b, h, i: (b, h)),
                  pl.BlockSpec((seq, hp * HEAD_DIM), lambda b, h, i: (b, h)),
                  vec(HALF_A), vec(HALF_A), vec(HALF_A), vec(HALF_A), vec(HEAD_DIM)],
        out_specs=pl.BlockSpec((t, hp * HEAD_DIM), lambda b, h, i: (b * nq + i, h)),
        scratch_shapes=[pltpu.VMEM((hp, 2 * t, 2 * HEAD_DIM), BF16), pltpu.VMEM((hp, 2 * t, LANES), F32),
                        pltpu.VMEM((hp, 2 * t, 2 * LANES), F32)],
        compiler_params=pltpu.CompilerParams(dimension_semantics=("parallel", "parallel", "parallel"),
                                             vmem_limit_bytes=_vmem_limit(est)),
        name="diff_attn_prompt",
    )(slopes, q, kb, vb, *lam_vecs, subg)


def _moba_body(slope_ref, q_ref, k_ref, v_ref, kbar_ref, o_ref, qa_ref, m_ref, acc_ref,
               *, t, tk, n_blocks):
    i = pl.program_id(2)
    hp = qa_ref.shape[0]
    lane = lax.broadcasted_iota(jnp.int32, (t, LANES), 1)
    nbp = -(-n_blocks // SUBLANES) * SUBLANES
    blk = lax.broadcasted_iota(jnp.int32, (nbp, t), 0)
    slopes = []
    for hh in range(hp):
        cs_h = slice(hh * HEAD_DIM, (hh + 1) * HEAD_DIM)
        slope = slope_ref[hh][:, 0:1]
        slopes.append(slope)
        q = q_ref[:, cs_h]
        kbar_hi, kbar_lo = _split_bf16(kbar_ref[0, :, cs_h])
        gate = (_dot_nt(kbar_hi, q) + _dot_nt(kbar_lo, q))[0:nbp]
        gate = jnp.where(blk < i, gate, NEG)
        cnt = jnp.zeros((nbp, t), jnp.int32)
        for n2 in range(n_blocks):
            gn = gate[n2:n2 + 1, :]
            beats = (gn > gate) | ((gn == gate) & (blk > n2))
            cnt = cnt + jnp.where(beats & (i > n2), 1, 0)
        keep = ((blk < i) & (cnt < MOBA_TOPK)) | (blk == i)
        pen = jnp.concatenate([jnp.where(keep, 0.0, NEG), jnp.full((LANES - nbp, t), NEG, F32)], axis=0).T
        qa_ref[hh, :, 0:HEAD_DIM] = q
        qa_ref[hh, :, HEAD_DIM:] = jnp.where(lane >= LANES - 2, slope, pen).astype(BF16)
    offs = _offset_cols(tk, LANES - 2, LANES - 1)
    lane_k = lax.broadcasted_iota(jnp.int32, (tk, LANES), 1)
    row_k = lax.broadcasted_iota(jnp.int32, (tk, LANES), 0)
    blk_off = lane_k - row_k // MOBA_BLOCK
    ones_col = jnp.where(lane_k == 0, 1.0, 0.0).astype(BF16)

    m_ref[...] = jnp.full(m_ref.shape, NEG, F32)
    acc_ref[...] = jnp.zeros(acc_ref.shape, F32)
    q0 = i * t

    def chunk(cs, causal):
        kext = jnp.where(blk_off == cs // MOBA_BLOCK, 1.0, offs).astype(BF16)
        for hh in range(hp):
            cs_h = slice(hh * HEAD_DIM, (hh + 1) * HEAD_DIM)
            ka = jnp.concatenate([k_ref[pl.ds(cs, tk), cs_h], kext], axis=1)
            s = _dot_nt(qa_ref[hh], ka)
            if causal:
                r2 = lax.broadcasted_iota(jnp.int32, (t, tk), 0)
                c2 = lax.broadcasted_iota(jnp.int32, (t, tk), 1)
                s = jnp.where(c2 + (cs - q0) <= r2, s, NEG)
            _softmax_tile(s, _chunk_shift(slopes[hh], cs - q0), v_ref[pl.ds(cs, tk), cs_h], ones_col,
                          m_ref.at[hh], acc_ref.at[hh])

    def body(j, carry):
        chunk(pl.multiple_of(j * tk, tk), False)
        return carry

    n_full = q0 // tk
    lax.fori_loop(0, n_full, body, 0)
    chunk(pl.multiple_of(n_full * tk, tk), True)
    for hh in range(hp):
        o_ref[:, hh * HEAD_DIM:(hh + 1) * HEAD_DIM] = _softmax_result(acc_ref.at[hh]).astype(BF16)


def _moba_call(q, kb, vb, kbar, slopes, batch, seq):
    n, d = q.shape
    nh = d // 2 // HEAD_DIM
    t = MOBA_BLOCK
    tk = _key_chunk(seq)
    assert seq % t == 0
    nq = seq // t
    assert nq <= LANES - 2
    hp = ATT_HEADS_PER_STEP
    assert nh % hp == 0
    ng = nh // hp
    est = hp * (2 * 2 * seq * HEAD_DIM * 2 + 2 * t * 2 * HEAD_DIM * 2 + t * 3 * LANES * 4 + 8 * t * tk * 4)
    return pl.pallas_call(
        functools.partial(_moba_body, t=t, tk=tk, n_blocks=nq),
        out_shape=jax.ShapeDtypeStruct((n, d // 2), BF16),
        grid=(batch, ng, nq),
        in_specs=[pl.BlockSpec((hp, 1, LANES), lambda b, h, i: (h, 0, 0)),
                  pl.BlockSpec((t, hp * HEAD_DIM), lambda b, h, i: (b * nq + i, ng + h)),
                  pl.BlockSpec((seq, hp * HEAD_DIM), lambda b, h, i: (b, ng + h)),
                  pl.BlockSpec((seq, hp * HEAD_DIM), lambda b, h, i: (b, ng + h)),
                  pl.BlockSpec((1, LANES, hp * HEAD_DIM), lambda b, h, i: (b, 0, h))],
        out_specs=pl.BlockSpec((t, hp * HEAD_DIM), lambda b, h, i: (b * nq + i, h)),
        scratch_shapes=[pltpu.VMEM((hp, t, 2 * HEAD_DIM), BF16), pltpu.VMEM((hp, t, LANES), F32),
                        pltpu.VMEM((hp, t, 2 * LANES), F32)],
        compiler_params=pltpu.CompilerParams(dimension_semantics=("parallel", "parallel", "parallel"),
                                             vmem_limit_bytes=_vmem_limit(est)),
        name="moba_attn_prompt",
    )(slopes, q, kb, vb, kbar)


HEAD_ROWS = 16


def _head_rows(page_ref, h, page, n_heads):
    return page_ref[0, pl.ds(h, page, stride=n_heads), :]


def _sample_a_body(pt_ref, *refs, pps, n_pages, nh, lam_init):
    del pt_ref
    kp = refs[:pps]
    vp = refs[pps:2 * pps]
    (q_ref, kn_ref, vn_ref, slope_ref, lq1_ref, lk1_ref, lq2_ref, lk2_ref, subg_ref,
     oa_ref, sel_ref, w_ref, m_ref, acc_ref, gate_ref) = refs[2 * pps:]
    c = pl.program_id(1)
    n_heads = 2 * nh
    page = kp[0].shape[1] // n_heads
    past = n_pages * page
    hr = HEAD_ROWS
    rows = nh * hr

    @pl.when(c == 0)
    def _():
        r = lax.broadcasted_iota(jnp.int32, (hr, HEAD_DIM), 0)
        ln = lax.broadcasted_iota(jnp.int32, (hr, HEAD_DIM), 1)
        for h in range(n_heads):
            qb = jnp.broadcast_to(q_ref[0, h:h + 1, :], (hr, HEAD_DIM))
            if h < nh:
                keep = ((r == 0) & (ln < HALF_A)) | ((r == 1) & (ln >= HALF_A))
                w = jnp.where(keep, qb * (HALF_A ** -0.5), 0.0)
            else:
                hi = qb.astype(BF16).astype(F32)
                w = jnp.where(r == 0, hi, jnp.where(r == 1, qb - hi, 0.0))
            w_ref[h * hr:(h + 1) * hr, :] = w.astype(BF16)
        m_ref[...] = jnp.full(m_ref.shape, NEG, F32)
        acc_ref[...] = jnp.zeros(acc_ref.shape, F32)
        gate_ref[...] = jnp.zeros(gate_ref.shape, F32)

    glane = lax.broadcasted_iota(jnp.int32, (hr, LANES), 1)
    s_heads = [[] for _ in range(nh)]
    gate_add = [jnp.zeros((hr, LANES), F32) for _ in range(nh)]
    for p in range(pps):
        blk = ((c * pps + p) * page) // MOBA_BLOCK
        for h in range(n_heads):
            kh = _head_rows(kp[p], h, page, n_heads).astype(BF16)
            s = _dot_nt(w_ref[h * hr:(h + 1) * hr, :], kh)
            if h < nh:
                s_heads[h].append(s)
            else:
                gate_add[h - nh] = gate_add[h - nh] + jnp.where(
                    glane == blk, jnp.sum(s, axis=-1, keepdims=True), 0.0)
    gate_ref[...] += jnp.concatenate(gate_add, axis=0)
    width = pps * page
    s = jnp.concatenate([jnp.concatenate(sh, axis=1) for sh in s_heads], axis=0)
    kpos = c * width + lax.broadcasted_iota(jnp.int32, (1, width), 1)
    dist = (past - kpos).astype(F32)
    slope_rows = jnp.concatenate(
        [jnp.broadcast_to(slope_ref[h][:, 0:1], (hr, 1)) for h in range(nh)], axis=0)
    s = s - slope_rows * dist
    m_old = m_ref[...]
    m_new = jnp.maximum(m_old, jnp.max(s, axis=-1, keepdims=True))
    pr = jnp.exp(s - m_new[:, 0:1])
    alpha = jnp.exp(m_old - m_new)
    pv_heads = []
    for h in range(nh):
        pv = jnp.zeros((hr, HEAD_DIM), F32)
        for p in range(pps):
            vh = _head_rows(vp[p], h, page, n_heads).astype(BF16)
            pv = pv + _dot(pr[h * hr:(h + 1) * hr, p * page:(p + 1) * page].astype(BF16), vh)
        pv_heads.append(pv)
    acc_ref[:, 0:HEAD_DIM] = alpha * acc_ref[:, 0:HEAD_DIM] + jnp.concatenate(pv_heads, axis=0)
    acc_ref[:, HEAD_DIM:] = alpha * acc_ref[:, HEAD_DIM:] + jnp.sum(pr, axis=-1, keepdims=True)
    m_ref[...] = m_new

    @pl.when(c == pl.num_programs(1) - 1)
    def _():
        wf = w_ref[0:rows, :].astype(F32)
        kn = jnp.concatenate([jnp.broadcast_to(kn_ref[0, h:h + 1, :], (hr, HEAD_DIM)) for h in range(nh)], axis=0)
        vn = jnp.concatenate([jnp.broadcast_to(vn_ref[0, h:h + 1, :], (hr, HEAD_DIM)) for h in range(nh)], axis=0)
        s_new = jnp.sum(wf * kn, axis=-1, keepdims=True)
        m_old = m_ref[...]
        m_new = jnp.maximum(m_old, s_new)
        p_new = jnp.exp(s_new - m_new)
        alpha = jnp.exp(m_old - m_new)
        l = alpha * acc_ref[:, HEAD_DIM:] + p_new
        o = (alpha * acc_ref[:, 0:HEAD_DIM] + p_new * vn) / l
        lam = _diff_lambda(lq1_ref, lk1_ref, lq2_ref, lk2_ref, lam_init)
        for h in range(nh):
            oa = o[h * hr:h * hr + 1, :] - lam * o[h * hr + 1:h * hr + 2, :]
            oa_ref[0, :, h * HEAD_DIM:(h + 1) * HEAD_DIM] = _rms(oa, subg_ref[...]) * (1.0 - lam_init)
        n_past = past // MOBA_BLOCK
        g = jnp.concatenate([gate_ref[h * hr:h * hr + 1, :] + gate_ref[h * hr + 1:h * hr + 2, :]
                             for h in range(nh)], axis=0) * (1.0 / MOBA_BLOCK)
        gl = lax.broadcasted_iota(jnp.int32, g.shape, 1)
        g = jnp.where(gl < n_past, g, NEG)
        cnt = jnp.zeros(g.shape, jnp.int32)
        for n2 in range(n_past):
            gn = g[:, n2:n2 + 1]
            cnt = cnt + jnp.where((gn > g) | ((gn == g) & (gl > n2)), 1, 0)
        sel = jnp.zeros(g.shape, jnp.int32)
        for j in range(MOBA_TOPK):
            idx = jnp.sum(jnp.where((cnt == j) & (gl < n_past), gl, 0), axis=-1, keepdims=True)
            sel = jnp.where(gl == j, idx, sel)
        sel_ref[0] = sel


def _sample_a_call(page_table, cache_k, cache_v, layer, q, k_new, v_new, slopes, lam_vecs, subg, lam_init):
    db, n_pages = page_table.shape
    n_layers, n_pool, page, n_heads, hd = cache_k.shape
    d = n_heads * hd
    nh = n_heads // 2
    assert (n_pages * page) % MOBA_BLOCK == 0 and MOBA_BLOCK % page == 0
    assert n_pages * page // MOBA_BLOCK <= LANES
    pps = math.gcd(PAGES_PER_STEP, n_pages)
    ck = cache_k.reshape(n_layers * n_pool, page * n_heads, hd)
    cv = cache_v.reshape(n_layers * n_pool, page * n_heads, hd)
    off = layer * n_pool

    def page_spec(p):
        return pl.BlockSpec((1, page * n_heads, hd), lambda b, c, pt: (pt[b, c * pps + p] + off, 0, 0))

    tok = pl.BlockSpec((1, n_heads, hd), lambda b, c, pt: (b, 0, 0))
    fixed = lambda shape: pl.BlockSpec(shape, lambda b, c, pt: (0,) * len(shape))
    in_specs = ([page_spec(p) for p in range(pps)] * 2
                + [tok, tok, tok, fixed((nh, 1, LANES)),
                   fixed((1, HALF_A)), fixed((1, HALF_A)), fixed((1, HALF_A)), fixed((1, HALF_A)),
                   fixed((1, HEAD_DIM))])
    rows = nh * HEAD_ROWS
    est = 2 * 2 * pps * page * d * 4 + 6 * rows * pps * page * 4
    to_heads = lambda a: a.reshape(db, n_heads, hd)
    oa, sel = pl.pallas_call(
        functools.partial(_sample_a_body, pps=pps, n_pages=n_pages, nh=nh, lam_init=lam_init),
        out_shape=[jax.ShapeDtypeStruct((db, 1, d // 2), F32),
                   jax.ShapeDtypeStruct((db, nh, LANES), jnp.int32)],
        grid_spec=pltpu.PrefetchScalarGridSpec(
            num_scalar_prefetch=1,
            grid=(db, n_pages // pps),
            in_specs=in_specs,
            out_specs=[pl.BlockSpec((1, 1, d // 2), lambda b, c, pt: (b, 0, 0)),
                       pl.BlockSpec((1, nh, LANES), lambda b, c, pt: (b, 0, 0))],
            scratch_shapes=[pltpu.VMEM((n_heads * HEAD_ROWS, HEAD_DIM), BF16), pltpu.VMEM((rows, LANES), F32),
                            pltpu.VMEM((rows, 2 * LANES), F32), pltpu.VMEM((rows, LANES), F32)]),
        compiler_params=pltpu.CompilerParams(dimension_semantics=("parallel", "arbitrary"),
                                             vmem_limit_bytes=_vmem_limit(est)),
        name="sample_attn_a",
    )(page_table, *([ck] * pps), *([cv] * pps), to_heads(q), to_heads(k_new), to_heads(v_new),
      slopes, *lam_vecs, subg)
    return oa, sel


def _sample_b_body(pt_ref, sel_ref, *refs, n_sel, ppb, n_pages, nh):
    del pt_ref
    n_op = n_sel * ppb
    kp = refs[:n_op]
    vp = refs[n_op:2 * n_op]
    q_ref, kn_ref, vn_ref, slope_ref, o_ref = refs[2 * n_op:]
    b = pl.program_id(0)
    h = pl.program_id(1)
    n_heads = 2 * nh
    page = kp[0].shape[1] // n_heads
    past = n_pages * page
    slope = slope_ref[0][:, 0:1]
    scale = HEAD_DIM ** -0.5
    q = jnp.broadcast_to(q_ref[0], (SUBLANES, HEAD_DIM)).astype(BF16)
    lane = lax.broadcasted_iota(jnp.int32, (SUBLANES, page), 1)
    scores = []
    for j in range(n_sel):
        blk = sel_ref[b, h * n_sel + j]
        for r in range(ppb):
            kh = _head_rows(kp[j * ppb + r], nh + h, page, n_heads).astype(BF16)
            s = _dot_nt(q, kh) * scale
            dist = (past - (blk * MOBA_BLOCK + r * page) - lane).astype(F32)
            scores.append(s - slope * dist)
    s_new = jnp.sum(q_ref[0] * kn_ref[0], axis=-1, keepdims=True) * scale
    m = s_new
    if scores:
        s = jnp.concatenate(scores, axis=1)
        m = jnp.maximum(jnp.max(s, axis=-1, keepdims=True)[0:1], s_new)
        pr = jnp.exp(s - m)
        l = jnp.sum(pr, axis=-1, keepdims=True)[0:1]
        acc = jnp.zeros((SUBLANES, HEAD_DIM), F32)
        for j in range(n_op):
            vh = _head_rows(vp[j], nh + h, page, n_heads).astype(BF16)
            acc = acc + _dot(pr[:, j * page:(j + 1) * page].astype(BF16), vh)
        acc = acc[0:1]
    else:
        l = jnp.zeros((1, 1), F32)
        acc = jnp.zeros((1, HEAD_DIM), F32)
    p_new = jnp.exp(s_new - m)
    o_ref[0] = (acc + p_new * vn_ref[0]) / (l + p_new)


def _sample_b_call(page_table, sel, cache_k, cache_v, layer, q, k_new, v_new, slopes):
    db, n_pages = page_table.shape
    n_layers, n_pool, page, n_heads, hd = cache_k.shape
    d = n_heads * hd
    nh = n_heads // 2
    ppb = MOBA_BLOCK // page
    n_past = n_pages * page // MOBA_BLOCK
    n_sel = min(MOBA_TOPK, n_past)
    ck = cache_k.reshape(n_layers * n_pool, page * n_heads, hd)
    cv = cache_v.reshape(n_layers * n_pool, page * n_heads, hd)
    off = layer * n_pool

    def page_spec(j, r):
        return pl.BlockSpec((1, page * n_heads, hd),
                            lambda b, h, pt, sl: (pt[b, sl[b, h * n_sel + j] * ppb + r] + off, 0, 0))

    tok = pl.BlockSpec((1, 1, hd), lambda b, h, pt, sl: (b, 0, nh + h))
    specs = [page_spec(j, r) for j in range(n_sel) for r in range(ppb)]
    in_specs = specs + specs + [tok, tok, tok, pl.BlockSpec((1, 1, LANES), lambda b, h, pt, sl: (h, 0, 0))]
    n_op = n_sel * ppb
    est = 2 * 2 * n_op * page * d * 4 + 64 * 1024
    return pl.pallas_call(
        functools.partial(_sample_b_body, n_sel=n_sel, ppb=ppb, n_pages=n_pages, nh=nh),
        out_shape=jax.ShapeDtypeStruct((db, 1, d // 2), F32),
        grid_spec=pltpu.PrefetchScalarGridSpec(
            num_scalar_prefetch=2,
            grid=(db, nh),
            in_specs=in_specs,
            out_specs=pl.BlockSpec((1, 1, hd), lambda b, h, pt, sl: (b, 0, h))),
        compiler_params=pltpu.CompilerParams(dimension_semantics=("parallel", "parallel"),
                                             vmem_limit_bytes=_vmem_limit(est)),
        name="sample_attn_b",
    )(page_table, sel[:, :, :n_sel].reshape(db, nh * n_sel), *([ck] * n_op), *([cv] * n_op),
      q, k_new, v_new, slopes)


def _rec_in_body(x_ref, g_ref, w_ref, qkvo_ref, gates_ref, xr_ref, gr_ref, *, dc, drg):
    x = x_ref[...]
    h = _rms(x, g_ref[...]).astype(BF16)
    kscale = HEAD_DIM ** -0.5
    qkvo_ref[:, 0:dc] = _dot(h, w_ref[:, 0:dc]).astype(qkvo_ref.dtype)
    qkvo_ref[:, dc:2 * dc] = (_dot(h, w_ref[:, dc:2 * dc]) * kscale).astype(qkvo_ref.dtype)
    qkvo_ref[:, 2 * dc:3 * dc] = _dot(h, w_ref[:, 2 * dc:3 * dc]).astype(qkvo_ref.dtype)
    qkvo_ref[:, 3 * dc:4 * dc] = _dot(h, w_ref[:, 3 * dc:4 * dc]).astype(qkvo_ref.dtype)
    xr_ref[...] = _dot(h, w_ref[:, 4 * dc:4 * dc + drg])
    gr_ref[...] = _dot(h, w_ref[:, 4 * dc + drg:4 * dc + 2 * drg])
    gates_ref[...] = _dot(h, w_ref[:, 4 * dc + 2 * drg:])


def _rec_in_call(x, g, w_packed, dc, drg, qkvo_dtype):
    n, d = x.shape
    tm = min(n, TOKEN_TILE)
    assert n % tm == 0
    row = lambda w: pl.BlockSpec((tm, w), lambda i: (i, 0))
    est = 2 * w_packed.size * 2 + 2 * tm * d * 4 + 2 * tm * (4 * dc + 2 * drg + LANES) * 4
    return pl.pallas_call(
        functools.partial(_rec_in_body, dc=dc, drg=drg),
        out_shape=[jax.ShapeDtypeStruct((n, 4 * dc), qkvo_dtype), jax.ShapeDtypeStruct((n, LANES), F32),
                   jax.ShapeDtypeStruct((n, drg), F32), jax.ShapeDtypeStruct((n, drg), F32)],
        grid=(n // tm,),
        in_specs=[row(d), _full((1, d)), _full(w_packed.shape)],
        out_specs=[row(4 * dc), row(LANES), row(drg), row(drg)],
        compiler_params=pltpu.CompilerParams(dimension_semantics=("parallel",),
                                             vmem_limit_bytes=_vmem_limit(est)),
        name="rec_in",
    )(x, g.reshape(1, d), w_packed)


def _log_sigmoid(x):
    return jnp.minimum(x, 0.0) - jnp.log1p(jnp.exp(-jnp.abs(x)))


def _softplus(x):
    return jnp.maximum(x, 0.0) + jnp.log1p(jnp.exp(-jnp.abs(x)))


def _mlstm_body(qkvo_ref, gates_ref, bias_ref, ng_ref, hc_ref, c_out, n_out, m_out,
                c_ref, n_ref, m_ref, *, nh):
    ch = pl.program_id(1)
    L = qkvo_ref.shape[0]
    dc = nh * HEAD_DIM

    @pl.when(ch == 0)
    def _():
        c_ref[...] = jnp.zeros(c_ref.shape, F32)
        n_ref[...] = jnp.zeros(n_ref.shape, F32)
        m_ref[...] = jnp.zeros(m_ref.shape, F32)

    lane = lax.broadcasted_iota(jnp.int32, (L, LANES), 1)
    g = gates_ref[...] + bias_ref[...]
    x = jnp.where(lane < nh, g, jnp.where(lane < 2 * nh, _log_sigmoid(g), 0.0))
    tt = lax.broadcasted_iota(jnp.int32, (L, L), 0)
    ss = lax.broadcasted_iota(jnp.int32, (L, L), 1)
    causal = ss <= tt
    tril = jnp.where(causal, 1.0, 0.0).astype(BF16)
    x_hi, x_lo = _split_bf16(x)
    x_mid, x_lo = _split_bf16(x - x_hi.astype(F32))
    cum = _dot(tril, x_hi) + _dot(tril, x_mid) + _dot(tril, x_lo)
    colv = jnp.where(lane < nh, x, cum)
    rowv = colv.T

    for h in range(nh):
        cs = slice(h * HEAD_DIM, (h + 1) * HEAD_DIM)
        q = qkvo_ref[:, cs]
        k = qkvo_ref[:, dc + h * HEAD_DIM:dc + (h + 1) * HEAD_DIM]
        v = qkvo_ref[:, 2 * dc + h * HEAD_DIM:2 * dc + (h + 1) * HEAD_DIM]
        o = qkvo_ref[:, 3 * dc + h * HEAD_DIM:3 * dc + (h + 1) * HEAD_DIM].astype(F32)
        ig_col = colv[:, h:h + 1]
        b_col = colv[:, nh + h:nh + h + 1]
        ig_row = rowv[h:h + 1, :]
        b_row = rowv[nh + h:nh + h + 1, :]
        m_prev = m_ref[h][:, 0:1]
        c_prev = c_ref[h]
        n_prev = n_ref[h]

        dmat = jnp.where(causal, b_col - b_row + ig_row, NEG)
        inter = b_col + m_prev
        m_t = jnp.maximum(inter, jnp.max(dmat, axis=-1, keepdims=True))
        w_inter = jnp.exp(inter - m_t)
        s = _dot_nt(q, k) * jnp.exp(dmat - m_t)
        qf = q.astype(F32)
        num = w_inter * _dot(q, c_prev.astype(BF16)) + _dot(s.astype(BF16), v)
        den = w_inter * jnp.sum(qf * n_prev, axis=-1, keepdims=True) + jnp.sum(s, axis=-1, keepdims=True)
        hh = num / jnp.maximum(jnp.abs(den), jnp.exp(-m_t))
        m_new = m_t[L - 1:L, :]
        b_last = b_col[L - 1:L, :]
        w_old = jnp.exp(b_last + m_prev - m_new)
        w_new = jnp.exp(b_last - b_col + ig_col - m_new)
        kw = k.astype(F32) * w_new
        c_ref[h] = w_old * c_prev + _dot_tn(kw.astype(BF16), v)
        n_ref[h] = w_old * n_prev + jnp.sum(kw, axis=0, keepdims=True)
        m_ref[h] = jnp.broadcast_to(m_new, (1, LANES))
        hc_ref[:, cs] = (_rms(hh, ng_ref[:, cs]) * jax.nn.sigmoid(o)).astype(hc_ref.dtype)

    @pl.when(ch == pl.num_programs(1) - 1)
    def _():
        c_out[0] = c_ref[...]
        n_out[0] = n_ref[...]
        m_out[0] = m_ref[...]


def _mlstm_call(qkvo, gates, bias, norm_g, batch, seq):
    n = qkvo.shape[0]
    dc = qkvo.shape[1] // 4
    nh = dc // HEAD_DIM
    L = MLSTM_CHUNK if seq % MLSTM_CHUNK == 0 else seq
    assert L % SUBLANES == 0 and L == LANES, "prompt mLSTM kernel needs 128-token chunks"
    nc = seq // L
    est = 2 * L * 4 * dc * 2 + 2 * L * LANES * 4 + 2 * L * dc * 2 + 3 * nh * HEAD_DIM * HEAD_DIM * 4 + 16 * L * L * 4
    return pl.pallas_call(
        functools.partial(_mlstm_body, nh=nh),
        out_shape=[jax.ShapeDtypeStruct((n, dc), BF16),
                   jax.ShapeDtypeStruct((batch, nh, HEAD_DIM, HEAD_DIM), F32),
                   jax.ShapeDtypeStruct((batch, nh, 1, HEAD_DIM), F32),
                   jax.ShapeDtypeStruct((batch, nh, 1, LANES), F32)],
        grid=(batch, nc),
        in_specs=[pl.BlockSpec((L, 4 * dc), lambda b, c: (b * nc + c, 0)),
                  pl.BlockSpec((L, LANES), lambda b, c: (b * nc + c, 0)),
                  _full((1, LANES)), _full((1, dc))],
        out_specs=[pl.BlockSpec((L, dc), lambda b, c: (b * nc + c, 0)),
                   pl.BlockSpec((1, nh, HEAD_DIM, HEAD_DIM), lambda b, c: (b, 0, 0, 0)),
                   pl.BlockSpec((1, nh, 1, HEAD_DIM), lambda b, c: (b, 0, 0, 0)),
                   pl.BlockSpec((1, nh, 1, LANES), lambda b, c: (b, 0, 0, 0))],
        scratch_shapes=[pltpu.VMEM((nh, HEAD_DIM, HEAD_DIM), F32), pltpu.VMEM((nh, 1, HEAD_DIM), F32),
                        pltpu.VMEM((nh, 1, LANES), F32)],
        compiler_params=pltpu.CompilerParams(dimension_semantics=("parallel", "arbitrary"),
                                             vmem_limit_bytes=_vmem_limit(est)),
        name="mlstm_prompt",
    )(qkvo, gates, bias, norm_g)


def _rg_gates(xc, wa_ref, ba_ref, wx_ref, bx_ref, lam_ref):
    xb = xc.astype(BF16)
    r = jax.nn.sigmoid(_dot(xb, wa_ref[...]) + ba_ref[...])
    i = jax.nn.sigmoid(_dot(xb, wx_ref[...]) + bx_ref[...])
    log_a = -RG_C * r * _softplus(-lam_ref[...])
    a = jnp.exp(log_a)
    u = jnp.sqrt(-jnp.tanh(log_a) * (a * a + 1.0)) * (i * xc)
    return a, u


def _rglru_body(xr_ref, gr_ref, cw_ref, cb_ref, wa_ref, ba_ref, wx_ref, bx_ref, lam_ref,
                y_ref, h_out, xbuf_ref, a_ref, u_ref, h_ref, *, width):
    tstep = pl.program_id(1)
    T, C = xr_ref.shape
    pad = SUBLANES

    @pl.when(tstep == 0)
    def _():
        xbuf_ref[0:pad, :] = jnp.zeros((pad, C), F32)
        h_ref[...] = jnp.zeros(h_ref.shape, F32)

    xbuf_ref[pad:pad + T, :] = xr_ref[...]
    xc = cb_ref[...]
    for j in range(width):
        xc = xc + xbuf_ref[pl.ds(pad - (width - 1) + j, T), :] * cw_ref[j:j + 1, :]
    xbuf_ref[0:pad, :] = xbuf_ref[T:T + pad, :]

    a, u = _rg_gates(xc, wa_ref, ba_ref, wx_ref, bx_ref, lam_ref)
    a_ref[...] = a
    u_ref[...] = u
    sub = lax.broadcasted_iota(jnp.int32, (SUBLANES, C), 0)

    def group(gi, h):
        start = pl.multiple_of(gi * SUBLANES, SUBLANES)
        ag = a_ref[pl.ds(start, SUBLANES), :]
        ug = u_ref[pl.ds(start, SUBLANES), :]
        for sh in (1, 2, 4):
            ap = jnp.where(sub >= sh, pltpu.roll(ag, sh, 0), 1.0)
            up = jnp.where(sub >= sh, pltpu.roll(ug, sh, 0), 0.0)
            ug = ag * up + ug
            ag = ag * ap
        hg = ag * h + ug
        u_ref[pl.ds(start, SUBLANES), :] = hg
        return hg[SUBLANES - 1:SUBLANES, :]

    h_last = lax.fori_loop(0, T // SUBLANES, group, h_ref[...])
    h_ref[...] = h_last
    y_ref[...] = (u_ref[...] * jax.nn.gelu(gr_ref[...], approximate=True)).astype(y_ref.dtype)

    @pl.when(tstep == pl.num_programs(1) - 1)
    def _():
        h_out[0] = h_last


def _rglru_call(xr, gr, conv_w, conv_b, wa, ba, wx, bx, lam, batch, seq):
    n, c = xr.shape
    width = conv_w.shape[0]
    t = min(RG_TILE, seq)
    assert seq % t == 0 and t % SUBLANES == 0 and width - 1 <= SUBLANES
    nt = seq // t
    row = pl.BlockSpec((t, c), lambda b, s: (b * nt + s, 0))
    est = 6 * t * c * 4 + 2 * c * c * 2 * 2 + 8 * t * c * 4
    return pl.pallas_call(
        functools.partial(_rglru_body, width=width),
        out_shape=[jax.ShapeDtypeStruct((n, c), BF16), jax.ShapeDtypeStruct((batch, 1, c), F32)],
        grid=(batch, nt),
        in_specs=[row, row, _full(conv_w.shape), _full((1, c)), _full(wa.shape), _full((1, c)),
                  _full(wx.shape), _full((1, c)), _full((1, c))],
        out_specs=[row, pl.BlockSpec((1, 1, c), lambda b, s: (b, 0, 0))],
        scratch_shapes=[pltpu.VMEM((t + SUBLANES, c), F32), pltpu.VMEM((t, c), F32),
                        pltpu.VMEM((t, c), F32), pltpu.VMEM((1, c), F32)],
        compiler_params=pltpu.CompilerParams(dimension_semantics=("parallel", "arbitrary"),
                                             vmem_limit_bytes=_vmem_limit(est)),
        name="rglru_prompt",
    )(xr, gr, conv_w, conv_b, wa, ba, wx, bx, lam)


def _to_column(row_vec):
    n = row_vec.shape[1]
    r = lax.broadcasted_iota(jnp.int32, (n, n), 0)
    c = lax.broadcasted_iota(jnp.int32, (n, n), 1)
    return jnp.sum(jnp.where(r == c, jnp.broadcast_to(row_vec, (n, n)), 0.0), axis=-1, keepdims=True)


def _rec_sample_body(qkvo_ref, gates_ref, xr_ref, gr_ref, c0_ref, n0_ref, m0_ref, h0_ref, conv0_ref,
                     bias_ref, ng_ref, cw_ref, cb_ref, wa_ref, ba_ref, wx_ref, bx_ref, lam_ref,
                     hc_ref, y_ref, c_out, n_out, m_out, h_out, conv_out, *, nh, width):
    dc = nh * HEAD_DIM
    g = gates_ref[0] + bias_ref[...]
    m0 = m0_ref[0]
    for h in range(nh):
        cs = slice(h * HEAD_DIM, (h + 1) * HEAD_DIM)
        q = qkvo_ref[0][:, cs]
        k = qkvo_ref[0][:, dc + h * HEAD_DIM:dc + (h + 1) * HEAD_DIM]
        v = qkvo_ref[0][:, 2 * dc + h * HEAD_DIM:2 * dc + (h + 1) * HEAD_DIM]
        o = qkvo_ref[0][:, 3 * dc + h * HEAD_DIM:3 * dc + (h + 1) * HEAD_DIM]
        ig = g[:, h:h + 1]
        lf = _log_sigmoid(g[:, nh + h:nh + h + 1])
        m_prev = m0[:, h:h + 1]
        c_prev = c0_ref[0, h]
        n_prev = n0_ref[0, h:h + 1, :]
        inter = lf + m_prev
        m_t = jnp.maximum(inter, ig)
        w_inter = jnp.exp(inter - m_t)
        s = jnp.sum(q * k, axis=-1, keepdims=True) * jnp.exp(ig - m_t)
        q_col = _to_column(q)
        qc = jnp.sum(q_col * c_prev, axis=0, keepdims=True)
        num = w_inter * qc + s * v
        den = w_inter * jnp.sum(q * n_prev, axis=-1, keepdims=True) + s
        hh = num / jnp.maximum(jnp.abs(den), jnp.exp(-m_t))
        w_old = jnp.exp(lf + m_prev - m_t)
        w_new = jnp.exp(ig - m_t)
        c_out[0, h] = w_old * c_prev + (_to_column(k) * w_new) * v
        n_out[0, h:h + 1, :] = w_old * n_prev + w_new * k
        m_out[0, :, h:h + 1] = m_t
        hc_ref[0, :, cs] = (_rms(hh, ng_ref[:, cs]) * jax.nn.sigmoid(o)).astype(hc_ref.dtype)

    x = xr_ref[0]
    xc = cb_ref[...]
    for j in range(width - 1):
        xc = xc + conv0_ref[0, j:j + 1, :] * cw_ref[j:j + 1, :]
    xc = xc + x * cw_ref[width - 1:width, :]
    a, u = _rg_gates(jnp.broadcast_to(xc, (SUBLANES, xc.shape[1])), wa_ref, ba_ref, wx_ref, bx_ref, lam_ref)
    h_new = a[0:1] * h0_ref[0] + u[0:1]
    h_out[0] = h_new
    y_ref[0] = (h_new * jax.nn.gelu(gr_ref[0], approximate=True)).astype(y_ref.dtype)
    for j in range(width - 2):
        conv_out[0, j:j + 1, :] = conv0_ref[0, j + 1:j + 2, :]
    conv_out[0, width - 2:width - 1, :] = x


def _rec_sample_call(qkvo, gates, xr, gr, c0, n0, m0, h0, conv0, bias, norm_g, conv_w, conv_b,
                     wa, ba, wx, bx, lam):
    db = qkvo.shape[0]
    dc = qkvo.shape[1] // 4
    nh = dc // HEAD_DIM
    c = xr.shape[1]
    width = conv_w.shape[0]
    tok = lambda w: pl.BlockSpec((1, 1, w), lambda b: (b, 0, 0))
    in_specs = [tok(4 * dc), tok(LANES), tok(c), tok(c),
                pl.BlockSpec((1, nh, HEAD_DIM, HEAD_DIM), lambda b: (b, 0, 0, 0)),
                pl.BlockSpec((1, nh, HEAD_DIM), lambda b: (b, 0, 0)),
                tok(nh), tok(c),
                pl.BlockSpec((1, width - 1, c), lambda b: (b, 0, 0)),
                _full((1, LANES)), _full((1, dc)), _full(conv_w.shape), _full((1, c)),
                _full(wa.shape), _full((1, c)), _full(wx.shape), _full((1, c)), _full((1, c))]
    out_shape = [jax.ShapeDtypeStruct((db, 1, dc), BF16), jax.ShapeDtypeStruct((db, 1, c), BF16),
                 jax.ShapeDtypeStruct((db, nh, HEAD_DIM, HEAD_DIM), F32),
                 jax.ShapeDtypeStruct((db, nh, HEAD_DIM), F32),
                 jax.ShapeDtypeStruct((db, 1, nh), F32),
                 jax.ShapeDtypeStruct((db, 1, c), F32),
                 jax.ShapeDtypeStruct((db, width - 1, c), F32)]
    out_specs = [tok(dc), tok(c),
                 pl.BlockSpec((1, nh, HEAD_DIM, HEAD_DIM), lambda b: (b, 0, 0, 0)),
                 pl.BlockSpec((1, nh, HEAD_DIM), lambda b: (b, 0, 0)),
                 tok(nh), tok(c),
                 pl.BlockSpec((1, width - 1, c), lambda b: (b, 0, 0))]
    est = 4 * nh * HEAD_DIM * HEAD_DIM * 4 + 2 * c * c * 2 * 2 + 1024 * 1024
    return pl.pallas_call(
        functools.partial(_rec_sample_body, nh=nh, width=width),
        out_shape=out_shape,
        grid=(db,),
        in_specs=in_specs,
        out_specs=out_specs,
        compiler_params=pltpu.CompilerParams(dimension_semantics=("parallel",),
                                             vmem_limit_bytes=_vmem_limit(est)),
        name="rec_sample",
    )(qkvo.reshape(db, 1, 4 * dc), gates.reshape(db, 1, LANES), xr.reshape(db, 1, c), gr.reshape(db, 1, c),
      c0, n0, m0.reshape(db, 1, nh), h0.reshape(db, 1, c), conv0, bias, norm_g, conv_w, conv_b,
      wa, ba, wx, bx, lam)


def _block_diag(w):
    n, k, j = w.shape
    eye = jnp.eye(n, dtype=w.dtype)
    return (eye[:, None, :, None] * w[:, :, None, :]).reshape(n * k, n * j)


def _lane_rows(vals):
    return jnp.broadcast_to(vals.astype(F32)[:, None, None], (vals.shape[0], 1, LANES))


def kernel(x_prompt, x_sample, cache_k, cache_v, state_mlstm_c, state_mlstm_n, state_mlstm_m, state_rglru_h, state_rglru_conv, page_table, norm_g, ffn_w_gate, ffn_w_up, ffn_w_down, att_w_in, att_w_out, diff_lambda_q1, diff_lambda_k1, diff_lambda_q2, diff_lambda_k2, diff_subln_g, rec_w_in, rec_w_out, mlstm_b_i, mlstm_b_f, mlstm_norm_g, rg_conv_w, rg_conv_b, rg_w_a, rg_b_a, rg_w_x, rg_b_x, rg_lambda, final_norm_g):
    batch, seq, d = x_prompt.shape
    db, dseq, _ = x_sample.shape
    assert dseq == 1, "the sample group holds one new token per sequence"
    depth = norm_g.shape[0]
    n_heads = cache_k.shape[3]
    nh = n_heads // 2
    dc = state_mlstm_c.shape[2] * HEAD_DIM
    nhc = dc // HEAD_DIM
    drg = state_rglru_h.shape[2]
    width = state_rglru_conv.shape[2] + 1

    xp = x_prompt.reshape(batch * seq, d)
    xs = x_sample.reshape(db, d)

    hidx = jnp.arange(n_heads, dtype=F32)
    slopes = 2.0 ** (-8.0 * (hidx + 1.0) / n_heads)
    slopes_a, slopes_b = _lane_rows(slopes[0::2]), _lane_rows(slopes[1::2])

    k_p, v_p, k_s, v_s = [], [], [], []
    rec_p, rec_s = [], []
    mix_p = mix_s = None
    for l in range(depth):
        j = l // 2
        wg, wu, wd = (w[l].astype(BF16) for w in (ffn_w_gate, ffn_w_up, ffn_w_down))
        xp = _ffn_call(xp, norm_g[l, 0], wg[0], wu[0], wd[0])
        xs = _ffn_call(xs, norm_g[l, 0], wg[0], wu[0], wd[0])
        if l % 2 == 0:
            lam_init = 0.8 - 0.6 * math.exp(-0.3 * l)
            w_in = att_w_in[j].astype(BF16)
            w_out = att_w_out[j].astype(BF16)
            lam_vecs = [v[j].reshape(1, HALF_A) for v in
                        (diff_lambda_q1, diff_lambda_k1, diff_lambda_q2, diff_lambda_k2)]
            subg = diff_subln_g[j].reshape(1, HEAD_DIM)
            q, k, v, kb, vb, kbar = _qkv_call(xp, norm_g[l, 1], w_in, prompt=True)
            kbar = kbar.reshape(batch, seq // MOBA_BLOCK, d // 2)
            kbar = jnp.pad(kbar, ((0, 0), (0, LANES - seq // MOBA_BLOCK), (0, 0)))
            oa = _diff_call(q, kb, vb, slopes_a, lam_vecs, subg, batch, seq, lam_init)
            ob = _moba_call(q, kb, vb, kbar, slopes_b, batch, seq)
            mix_p = (oa, ob, w_out[:d // 2], w_out[d // 2:])
            k_p.append(k.reshape(batch, seq, n_heads, HEAD_DIM))
            v_p.append(v.reshape(batch, seq, n_heads, HEAD_DIM))
            qs, ks, vs = _qkv_call(xs, norm_g[l, 1], w_in, prompt=False)
            qs3, ks3, vs3 = (a.reshape(db, 1, d) for a in (qs, ks, vs))
            oa_s, sel = _sample_a_call(page_table, cache_k, cache_v, j, qs3, ks3, vs3, slopes_a,
                                       lam_vecs, subg, lam_init)
            ob_s = _sample_b_call(page_table, sel, cache_k, cache_v, j, qs3, ks3, vs3, slopes_b)
            mix_s = (oa_s.reshape(db, d // 2).astype(BF16), ob_s.reshape(db, d // 2).astype(BF16),
                     w_out[:d // 2], w_out[d // 2:])
            k_s.append(ks.reshape(db, 1, n_heads, HEAD_DIM))
            v_s.append(vs.reshape(db, 1, n_heads, HEAD_DIM))
        else:
            w = rec_w_in[j]
            n_gate = 2 * nhc
            w_packed = jnp.concatenate(
                [w[:, :4 * dc], w[:, 4 * dc + n_gate:],
                 jnp.pad(w[:, 4 * dc:4 * dc + n_gate], ((0, 0), (0, LANES - n_gate)))], axis=1).astype(BF16)
            w_out = rec_w_out[j].astype(BF16)
            bias = jnp.pad(jnp.concatenate([mlstm_b_i[j], mlstm_b_f[j]]), (0, LANES - n_gate)).reshape(1, LANES)
            ng = mlstm_norm_g[j].reshape(1, dc)
            conv_w, conv_b = rg_conv_w[j], rg_conv_b[j].reshape(1, drg)
            wa, wx = _block_diag(rg_w_a[j]).astype(BF16), _block_diag(rg_w_x[j]).astype(BF16)
            ba, bx, lam = (a[j].reshape(1, drg) for a in (rg_b_a, rg_b_x, rg_lambda))
            qkvo, gates, xr, gr = _rec_in_call(xp, norm_g[l, 1], w_packed, dc, drg, BF16)
            hc, c1, n1, m1 = _mlstm_call(qkvo, gates, bias, ng, batch, seq)
            y, h1 = _rglru_call(xr, gr, conv_w, conv_b, wa, ba, wx, bx, lam, batch, seq)
            mix_p = (hc, y, w_out[:dc], w_out[dc:])
            conv1 = xr.reshape(batch, seq, drg)[:, seq - (width - 1):, :]
            rec_p.append((c1, n1.reshape(batch, nhc, HEAD_DIM), m1[:, :, 0, 0], h1.reshape(batch, drg), conv1))
            qkvo_s, gates_s, xr_s, gr_s = _rec_in_call(xs, norm_g[l, 1], w_packed, dc, drg, F32)
            hc_s, y_s, c1s, n1s, m1s, h1s, conv1s = _rec_sample_call(
                qkvo_s, gates_s, xr_s, gr_s, state_mlstm_c[j], state_mlstm_n[j], state_mlstm_m[j],
                state_rglru_h[j], state_rglru_conv[j], bias, ng, conv_w, conv_b, wa, ba, wx, bx, lam)
            mix_s = (hc_s.reshape(db, dc), y_s.reshape(db, drg), w_out[:dc], w_out[dc:])
            rec_s.append((c1s, n1s, m1s.reshape(db, nhc), h1s.reshape(db, drg), conv1s))
        fg = final_norm_g if l == depth - 1 else None
        xp = _ffn_call(xp, norm_g[l, 2], wg[1], wu[1], wd[1], mix=mix_p, final_g=fg)
        xs = _ffn_call(xs, norm_g[l, 2], wg[1], wu[1], wd[1], mix=mix_s, final_g=fg)

    stack = lambda items: jnp.stack(items, axis=0)
    return (xp.reshape(batch, seq, d), xs.reshape(db, 1, d),
            stack(k_p), stack(v_p), stack(k_s), stack(v_s),
            *(stack([st[i] for st in rec_p]) for i in range(5)),
            *(stack([st[i] for st in rec_s]) for i in range(5)))
```

```python
import functools
import math

import jax
import jax.numpy as jnp
from jax import lax
from jax.experimental import pallas as pl
from jax.experimental.pallas import tpu as pltpu

F32 = jnp.float32
BF16 = jnp.bfloat16

HEAD_DIM = 128
HALF_A = HEAD_DIM // 2
MOBA_BLOCK = 256
MOBA_TOPK = 3
MLSTM_CHUNK = 128
RG_C = 8.0
N_BLOCKS_RG = 8
RMS_EPS = 1e-6
NEG = -1e30

V7X_VMEM_BYTES = 64 * 1024 * 1024
LANES = 128
SUBLANES = 8

TOKEN_TILE = 512
FF_CHUNK = 256
ATT_TILE = 256
ATT_HEADS_PER_STEP = 4
MLSTM_SEQS_PER_STEP = 1
RG_TILE = 512
PAGES_PER_STEP = 16


def _vmem_limit(est_bytes):
    return int(min(max(2 * est_bytes, 32 * 1024 * 1024), V7X_VMEM_BYTES - 8 * 1024 * 1024))


def _rms(x, g):
    return x * lax.rsqrt(jnp.mean(x * x, axis=-1, keepdims=True) + RMS_EPS) * g


def _dot(a, b):
    return jnp.dot(a, b, preferred_element_type=F32)


def _dot_nt(a, b):
    return lax.dot_general(a, b, (((1,), (1,)), ((), ())), preferred_element_type=F32)


def _dot_tn(a, b):
    return lax.dot_general(a, b, (((0,), (0,)), ((), ())), preferred_element_type=F32)


def _split_bf16(x):
    hi = x.astype(BF16)
    lo = (x - hi.astype(F32)).astype(BF16)
    return hi, lo


def _full(shape):
    nd = len(shape)
    return pl.BlockSpec(shape, lambda *_: (0,) * nd)


def _ffn_body(*refs, has_mix, has_final, ff_chunk):
    refs = list(refs)
    x_ref = refs.pop(0)
    if has_mix:
        a_ref, b_ref, wo_ref = refs[:3]
        refs = refs[3:]
    g_ref, wg_ref, wu_ref, wd_ref = refs[:4]
    refs = refs[4:]
    if has_final:
        fg_ref = refs.pop(0)
    o_ref, act_ref = refs

    x = x_ref[...]
    if has_mix:
        da = a_ref.shape[1]
        x = x + _dot(a_ref[...], wo_ref[0:da, :]) + _dot(b_ref[...], wo_ref[da:, :])
    h = _rms(x, g_ref[...]).astype(BF16)
    ff = wg_ref.shape[1]
    for c in range(ff // ff_chunk):
        sl = slice(c * ff_chunk, (c + 1) * ff_chunk)
        gate = _dot(h, wg_ref[:, sl])
        up = _dot(h, wu_ref[:, sl])
        act_ref[:, sl] = (gate * jax.nn.sigmoid(gate) * up).astype(BF16)
    y = x + 0.5 * _dot(act_ref[...], wd_ref[...])
    if has_final:
        y = _rms(y, fg_ref[...])
    o_ref[...] = y


def _ffn_call(x, g, wg, wu, wd, l, j, mix=None, final_g=None):
    n, d = x.shape
    ff = wg.shape[3]
    tm = min(n, TOKEN_TILE)
    assert n % tm == 0 and ff % FF_CHUNK == 0
    row = lambda w: pl.BlockSpec((tm, w), lambda i: (i, 0))
    pick = lambda w: pl.BlockSpec((None, None) + w.shape[2:], lambda i: (l, j, 0, 0))
    args, specs = [x], [row(d)]
    if mix is not None:
        a, b, wo = mix
        args += [a, b, wo]
        specs += [row(a.shape[1]), row(b.shape[1]), _full(wo.shape)]
    args += [g.reshape(1, d), wg, wu, wd]
    specs += [_full((1, d)), pick(wg), pick(wu), pick(wd)]
    if final_g is not None:
        args.append(final_g.reshape(1, d))
        specs.append(_full((1, d)))
    est = 2 * 3 * d * ff * 2 + 4 * tm * d * 4 + tm * ff * 2 + 4 * tm * FF_CHUNK * 4 + 2 * tm * d * 4
    return pl.pallas_call(
        functools.partial(_ffn_body, has_mix=mix is not None, has_final=final_g is not None,
                          ff_chunk=FF_CHUNK),
        out_shape=jax.ShapeDtypeStruct((n, d), F32),
        grid=(n // tm,),
        in_specs=specs,
        out_specs=row(d),
        scratch_shapes=[pltpu.VMEM((tm, ff), BF16)],
        compiler_params=pltpu.CompilerParams(dimension_semantics=("parallel",),
                                             vmem_limit_bytes=_vmem_limit(est)),
        name="ffn",
    )(*args)


def _qkv_body(x_ref, g_ref, w_ref, *out_refs, prompt):
    x = x_ref[...]
    d = x.shape[1]
    h = _rms(x, g_ref[...]).astype(BF16)
    q = _dot(h, w_ref[:, 0:d])
    k = _dot(h, w_ref[:, d:2 * d])
    v = _dot(h, w_ref[:, 2 * d:3 * d])
    if not prompt:
        q_ref, k_ref, v_ref = out_refs
        q_ref[...] = q
        k_ref[...] = k
        v_ref[...] = v
        return
    q_ref, k_ref, v_ref, kb_ref, vb_ref, kbar_ref = out_refs
    q_ref[...] = q.astype(BF16)
    k_ref[...] = k
    v_ref[...] = v
    lane = lax.broadcasted_iota(jnp.int32, (1, d), 1)
    scale = jnp.where(lane < d // 2, HALF_A ** -0.5, HEAD_DIM ** -0.5).astype(F32)
    kb_ref[...] = (k * scale).astype(BF16)
    vb_ref[...] = v.astype(BF16)
    tm = x.shape[0]
    for r in range(tm // MOBA_BLOCK):
        blk = k[r * MOBA_BLOCK:(r + 1) * MOBA_BLOCK, d // 2:]
        kbar_ref[0, r:r + 1, :] = jnp.mean(blk, axis=0, keepdims=True)


def _qkv_call(x, g, w_in, prompt):
    n, d = x.shape
    tm = min(n, TOKEN_TILE)
    assert n % tm == 0
    row = lambda w: pl.BlockSpec((tm, w), lambda i: (i, 0))
    if prompt:
        assert tm % MOBA_BLOCK == 0
        nb = tm // MOBA_BLOCK
        out_shape = [jax.ShapeDtypeStruct((n, d), BF16), jax.ShapeDtypeStruct((n, d), F32),
                     jax.ShapeDtypeStruct((n, d), F32), jax.ShapeDtypeStruct((n, d), BF16),
                     jax.ShapeDtypeStruct((n, d), BF16),
                     jax.ShapeDtypeStruct((n // tm, nb, d // 2), F32)]
        out_specs = [row(d)] * 5 + [pl.BlockSpec((1, nb, d // 2), lambda i: (i, 0, 0))]
    else:
        out_shape = [jax.ShapeDtypeStruct((n, d), F32)] * 3
        out_specs = [row(d)] * 3
    est = 2 * d * 3 * d * 2 + 2 * tm * d * 4 * 4 + 3 * tm * d * 4
    return pl.pallas_call(
        functools.partial(_qkv_body, prompt=prompt),
        out_shape=out_shape,
        grid=(n // tm,),
        in_specs=[row(d), _full((1, d)), _full(w_in.shape)],
        out_specs=out_specs,
        compiler_params=pltpu.CompilerParams(dimension_semantics=("parallel",),
                                             vmem_limit_bytes=_vmem_limit(est)),
        name="qkv_prompt" if prompt else "qkv_sample",
    )(x, g.reshape(1, d), w_in)


def _lanes(x, width):
    return x if width == LANES else jnp.concatenate([x] * (width // LANES), axis=1)


def _softmax_tile(s, shift, v_tile, ones_col, m_ref, acc_ref):
    m_old = m_ref[...]
    m_new = jnp.maximum(m_old, jnp.max(s, axis=-1, keepdims=True) + shift)
    p = jnp.exp(s - _lanes(m_new - shift, s.shape[1]))
    alpha = jnp.exp(m_old - m_new)
    va = jnp.concatenate([v_tile, ones_col], axis=1)
    acc_ref[...] = _lanes(alpha, 2 * LANES) * acc_ref[...] + _dot(p.astype(BF16), va)
    m_ref[...] = m_new


def _softmax_result(acc_ref):
    acc = acc_ref[...]
    return acc[:, 0:HEAD_DIM] / acc[:, HEAD_DIM:HEAD_DIM + 1]


def _chunk_shift(slope, delta):
    return slope * (jnp.zeros((1, 1), jnp.int32) + delta).astype(F32)


def _offset_cols(tk, lo_lane, hi_lane):
    lane = lax.broadcasted_iota(jnp.int32, (tk, LANES), 1)
    c = lax.broadcasted_iota(jnp.int32, (tk, LANES), 0)
    c_lo = jnp.bitwise_and(c, 255)
    return jnp.where(lane == lo_lane, c_lo, jnp.where(lane == hi_lane, c - c_lo, 0)).astype(F32)


def _ones_col(rows):
    lane = lax.broadcasted_iota(jnp.int32, (rows, LANES), 1)
    return jnp.where(lane == 0, 1.0, 0.0).astype(BF16)


def _causal_bias(t):
    r = lax.broadcasted_iota(jnp.int32, (t, t), 0)
    c = lax.broadcasted_iota(jnp.int32, (t, t), 1)
    return jnp.where(c <= r, 0.0, NEG).astype(F32)


def _add_diagonal_bias(s, bias):
    t = bias.shape[1]
    width = s.shape[1]
    if width == t:
        return s + bias
    return jnp.concatenate([s[:, 0:width - t], s[:, width - t:] + bias], axis=1)


def _key_chunk(seq):
    return next(tk for tk in (1024, 512, 256) if seq % tk == 0)


def _diff_lambda(lq1_ref, lk1_ref, lq2_ref, lk2_ref, lam_init):
    a = jnp.exp(jnp.sum(lq1_ref[...] * lk1_ref[...], axis=-1, keepdims=True))
    b = jnp.exp(jnp.sum(lq2_ref[...] * lk2_ref[...], axis=-1, keepdims=True))
    return a - b + lam_init


def _diff_body(slope_ref, q_ref, k_ref, v_ref, lq1_ref, lk1_ref, lq2_ref, lk2_ref, subg_ref,
               o_ref, qa_ref, m_ref, acc_ref, *, t, tk, lam_init):
    i = pl.program_id(2)
    hp = qa_ref.shape[0]
    lane = lax.broadcasted_iota(jnp.int32, (t, HEAD_DIM), 1)
    consts = {tk: (_offset_cols(tk, 0, 1).astype(BF16), _ones_col(tk))}
    slopes = []
    for hh in range(hp):
        cs_h = slice(hh * HEAD_DIM, (hh + 1) * HEAD_DIM)
        q = q_ref[:, cs_h]
        zero = jnp.zeros_like(q)
        slopes.append(slope_ref[hh][:, 0:1])
        q_ext = jnp.where(lane < 2, slopes[hh], 0.0).astype(BF16)
        qa_ref[hh, 0:t, 0:HEAD_DIM] = jnp.where(lane < HALF_A, q, zero)
        qa_ref[hh, t:2 * t, 0:HEAD_DIM] = jnp.where(lane >= HALF_A, q, zero)
        qa_ref[hh, 0:t, HEAD_DIM:] = q_ext
        qa_ref[hh, t:2 * t, HEAD_DIM:] = q_ext

    m_ref[...] = jnp.full(m_ref.shape, NEG, F32)
    acc_ref[...] = jnp.zeros(acc_ref.shape, F32)
    q0 = i * t

    tri = _causal_bias(t)

    def chunk(cs, width, diagonal):
        kext, ones_col = consts.get(width) or (_offset_cols(width, 0, 1).astype(BF16), _ones_col(width))
        for hh in range(hp):
            cs_h = slice(hh * HEAD_DIM, (hh + 1) * HEAD_DIM)
            ka = jnp.concatenate([k_ref[pl.ds(cs, width), cs_h], kext], axis=1)
            s = _dot_nt(qa_ref[hh], ka)
            if diagonal:
                s = _add_diagonal_bias(s, jnp.concatenate([tri, tri], axis=0))
            _softmax_tile(s, _chunk_shift(slopes[hh], cs - q0), v_ref[pl.ds(cs, width), cs_h],
                          ones_col, m_ref.at[hh], acc_ref.at[hh])

    def body(j, carry):
        chunk(pl.multiple_of(j * tk, tk), tk, False)
        return carry

    n_full = q0 // tk
    lax.fori_loop(0, n_full, body, 0)
    rest = i - n_full * (tk // t)
    for r in range(tk // t):
        @pl.when(rest == r)
        def _():
            chunk(pl.multiple_of(n_full * tk, tk), (r + 1) * t, True)

    lam = _diff_lambda(lq1_ref, lk1_ref, lq2_ref, lk2_ref, lam_init)
    for hh in range(hp):
        o = _softmax_result(acc_ref.at[hh])
        oa = o[0:t] - lam * o[t:2 * t]
        o_ref[:, hh * HEAD_DIM:(hh + 1) * HEAD_DIM] = (
            _rms(oa, subg_ref[...]) * (1.0 - lam_init)).astype(BF16)


def _diff_call(q, kb, vb, slopes, lam_vecs, subg, batch, seq, lam_init):
    n, d = q.shape
    nh = d // 2 // HEAD_DIM
    t = ATT_TILE
    tk = _key_chunk(seq)
    assert seq % t == 0 and tk % t == 0
    nq = seq // t
    hp = ATT_HEADS_PER_STEP
    assert nh % hp == 0
    vec = lambda w: _full((1, w))
    est = hp * (2 * 2 * seq * HEAD_DIM * 2 + 2 * t * 2 * HEAD_DIM * 2 + 2 * t * 3 * LANES * 4 + 6 * 2 * t * tk * 4)
    return pl.pallas_call(
        functools.partial(_diff_body, t=t, tk=tk, lam_init=lam_init),
        out_shape=jax.ShapeDtypeStruct((n, d // 2), BF16),
        grid=(batch, nh // hp, nq),
        in_specs=[pl.BlockSpec((hp, 1, LANES), lambda b, h, i: (h, 0, 0)),
                  pl.BlockSpec((t, hp * HEAD_DIM), lambda b, h, i: (b * nq + i, h)),
                  pl.BlockSpec((seq, hp * HEAD_DIM), lambda b, h, i: (b, h)),
                  pl.BlockSpec((seq, hp * HEAD_DIM), lambda b, h, i: (b, h)),
                  vec(HALF_A), vec(HALF_A), vec(HALF_A), vec(HALF_A), vec(HEAD_DIM)],
        out_specs=pl.BlockSpec((t, hp * HEAD_DIM), lambda b, h, i: (b * nq + i, h)),
        scratch_shapes=[pltpu.VMEM((hp, 2 * t, 2 * HEAD_DIM), BF16), pltpu.VMEM((hp, 2 * t, LANES), F32),
                        pltpu.VMEM((hp, 2 * t, 2 * LANES), F32)],
        compiler_params=pltpu.CompilerParams(dimension_semantics=("parallel", "parallel", "parallel"),
                                             vmem_limit_bytes=_vmem_limit(est)),
        name="diff_attn_prompt",
    )(slopes, q, kb, vb, *lam_vecs, subg)


def _moba_body(slope_ref, q_ref, k_ref, v_ref, kbar_ref, o_ref, qa_ref, m_ref, acc_ref,
               *, t, tk, n_blocks):
    i = pl.program_id(2)
    hp = qa_ref.shape[0]
    lane = lax.broadcasted_iota(jnp.int32, (t, LANES), 1)
    nbp = -(-n_blocks // SUBLANES) * SUBLANES
    blk = lax.broadcasted_iota(jnp.int32, (nbp, t), 0)
    slopes = []
    for hh in range(hp):
        cs_h = slice(hh * HEAD_DIM, (hh + 1) * HEAD_DIM)
        slope = slope_ref[hh][:, 0:1]
        slopes.append(slope)
        q = q_ref[:, cs_h]
        kbar_hi, kbar_lo = _split_bf16(kbar_ref[0, :, cs_h])
        gate = (_dot_nt(kbar_hi, q) + _dot_nt(kbar_lo, q))[0:nbp]
        gate = jnp.where(blk < i, gate, NEG)
        cnt = jnp.zeros((nbp, t), jnp.int32)
        for n2 in range(n_blocks):
            gn = gate[n2:n2 + 1, :]
            beats = (gn > gate) | ((gn == gate) & (blk > n2))
            cnt = cnt + jnp.where(beats & (i > n2), 1, 0)
        keep = ((blk < i) & (cnt < MOBA_TOPK)) | (blk == i)
        pen = jnp.concatenate([jnp.where(keep, 0.0, NEG), jnp.full((LANES - nbp, t), NEG, F32)], axis=0).T
        qa_ref[hh, :, 0:HEAD_DIM] = q
        qa_ref[hh, :, HEAD_DIM:] = jnp.where(lane >= LANES - 2, slope, pen).astype(BF16)

    def key_consts(width):
        lane_k = lax.broadcasted_iota(jnp.int32, (width, LANES), 1)
        row_k = lax.broadcasted_iota(jnp.int32, (width, LANES), 0)
        return _offset_cols(width, LANES - 2, LANES - 1), lane_k - row_k // MOBA_BLOCK, _ones_col(width)

    consts = {tk: key_consts(tk)}

    m_ref[...] = jnp.full(m_ref.shape, NEG, F32)
    acc_ref[...] = jnp.zeros(acc_ref.shape, F32)
    q0 = i * t

    tri = _causal_bias(t)

    def chunk(cs, width, diagonal):
        offs, blk_off, ones_col = consts.get(width) or key_consts(width)
        kext = jnp.where(blk_off == cs // MOBA_BLOCK, 1.0, offs).astype(BF16)
        for hh in range(hp):
            cs_h = slice(hh * HEAD_DIM, (hh + 1) * HEAD_DIM)
            ka = jnp.concatenate([k_ref[pl.ds(cs, width), cs_h], kext], axis=1)
            s = _dot_nt(qa_ref[hh], ka)
            if diagonal:
                s = _add_diagonal_bias(s, tri)
            _softmax_tile(s, _chunk_shift(slopes[hh], cs - q0), v_ref[pl.ds(cs, width), cs_h],
                          ones_col, m_ref.at[hh], acc_ref.at[hh])

    def body(j, carry):
        chunk(pl.multiple_of(j * tk, tk), tk, False)
        return carry

    n_full = q0 // tk
    lax.fori_loop(0, n_full, body, 0)
    rest = i - n_full * (tk // t)
    for r in range(tk // t):
        @pl.when(rest == r)
        def _():
            chunk(pl.multiple_of(n_full * tk, tk), (r + 1) * t, True)
    for hh in range(hp):
        o_ref[:, hh * HEAD_DIM:(hh + 1) * HEAD_DIM] = _softmax_result(acc_ref.at[hh]).astype(BF16)


def _moba_call(q, kb, vb, kbar, slopes, batch, seq):
    n, d = q.shape
    nh = d // 2 // HEAD_DIM
    t = MOBA_BLOCK
    tk = _key_chunk(seq)
    assert seq % t == 0
    nq = seq // t
    assert nq <= LANES - 2
    hp = ATT_HEADS_PER_STEP
    assert nh % hp == 0
    ng = nh // hp
    est = hp * (2 * 2 * seq * HEAD_DIM * 2 + 2 * t * 2 * HEAD_DIM * 2 + t * 3 * LANES * 4 + 8 * t * tk * 4)
    return pl.pallas_call(
        functools.partial(_moba_body, t=t, tk=tk, n_blocks=nq),
        out_shape=jax.ShapeDtypeStruct((n, d // 2), BF16),
        grid=(batch, ng, nq),
        in_specs=[pl.BlockSpec((hp, 1, LANES), lambda b, h, i: (h, 0, 0)),
                  pl.BlockSpec((t, hp * HEAD_DIM), lambda b, h, i: (b * nq + i, ng + h)),
                  pl.BlockSpec((seq, hp * HEAD_DIM), lambda b, h, i: (b, ng + h)),
                  pl.BlockSpec((seq, hp * HEAD_DIM), lambda b, h, i: (b, ng + h)),
                  pl.BlockSpec((1, LANES, hp * HEAD_DIM), lambda b, h, i: (b, 0, h))],
        out_specs=pl.BlockSpec((t, hp * HEAD_DIM), lambda b, h, i: (b * nq + i, h)),
        scratch_shapes=[pltpu.VMEM((hp, t, 2 * HEAD_DIM), BF16), pltpu.VMEM((hp, t, LANES), F32),
                        pltpu.VMEM((hp, t, 2 * LANES), F32)],
        compiler_params=pltpu.CompilerParams(dimension_semantics=("parallel", "parallel", "parallel"),
                                             vmem_limit_bytes=_vmem_limit(est)),
        name="moba_attn_prompt",
    )(slopes, q, kb, vb, kbar)


def _head_rows(page_ref, h, page, n_heads):
    return page_ref[0, pl.ds(h, page, stride=n_heads), :]


def _sample_a_body(pt_ref, *refs, pps, n_pages, nh, lam_init):
    del pt_ref
    kp = refs[:pps]
    vp = refs[pps:2 * pps]
    (q_ref, kn_ref, vn_ref, slope_ref, lq1_ref, lk1_ref, lq2_ref, lk2_ref, subg_ref,
     oa_ref, sel_ref, w_ref, m_ref, l_ref, acc_ref, gate_ref) = refs[2 * pps:]
    c = pl.program_id(1)
    n_heads = 2 * nh
    page = kp[0].shape[1] // n_heads
    past = n_pages * page
    nd = 2 * nh
    qrows = 2 * nd

    @pl.when(c == 0)
    def _():
        r = lax.broadcasted_iota(jnp.int32, (qrows, HEAD_DIM), 0)
        ln = lax.broadcasted_iota(jnp.int32, (qrows, HEAD_DIM), 1)
        w = jnp.zeros((qrows, HEAD_DIM), F32)
        for h in range(n_heads):
            qb = jnp.broadcast_to(q_ref[0, h:h + 1, :], (qrows, HEAD_DIM))
            if h < nh:
                keep = ((r == 2 * h) & (ln < HALF_A)) | ((r == 2 * h + 1) & (ln >= HALF_A))
                w = jnp.where(keep, qb * (HALF_A ** -0.5), w)
            else:
                hi = qb.astype(BF16).astype(F32)
                w = jnp.where(r == nd + 2 * (h - nh), hi, jnp.where(r == nd + 2 * (h - nh) + 1, qb - hi, w))
        w_ref[...] = w.astype(BF16)
        m_ref[...] = jnp.full(m_ref.shape, NEG, F32)
        l_ref[...] = jnp.zeros(l_ref.shape, F32)
        acc_ref[...] = jnp.zeros(acc_ref.shape, F32)
        gate_ref[...] = jnp.zeros(gate_ref.shape, F32)

    pcols = page * n_heads
    width = pps * pcols
    w = w_ref[...]
    s = jnp.concatenate([_dot_nt(w, kp[p][0].astype(BF16)) for p in range(pps)], axis=1)
    col = lax.broadcasted_iota(jnp.int32, (1, width), 1)
    row = lax.broadcasted_iota(jnp.int32, (qrows, 1), 0)
    row_head = jnp.where(row < nd, row // 2, nh + (row - nd) // 2)
    own = jnp.bitwise_and(col, n_heads - 1) == row_head
    kpos = c * (pps * page) + col // n_heads
    dist = (past - kpos).astype(F32)
    slope_rows = jnp.concatenate(
        [jnp.broadcast_to(slope_ref[h][:, 0:1], (2, 1)) for h in range(nh)], axis=0)
    sd = jnp.where(own[0:nd], s[0:nd] - slope_rows * dist, NEG)
    m_old = m_ref[...]
    m_new = jnp.maximum(m_old, jnp.max(sd, axis=-1, keepdims=True))
    pr = jnp.exp(sd - m_new[:, 0:1])
    alpha = jnp.exp(m_old - m_new)
    l_ref[...] = alpha * l_ref[...] + jnp.sum(pr, axis=-1, keepdims=True)
    pv = jnp.zeros((nd, HEAD_DIM), F32)
    for p in range(pps):
        pv = pv + _dot(pr[:, p * pcols:(p + 1) * pcols].astype(BF16), vp[p][0].astype(BF16))
    acc_ref[...] = alpha * acc_ref[...] + pv
    m_ref[...] = m_new
    sg = jnp.where(own[nd:], s[nd:], 0.0)
    bcols = MOBA_BLOCK * n_heads
    glane = lax.broadcasted_iota(jnp.int32, (qrows - nd, LANES), 1)
    gate_add = jnp.zeros((qrows - nd, LANES), F32)
    for bi in range(width // bcols):
        gsum = jnp.sum(sg[:, bi * bcols:(bi + 1) * bcols], axis=-1, keepdims=True)
        gate_add = gate_add + jnp.where(glane == c * (width // bcols) + bi, gsum, 0.0)
    gate_ref[...] += gate_add

    @pl.when(c == pl.num_programs(1) - 1)
    def _():
        wf = w_ref[...].astype(F32)[0:nd]
        twice = lambda ref: jnp.concatenate(
            [jnp.broadcast_to(ref[0, h:h + 1, :], (2, HEAD_DIM)) for h in range(nh)], axis=0)
        s_new = jnp.sum(wf * twice(kn_ref), axis=-1, keepdims=True)
        m_old = m_ref[...]
        m_new = jnp.maximum(m_old, s_new)
        p_new = jnp.exp(s_new - m_new)
        alpha = jnp.exp(m_old - m_new)
        l = alpha * l_ref[...] + p_new
        o = (alpha * acc_ref[...] + p_new * twice(vn_ref)) / l
        lam = _diff_lambda(lq1_ref, lk1_ref, lq2_ref, lk2_ref, lam_init)
        for h in range(nh):
            oa = o[2 * h:2 * h + 1, :] - lam * o[2 * h + 1:2 * h + 2, :]
            oa_ref[0, :, h * HEAD_DIM:(h + 1) * HEAD_DIM] = _rms(oa, subg_ref[...]) * (1.0 - lam_init)
        n_past = past // MOBA_BLOCK
        g = jnp.concatenate([gate_ref[2 * h:2 * h + 1, :] + gate_ref[2 * h + 1:2 * h + 2, :]
                             for h in range(nh)], axis=0) * (1.0 / MOBA_BLOCK)
        gl = lax.broadcasted_iota(jnp.int32, g.shape, 1)
        g = jnp.where(gl < n_past, g, NEG)
        cnt = jnp.zeros(g.shape, jnp.int32)
        for n2 in range(n_past):
            gn = g[:, n2:n2 + 1]
            cnt = cnt + jnp.where((gn > g) | ((gn == g) & (gl > n2)), 1, 0)
        sel = jnp.zeros(g.shape, jnp.int32)
        for j in range(MOBA_TOPK):
            idx = jnp.sum(jnp.where((cnt == j) & (gl < n_past), gl, 0), axis=-1, keepdims=True)
            sel = jnp.where(gl == j, idx, sel)
        sel_ref[0] = sel


def _sample_a_call(page_table, cache_k, cache_v, layer, q, k_new, v_new, slopes, lam_vecs, subg, lam_init):
    db, n_pages = page_table.shape
    n_layers, n_pool, page, n_heads, hd = cache_k.shape
    d = n_heads * hd
    nh = n_heads // 2
    assert (n_pages * page) % MOBA_BLOCK == 0 and MOBA_BLOCK % page == 0
    assert n_pages * page // MOBA_BLOCK <= LANES
    pps = math.gcd(PAGES_PER_STEP, n_pages)
    ck = cache_k.reshape(n_layers * n_pool, page * n_heads, hd)
    cv = cache_v.reshape(n_layers * n_pool, page * n_heads, hd)
    off = layer * n_pool

    def page_spec(p):
        return pl.BlockSpec((1, page * n_heads, hd), lambda b, c, pt: (pt[b, c * pps + p] + off, 0, 0))

    tok = pl.BlockSpec((1, n_heads, hd), lambda b, c, pt: (b, 0, 0))
    fixed = lambda shape: pl.BlockSpec(shape, lambda b, c, pt: (0,) * len(shape))
    in_specs = ([page_spec(p) for p in range(pps)] * 2
                + [tok, tok, tok, fixed((nh, 1, LANES)),
                   fixed((1, HALF_A)), fixed((1, HALF_A)), fixed((1, HALF_A)), fixed((1, HALF_A)),
                   fixed((1, HEAD_DIM))])
    assert pps % (MOBA_BLOCK // page) == 0
    nd = 2 * nh
    est = 2 * 2 * pps * page * d * 4 + 8 * 2 * nd * pps * page * n_heads * 4
    to_heads = lambda a: a.reshape(db, n_heads, hd)
    oa, sel = pl.pallas_call(
        functools.partial(_sample_a_body, pps=pps, n_pages=n_pages, nh=nh, lam_init=lam_init),
        out_shape=[jax.ShapeDtypeStruct((db, 1, d // 2), F32),
                   jax.ShapeDtypeStruct((db, nh, LANES), jnp.int32)],
        grid_spec=pltpu.PrefetchScalarGridSpec(
            num_scalar_prefetch=1,
            grid=(db, n_pages // pps),
            in_specs=in_specs,
            out_specs=[pl.BlockSpec((1, 1, d // 2), lambda b, c, pt: (b, 0, 0)),
                       pl.BlockSpec((1, nh, LANES), lambda b, c, pt: (b, 0, 0))],
            scratch_shapes=[pltpu.VMEM((2 * nd, HEAD_DIM), BF16), pltpu.VMEM((nd, LANES), F32),
                            pltpu.VMEM((nd, LANES), F32), pltpu.VMEM((nd, HEAD_DIM), F32),
                            pltpu.VMEM((nd, LANES), F32)]),
        compiler_params=pltpu.CompilerParams(dimension_semantics=("parallel", "arbitrary"),
                                             vmem_limit_bytes=_vmem_limit(est)),
        name="sample_attn_a",
    )(page_table, *([ck] * pps), *([cv] * pps), to_heads(q), to_heads(k_new), to_heads(v_new),
      slopes, *lam_vecs, subg)
    return oa, sel


def _sample_b_body(pt_ref, sel_ref, *refs, n_sel, ppb, n_pages, nh):
    del pt_ref
    n_op = n_sel * ppb
    kp = refs[:n_op]
    vp = refs[n_op:2 * n_op]
    q_ref, kn_ref, vn_ref, slope_ref, o_ref = refs[2 * n_op:]
    b = pl.program_id(0)
    h = pl.program_id(1)
    n_heads = 2 * nh
    page = kp[0].shape[1] // n_heads
    past = n_pages * page
    slope = slope_ref[0][:, 0:1]
    scale = HEAD_DIM ** -0.5
    q = jnp.broadcast_to(q_ref[0], (SUBLANES, HEAD_DIM)).astype(BF16)
    lane = lax.broadcasted_iota(jnp.int32, (SUBLANES, page), 1)
    scores = []
    for j in range(n_sel):
        blk = sel_ref[b, h * n_sel + j]
        for r in range(ppb):
            kh = _head_rows(kp[j * ppb + r], nh + h, page, n_heads).astype(BF16)
            s = _dot_nt(q, kh) * scale
            dist = (past - (blk * MOBA_BLOCK + r * page) - lane).astype(F32)
            scores.append(s - slope * dist)
    s_new = jnp.sum(q_ref[0] * kn_ref[0], axis=-1, keepdims=True) * scale
    m = s_new
    if scores:
        s = jnp.concatenate(scores, axis=1)
        m = jnp.maximum(jnp.max(s, axis=-1, keepdims=True)[0:1], s_new)
        pr = jnp.exp(s - m)
        l = jnp.sum(pr, axis=-1, keepdims=True)[0:1]
        acc = jnp.zeros((SUBLANES, HEAD_DIM), F32)
        for j in range(n_op):
            vh = _head_rows(vp[j], nh + h, page, n_heads).astype(BF16)
            acc = acc + _dot(pr[:, j * page:(j + 1) * page].astype(BF16), vh)
        acc = acc[0:1]
    else:
        l = jnp.zeros((1, 1), F32)
        acc = jnp.zeros((1, HEAD_DIM), F32)
    p_new = jnp.exp(s_new - m)
    o_ref[0] = (acc + p_new * vn_ref[0]) / (l + p_new)


def _sample_b_call(page_table, sel, cache_k, cache_v, layer, q, k_new, v_new, slopes):
    db, n_pages = page_table.shape
    n_layers, n_pool, page, n_heads, hd = cache_k.shape
    d = n_heads * hd
    nh = n_heads // 2
    ppb = MOBA_BLOCK // page
    n_past = n_pages * page // MOBA_BLOCK
    n_sel = min(MOBA_TOPK, n_past)
    ck = cache_k.reshape(n_layers * n_pool, page * n_heads, hd)
    cv = cache_v.reshape(n_layers * n_pool, page * n_heads, hd)
    off = layer * n_pool

    def page_spec(j, r):
        return pl.BlockSpec((1, page * n_heads, hd),
                            lambda b, h, pt, sl: (pt[b, sl[b, h * n_sel + j] * ppb + r] + off, 0, 0))

    tok = pl.BlockSpec((1, 1, hd), lambda b, h, pt, sl: (b, 0, nh + h))
    specs = [page_spec(j, r) for j in range(n_sel) for r in range(ppb)]
    in_specs = specs + specs + [tok, tok, tok, pl.BlockSpec((1, 1, LANES), lambda b, h, pt, sl: (h, 0, 0))]
    n_op = n_sel * ppb
    est = 2 * 2 * n_op * page * d * 4 + 64 * 1024
    return pl.pallas_call(
        functools.partial(_sample_b_body, n_sel=n_sel, ppb=ppb, n_pages=n_pages, nh=nh),
        out_shape=jax.ShapeDtypeStruct((db, 1, d // 2), F32),
        grid_spec=pltpu.PrefetchScalarGridSpec(
            num_scalar_prefetch=2,
            grid=(db, nh),
            in_specs=in_specs,
            out_specs=pl.BlockSpec((1, 1, hd), lambda b, h, pt, sl: (b, 0, h))),
        compiler_params=pltpu.CompilerParams(dimension_semantics=("parallel", "parallel"),
                                             vmem_limit_bytes=_vmem_limit(est)),
        name="sample_attn_b",
    )(page_table, sel[:, :, :n_sel].reshape(db, nh * n_sel), *([ck] * n_op), *([cv] * n_op),
      q, k_new, v_new, slopes)


def _rec_in_body(x_ref, g_ref, w_ref, qkvo_ref, gates_ref, xr_ref, gr_ref, *, dc, drg):
    x = x_ref[...]
    h = _rms(x, g_ref[...]).astype(BF16)
    kscale = HEAD_DIM ** -0.5
    qkvo_ref[:, 0:dc] = _dot(h, w_ref[:, 0:dc]).astype(qkvo_ref.dtype)
    qkvo_ref[:, dc:2 * dc] = (_dot(h, w_ref[:, dc:2 * dc]) * kscale).astype(qkvo_ref.dtype)
    qkvo_ref[:, 2 * dc:3 * dc] = _dot(h, w_ref[:, 2 * dc:3 * dc]).astype(qkvo_ref.dtype)
    qkvo_ref[:, 3 * dc:4 * dc] = _dot(h, w_ref[:, 3 * dc:4 * dc]).astype(qkvo_ref.dtype)
    xr_ref[...] = _dot(h, w_ref[:, 4 * dc:4 * dc + drg])
    gr_ref[...] = _dot(h, w_ref[:, 4 * dc + drg:4 * dc + 2 * drg])
    gates_ref[...] = _dot(h, w_ref[:, 4 * dc + 2 * drg:])


def _rec_in_call(x, g, w_packed, dc, drg, qkvo_dtype):
    n, d = x.shape
    tm = min(n, TOKEN_TILE)
    assert n % tm == 0
    row = lambda w: pl.BlockSpec((tm, w), lambda i: (i, 0))
    est = 2 * w_packed.size * 2 + 2 * tm * d * 4 + 2 * tm * (4 * dc + 2 * drg + LANES) * 4
    return pl.pallas_call(
        functools.partial(_rec_in_body, dc=dc, drg=drg),
        out_shape=[jax.ShapeDtypeStruct((n, 4 * dc), qkvo_dtype), jax.ShapeDtypeStruct((n, LANES), F32),
                   jax.ShapeDtypeStruct((n, drg), F32), jax.ShapeDtypeStruct((n, drg), F32)],
        grid=(n // tm,),
        in_specs=[row(d), _full((1, d)), _full(w_packed.shape)],
        out_specs=[row(4 * dc), row(LANES), row(drg), row(drg)],
        compiler_params=pltpu.CompilerParams(dimension_semantics=("parallel",),
                                             vmem_limit_bytes=_vmem_limit(est)),
        name="rec_in",
    )(x, g.reshape(1, d), w_packed)


def _log_sigmoid(x):
    return jnp.minimum(x, 0.0) - jnp.log1p(jnp.exp(-jnp.abs(x)))


def _softplus(x):
    return jnp.maximum(x, 0.0) + jnp.log1p(jnp.exp(-jnp.abs(x)))


def _mlstm_body(qkvo_ref, gates_ref, bias_ref, ng_ref, hc_ref, c_out, n_out, m_out,
                c_ref, n_ref, m_ref, *, nh):
    ch = pl.program_id(1)
    nseq, L = qkvo_ref.shape[0], qkvo_ref.shape[1]

    @pl.when(ch == 0)
    def _():
        c_ref[...] = jnp.zeros(c_ref.shape, F32)
        n_ref[...] = jnp.zeros(n_ref.shape, F32)
        m_ref[...] = jnp.zeros(m_ref.shape, F32)

    lane = lax.broadcasted_iota(jnp.int32, (L, LANES), 1)
    tt = lax.broadcasted_iota(jnp.int32, (L, L), 0)
    ss = lax.broadcasted_iota(jnp.int32, (L, L), 1)
    causal = ss <= tt
    tril = jnp.where(causal, 1.0, 0.0).astype(BF16)
    for sq in range(nseq):
        g = gates_ref[sq] + bias_ref[...]
        x = jnp.where(lane < nh, g, jnp.where(lane < 2 * nh, _log_sigmoid(g), 0.0))
        x_hi, x_lo = _split_bf16(x)
        x_mid, x_lo = _split_bf16(x - x_hi.astype(F32))
        cum = _dot(tril, x_hi) + _dot(tril, x_mid) + _dot(tril, x_lo)
        colv = jnp.where(lane < nh, x, cum)
        rowv = colv.T
        for h in range(nh):
            _mlstm_head(qkvo_ref.at[sq], hc_ref.at[sq], ng_ref, c_ref.at[sq], n_ref.at[sq], m_ref.at[sq],
                        colv, rowv, causal, h, nh)

    @pl.when(ch == pl.num_programs(1) - 1)
    def _():
        c_out[...] = c_ref[...]
        n_out[...] = n_ref[...]
        m_out[...] = m_ref[...]


def _mlstm_head(qkvo_ref, hc_ref, ng_ref, c_ref, n_ref, m_ref, colv, rowv, causal, h, nh):
    L = qkvo_ref.shape[0]
    dc = nh * HEAD_DIM
    cs = slice(h * HEAD_DIM, (h + 1) * HEAD_DIM)
    q = qkvo_ref[:, cs]
    k = qkvo_ref[:, dc + h * HEAD_DIM:dc + (h + 1) * HEAD_DIM]
    v = qkvo_ref[:, 2 * dc + h * HEAD_DIM:2 * dc + (h + 1) * HEAD_DIM]
    o = qkvo_ref[:, 3 * dc + h * HEAD_DIM:3 * dc + (h + 1) * HEAD_DIM].astype(F32)
    rep = lambda col: jnp.broadcast_to(col, (L, LANES))
    ig_t = rep(colv[:, h:h + 1])
    b_t = rep(colv[:, nh + h:nh + h + 1])
    ig_row = rowv[h:h + 1, :]
    b_row = rowv[nh + h:nh + h + 1, :]
    m_prev = m_ref[h]
    c_prev = c_ref[h]
    n_prev = n_ref[h]

    dmat = jnp.where(causal, b_t - b_row + ig_row, NEG)
    inter = b_t + m_prev
    m_t = jnp.maximum(inter, rep(jnp.max(dmat, axis=-1, keepdims=True)))
    w_inter = jnp.exp(inter - m_t)
    s = _dot_nt(q, k) * jnp.exp(dmat - m_t)
    qf = q.astype(F32)
    num = w_inter * _dot(q, c_prev.astype(BF16)) + _dot(s.astype(BF16), v)
    den = (w_inter * rep(jnp.sum(qf * n_prev, axis=-1, keepdims=True))
           + rep(jnp.sum(s, axis=-1, keepdims=True)))
    hh = num / jnp.maximum(jnp.abs(den), jnp.exp(-m_t))
    m_new = m_t[L - 1:L, :]
    b_last = b_t[L - 1:L, :]
    w_old = jnp.exp(b_last + m_prev - m_new)
    w_new = jnp.exp(b_last - b_t + ig_t - m_new)
    kw = k.astype(F32) * w_new
    c_ref[h] = w_old * c_prev + _dot_tn(kw.astype(BF16), v)
    n_ref[h] = w_old * n_prev + jnp.sum(kw, axis=0, keepdims=True)
    m_ref[h] = m_new
    hc_ref[:, cs] = (_rms(hh, ng_ref[:, cs]) * jax.nn.sigmoid(o)).astype(hc_ref.dtype)


def _mlstm_call(qkvo, gates, bias, norm_g, batch, seq):
    n = qkvo.shape[0]
    dc = qkvo.shape[1] // 4
    nh = dc // HEAD_DIM
    L = MLSTM_CHUNK if seq % MLSTM_CHUNK == 0 else seq
    assert L % SUBLANES == 0 and L == LANES, "prompt mLSTM kernel needs 128-token chunks"
    nc = seq // L
    ns = math.gcd(MLSTM_SEQS_PER_STEP, batch)
    est = ns * (2 * L * 4 * dc * 2 + 2 * L * LANES * 4 + 2 * L * dc * 2 + 3 * nh * HEAD_DIM * HEAD_DIM * 4
                + 16 * L * L * 4)
    hc, c1, n1, m1 = pl.pallas_call(
        functools.partial(_mlstm_body, nh=nh),
        out_shape=[jax.ShapeDtypeStruct((batch, seq, dc), BF16),
                   jax.ShapeDtypeStruct((batch, nh, HEAD_DIM, HEAD_DIM), F32),
                   jax.ShapeDtypeStruct((batch, nh, 1, HEAD_DIM), F32),
                   jax.ShapeDtypeStruct((batch, nh, 1, LANES), F32)],
        grid=(batch // ns, nc),
        in_specs=[pl.BlockSpec((ns, L, 4 * dc), lambda b, c: (b, c, 0)),
                  pl.BlockSpec((ns, L, LANES), lambda b, c: (b, c, 0)),
                  _full((1, LANES)), _full((1, dc))],
        out_specs=[pl.BlockSpec((ns, L, dc), lambda b, c: (b, c, 0)),
                   pl.BlockSpec((ns, nh, HEAD_DIM, HEAD_DIM), lambda b, c: (b, 0, 0, 0)),
                   pl.BlockSpec((ns, nh, 1, HEAD_DIM), lambda b, c: (b, 0, 0, 0)),
                   pl.BlockSpec((ns, nh, 1, LANES), lambda b, c: (b, 0, 0, 0))],
        scratch_shapes=[pltpu.VMEM((ns, nh, HEAD_DIM, HEAD_DIM), F32), pltpu.VMEM((ns, nh, 1, HEAD_DIM), F32),
                        pltpu.VMEM((ns, nh, 1, LANES), F32)],
        compiler_params=pltpu.CompilerParams(dimension_semantics=("parallel", "arbitrary"),
                                             vmem_limit_bytes=_vmem_limit(est)),
        name="mlstm_prompt",
    )(qkvo.reshape(batch, seq, 4 * dc), gates.reshape(batch, seq, LANES), bias, norm_g)
    return hc.reshape(n, dc), c1, n1, m1


def _rg_gates(xc, wa_ref, ba_ref, wx_ref, bx_ref, lam_ref):
    xb = xc.astype(BF16)
    r = jax.nn.sigmoid(_dot(xb, wa_ref[...]) + ba_ref[...])
    i = jax.nn.sigmoid(_dot(xb, wx_ref[...]) + bx_ref[...])
    log_a = -RG_C * r * _softplus(-lam_ref[...])
    a = jnp.exp(log_a)
    u = jnp.sqrt(-jnp.tanh(log_a) * (a * a + 1.0)) * (i * xc)
    return a, u


def _rglru_body(xr_ref, gr_ref, cw_ref, cb_ref, wa_ref, ba_ref, wx_ref, bx_ref, lam_ref,
                y_ref, h_out, xbuf_ref, a_ref, u_ref, h_ref, *, width):
    tstep = pl.program_id(1)
    T, C = xr_ref.shape
    pad = SUBLANES

    @pl.when(tstep == 0)
    def _():
        xbuf_ref[0:pad, :] = jnp.zeros((pad, C), F32)
        h_ref[...] = jnp.zeros(h_ref.shape, F32)

    xbuf_ref[pad:pad + T, :] = xr_ref[...]
    xc = cb_ref[...]
    for j in range(width):
        xc = xc + xbuf_ref[pl.ds(pad - (width - 1) + j, T), :] * cw_ref[j:j + 1, :]
    xbuf_ref[0:pad, :] = xbuf_ref[T:T + pad, :]

    a, u = _rg_gates(xc, wa_ref, ba_ref, wx_ref, bx_ref, lam_ref)
    a_ref[...] = a
    u_ref[...] = u
    sub = lax.broadcasted_iota(jnp.int32, (SUBLANES, C), 0)

    def group(gi, h):
        start = pl.multiple_of(gi * SUBLANES, SUBLANES)
        ag = a_ref[pl.ds(start, SUBLANES), :]
        ug = u_ref[pl.ds(start, SUBLANES), :]
        for sh in (1, 2, 4):
            ap = jnp.where(sub >= sh, pltpu.roll(ag, sh, 0), 1.0)
            up = jnp.where(sub >= sh, pltpu.roll(ug, sh, 0), 0.0)
            ug = ag * up + ug
            ag = ag * ap
        hg = ag * h + ug
        u_ref[pl.ds(start, SUBLANES), :] = hg
        return hg[SUBLANES - 1:SUBLANES, :]

    h_last = lax.fori_loop(0, T // SUBLANES, group, h_ref[...])
    h_ref[...] = h_last
    y_ref[...] = (u_ref[...] * jax.nn.gelu(gr_ref[...], approximate=True)).astype(y_ref.dtype)

    @pl.when(tstep == pl.num_programs(1) - 1)
    def _():
        h_out[0] = h_last


def _rglru_call(xr, gr, conv_w, conv_b, wa, ba, wx, bx, lam, batch, seq):
    n, c = xr.shape
    width = conv_w.shape[0]
    t = min(RG_TILE, seq)
    assert seq % t == 0 and t % SUBLANES == 0 and width - 1 <= SUBLANES
    nt = seq // t
    row = pl.BlockSpec((t, c), lambda b, s: (b * nt + s, 0))
    est = 6 * t * c * 4 + 2 * c * c * 2 * 2 + 8 * t * c * 4
    return pl.pallas_call(
        functools.partial(_rglru_body, width=width),
        out_shape=[jax.ShapeDtypeStruct((n, c), BF16), jax.ShapeDtypeStruct((batch, 1, c), F32)],
        grid=(batch, nt),
        in_specs=[row, row, _full(conv_w.shape), _full((1, c)), _full(wa.shape), _full((1, c)),
                  _full(wx.shape), _full((1, c)), _full((1, c))],
        out_specs=[row, pl.BlockSpec((1, 1, c), lambda b, s: (b, 0, 0))],
        scratch_shapes=[pltpu.VMEM((t + SUBLANES, c), F32), pltpu.VMEM((t, c), F32),
                        pltpu.VMEM((t, c), F32), pltpu.VMEM((1, c), F32)],
        compiler_params=pltpu.CompilerParams(dimension_semantics=("parallel", "arbitrary"),
                                             vmem_limit_bytes=_vmem_limit(est)),
        name="rglru_prompt",
    )(xr, gr, conv_w, conv_b, wa, ba, wx, bx, lam)


def _to_column(row_vec):
    n = row_vec.shape[1]
    r = lax.broadcasted_iota(jnp.int32, (n, n), 0)
    c = lax.broadcasted_iota(jnp.int32, (n, n), 1)
    return jnp.sum(jnp.where(r == c, jnp.broadcast_to(row_vec, (n, n)), 0.0), axis=-1, keepdims=True)


def _rec_sample_body(qkvo_ref, gates_ref, xr_ref, gr_ref, c0_ref, n0_ref, m0_ref, h0_ref, conv0_ref,
                     bias_ref, ng_ref, cw_ref, cb_ref, wa_ref, ba_ref, wx_ref, bx_ref, lam_ref,
                     hc_ref, y_ref, c_out, n_out, m_out, h_out, conv_out, *, nh, width):
    dc = nh * HEAD_DIM
    g = gates_ref[0] + bias_ref[...]
    m0 = m0_ref[0]
    for h in range(nh):
        cs = slice(h * HEAD_DIM, (h + 1) * HEAD_DIM)
        q = qkvo_ref[0][:, cs]
        k = qkvo_ref[0][:, dc + h * HEAD_DIM:dc + (h + 1) * HEAD_DIM]
        v = qkvo_ref[0][:, 2 * dc + h * HEAD_DIM:2 * dc + (h + 1) * HEAD_DIM]
        o = qkvo_ref[0][:, 3 * dc + h * HEAD_DIM:3 * dc + (h + 1) * HEAD_DIM]
        ig = g[:, h:h + 1]
        lf = _log_sigmoid(g[:, nh + h:nh + h + 1])
        m_prev = m0[:, h:h + 1]
        c_prev = c0_ref[0, h]
        n_prev = n0_ref[0, h:h + 1, :]
        inter = lf + m_prev
        m_t = jnp.maximum(inter, ig)
        w_inter = jnp.exp(inter - m_t)
        s = jnp.sum(q * k, axis=-1, keepdims=True) * jnp.exp(ig - m_t)
        q_col = _to_column(q)
        qc = jnp.sum(q_col * c_prev, axis=0, keepdims=True)
        num = w_inter * qc + s * v
        den = w_inter * jnp.sum(q * n_prev, axis=-1, keepdims=True) + s
        hh = num / jnp.maximum(jnp.abs(den), jnp.exp(-m_t))
        w_old = jnp.exp(lf + m_prev - m_t)
        w_new = jnp.exp(ig - m_t)
        c_out[0, h] = w_old * c_prev + (_to_column(k) * w_new) * v
        n_out[0, h:h + 1, :] = w_old * n_prev + w_new * k
        m_out[0, :, h:h + 1] = m_t
        hc_ref[0, :, cs] = (_rms(hh, ng_ref[:, cs]) * jax.nn.sigmoid(o)).astype(hc_ref.dtype)

    x = xr_ref[0]
    xc = cb_ref[...]
    for j in range(width - 1):
        xc = xc + conv0_ref[0, j:j + 1, :] * cw_ref[j:j + 1, :]
    xc = xc + x * cw_ref[width - 1:width, :]
    a, u = _rg_gates(jnp.broadcast_to(xc, (SUBLANES, xc.shape[1])), wa_ref, ba_ref, wx_ref, bx_ref, lam_ref)
    h_new = a[0:1] * h0_ref[0] + u[0:1]
    h_out[0] = h_new
    y_ref[0] = (h_new * jax.nn.gelu(gr_ref[0], approximate=True)).astype(y_ref.dtype)
    for j in range(width - 2):
        conv_out[0, j:j + 1, :] = conv0_ref[0, j + 1:j + 2, :]
    conv_out[0, width - 2:width - 1, :] = x


def _rec_sample_call(qkvo, gates, xr, gr, c0, n0, m0, h0, conv0, bias, norm_g, conv_w, conv_b,
                     wa, ba, wx, bx, lam):
    db = qkvo.shape[0]
    dc = qkvo.shape[1] // 4
    nh = dc // HEAD_DIM
    c = xr.shape[1]
    width = conv_w.shape[0]
    tok = lambda w: pl.BlockSpec((1, 1, w), lambda b: (b, 0, 0))
    in_specs = [tok(4 * dc), tok(LANES), tok(c), tok(c),
                pl.BlockSpec((1, nh, HEAD_DIM, HEAD_DIM), lambda b: (b, 0, 0, 0)),
                pl.BlockSpec((1, nh, HEAD_DIM), lambda b: (b, 0, 0)),
                tok(nh), tok(c),
                pl.BlockSpec((1, width - 1, c), lambda b: (b, 0, 0)),
                _full((1, LANES)), _full((1, dc)), _full(conv_w.shape), _full((1, c)),
                _full(wa.shape), _full((1, c)), _full(wx.shape), _full((1, c)), _full((1, c))]
    out_shape = [jax.ShapeDtypeStruct((db, 1, dc), BF16), jax.ShapeDtypeStruct((db, 1, c), BF16),
                 jax.ShapeDtypeStruct((db, nh, HEAD_DIM, HEAD_DIM), F32),
                 jax.ShapeDtypeStruct((db, nh, HEAD_DIM), F32),
                 jax.ShapeDtypeStruct((db, 1, nh), F32),
                 jax.ShapeDtypeStruct((db, 1, c), F32),
                 jax.ShapeDtypeStruct((db, width - 1, c), F32)]
    out_specs = [tok(dc), tok(c),
                 pl.BlockSpec((1, nh, HEAD_DIM, HEAD_DIM), lambda b: (b, 0, 0, 0)),
                 pl.BlockSpec((1, nh, HEAD_DIM), lambda b: (b, 0, 0)),
                 tok(nh), tok(c),
                 pl.BlockSpec((1, width - 1, c), lambda b: (b, 0, 0))]
    est = 4 * nh * HEAD_DIM * HEAD_DIM * 4 + 2 * c * c * 2 * 2 + 1024 * 1024
    return pl.pallas_call(
        functools.partial(_rec_sample_body, nh=nh, width=width),
        out_shape=out_shape,
        grid=(db,),
        in_specs=in_specs,
        out_specs=out_specs,
        compiler_params=pltpu.CompilerParams(dimension_semantics=("parallel",),
                                             vmem_limit_bytes=_vmem_limit(est)),
        name="rec_sample",
    )(qkvo.reshape(db, 1, 4 * dc), gates.reshape(db, 1, LANES), xr.reshape(db, 1, c), gr.reshape(db, 1, c),
      c0, n0, m0.reshape(db, 1, nh), h0.reshape(db, 1, c), conv0, bias, norm_g, conv_w, conv_b,
      wa, ba, wx, bx, lam)


def _block_diag(w):
    n, k, j = w.shape
    eye = jnp.eye(n, dtype=w.dtype)
    return (eye[:, None, :, None] * w[:, :, None, :]).reshape(n * k, n * j)


def _lane_rows(vals):
    return jnp.broadcast_to(vals.astype(F32)[:, None, None], (vals.shape[0], 1, LANES))


def kernel(x_prompt, x_sample, cache_k, cache_v, state_mlstm_c, state_mlstm_n, state_mlstm_m, state_rglru_h, state_rglru_conv, page_table, norm_g, ffn_w_gate, ffn_w_up, ffn_w_down, att_w_in, att_w_out, diff_lambda_q1, diff_lambda_k1, diff_lambda_q2, diff_lambda_k2, diff_subln_g, rec_w_in, rec_w_out, mlstm_b_i, mlstm_b_f, mlstm_norm_g, rg_conv_w, rg_conv_b, rg_w_a, rg_b_a, rg_w_x, rg_b_x, rg_lambda, final_norm_g):
    batch, seq, d = x_prompt.shape
    db, dseq, _ = x_sample.shape
    assert dseq == 1, "the sample group holds one new token per sequence"
    depth = norm_g.shape[0]
    n_heads = cache_k.shape[3]
    dc = state_mlstm_c.shape[2] * HEAD_DIM
    nhc = dc // HEAD_DIM
    drg = state_rglru_h.shape[2]
    width = state_rglru_conv.shape[2] + 1

    xp = x_prompt.reshape(batch * seq, d)
    xs = x_sample.reshape(db, d)

    hidx = jnp.arange(n_heads, dtype=F32)
    slopes = 2.0 ** (-8.0 * (hidx + 1.0) / n_heads)
    slopes_a, slopes_b = _lane_rows(slopes[0::2]), _lane_rows(slopes[1::2])

    k_p, v_p, k_s, v_s = [], [], [], []
    rec_p, rec_s = [], []
    mix_p = mix_s = None
    wg, wu, wd = (w.astype(BF16) for w in (ffn_w_gate, ffn_w_up, ffn_w_down))
    for l in range(depth):
        j = l // 2
        xp = _ffn_call(xp, norm_g[l, 0], wg, wu, wd, l, 0)
        xs = _ffn_call(xs, norm_g[l, 0], wg, wu, wd, l, 0)
        if l % 2 == 0:
            lam_init = 0.8 - 0.6 * math.exp(-0.3 * l)
            w_in = att_w_in[j].astype(BF16)
            w_out = att_w_out[j].astype(BF16)
            lam_vecs = [v[j].reshape(1, HALF_A) for v in
                        (diff_lambda_q1, diff_lambda_k1, diff_lambda_q2, diff_lambda_k2)]
            subg = diff_subln_g[j].reshape(1, HEAD_DIM)
            q, k, v, kb, vb, kbar = _qkv_call(xp, norm_g[l, 1], w_in, prompt=True)
            kbar = kbar.reshape(batch, seq // MOBA_BLOCK, d // 2)
            kbar = jnp.pad(kbar, ((0, 0), (0, LANES - seq // MOBA_BLOCK), (0, 0)))
            oa = _diff_call(q, kb, vb, slopes_a, lam_vecs, subg, batch, seq, lam_init)
            ob = _moba_call(q, kb, vb, kbar, slopes_b, batch, seq)
            mix_p = (oa, ob, w_out)
            k_p.append(k.reshape(batch, seq, n_heads, HEAD_DIM))
            v_p.append(v.reshape(batch, seq, n_heads, HEAD_DIM))
            qs, ks, vs = _qkv_call(xs, norm_g[l, 1], w_in, prompt=False)
            qs3, ks3, vs3 = (a.reshape(db, 1, d) for a in (qs, ks, vs))
            oa_s, sel = _sample_a_call(page_table, cache_k, cache_v, j, qs3, ks3, vs3, slopes_a,
                                       lam_vecs, subg, lam_init)
            ob_s = _sample_b_call(page_table, sel, cache_k, cache_v, j, qs3, ks3, vs3, slopes_b)
            mix_s = (oa_s.reshape(db, d // 2).astype(BF16), ob_s.reshape(db, d // 2).astype(BF16), w_out)
            k_s.append(ks.reshape(db, 1, n_heads, HEAD_DIM))
            v_s.append(vs.reshape(db, 1, n_heads, HEAD_DIM))
        else:
            w = rec_w_in[j]
            n_gate = 2 * nhc
            w_packed = jnp.concatenate(
                [w[:, :4 * dc], w[:, 4 * dc + n_gate:],
                 jnp.pad(w[:, 4 * dc:4 * dc + n_gate], ((0, 0), (0, LANES - n_gate)))], axis=1).astype(BF16)
            w_out = rec_w_out[j].astype(BF16)
            bias = jnp.pad(jnp.concatenate([mlstm_b_i[j], mlstm_b_f[j]]), (0, LANES - n_gate)).reshape(1, LANES)
            ng = mlstm_norm_g[j].reshape(1, dc)
            conv_w, conv_b = rg_conv_w[j], rg_conv_b[j].reshape(1, drg)
            wa, wx = _block_diag(rg_w_a[j]).astype(BF16), _block_diag(rg_w_x[j]).astype(BF16)
            ba, bx, lam = (a[j].reshape(1, drg) for a in (rg_b_a, rg_b_x, rg_lambda))
            qkvo, gates, xr, gr = _rec_in_call(xp, norm_g[l, 1], w_packed, dc, drg, BF16)
            hc, c1, n1, m1 = _mlstm_call(qkvo, gates, bias, ng, batch, seq)
            y, h1 = _rglru_call(xr, gr, conv_w, conv_b, wa, ba, wx, bx, lam, batch, seq)
            mix_p = (hc, y, w_out)
            conv1 = xr.reshape(batch, seq, drg)[:, seq - (width - 1):, :]
            rec_p.append((c1, n1.reshape(batch, nhc, HEAD_DIM), m1[:, :, 0, 0], h1.reshape(batch, drg), conv1))
            qkvo_s, gates_s, xr_s, gr_s = _rec_in_call(xs, norm_g[l, 1], w_packed, dc, drg, F32)
            hc_s, y_s, c1s, n1s, m1s, h1s, conv1s = _rec_sample_call(
                qkvo_s, gates_s, xr_s, gr_s, state_mlstm_c[j], state_mlstm_n[j], state_mlstm_m[j],
                state_rglru_h[j], state_rglru_conv[j], bias, ng, conv_w, conv_b, wa, ba, wx, bx, lam)
            mix_s = (hc_s.reshape(db, dc), y_s.reshape(db, drg), w_out)
            rec_s.append((c1s, n1s, m1s.reshape(db, nhc), h1s.reshape(db, drg), conv1s))
        fg = final_norm_g if l == depth - 1 else None
        xp = _ffn_call(xp, norm_g[l, 2], wg, wu, wd, l, 1, mix=mix_p, final_g=fg)
        xs = _ffn_call(xs, norm_g[l, 2], wg, wu, wd, l, 1, mix=mix_s, final_g=fg)

    stack = lambda items: jnp.stack(items, axis=0)
    return (xp.reshape(batch, seq, d), xs.reshape(db, 1, d),
            stack(k_p), stack(v_p), stack(k_s), stack(v_s),
            *(stack([st[i] for st in rec_p]) for i in range(5)),
            *(stack([st[i] for st in rec_s]) for i in range(5)))
```

```python
import functools
import math

import jax
import jax.numpy as jnp
from jax import lax
from jax.experimental import pallas as pl
from jax.experimental.pallas import tpu as pltpu

F32 = jnp.float32
BF16 = jnp.bfloat16

HEAD_DIM = 128
HALF_A = HEAD_DIM // 2
MOBA_BLOCK = 256
MOBA_TOPK = 3
MLSTM_CHUNK = 128
RG_C = 8.0
N_BLOCKS_RG = 8
RMS_EPS = 1e-6
NEG = -1e30

V7X_VMEM_BYTES = 64 * 1024 * 1024
LANES = 128
SUBLANES = 8

TOKEN_TILE = 512
FF_CHUNK = 256
ATT_TILE = 256
ATT_HEADS_PER_STEP = 4
MLSTM_SEQS_PER_STEP = 1
RG_TILE = 512
PAGES_PER_STEP = 16


def _vmem_limit(est_bytes):
    return int(min(max(2 * est_bytes, 32 * 1024 * 1024), V7X_VMEM_BYTES - 8 * 1024 * 1024))


def _rms(x, g):
    return x * lax.rsqrt(jnp.mean(x * x, axis=-1, keepdims=True) + RMS_EPS) * g


def _dot(a, b):
    return jnp.dot(a, b, preferred_element_type=F32)


def _dot_nt(a, b):
    return lax.dot_general(a, b, (((1,), (1,)), ((), ())), preferred_element_type=F32)


def _dot_tn(a, b):
    return lax.dot_general(a, b, (((0,), (0,)), ((), ())), preferred_element_type=F32)


def _split_bf16(x):
    hi = x.astype(BF16)
    lo = (x - hi.astype(F32)).astype(BF16)
    return hi, lo


def _full(shape):
    nd = len(shape)
    return pl.BlockSpec(shape, lambda *_: (0,) * nd)


def _ffn_body(*refs, has_mix, has_final, ff_chunk):
    refs = list(refs)
    x_ref = refs.pop(0)
    if has_mix:
        a_ref, b_ref, wo_ref = refs[:3]
        refs = refs[3:]
    g_ref, wg_ref, wu_ref, wd_ref = refs[:4]
    refs = refs[4:]
    if has_final:
        fg_ref = refs.pop(0)
    o_ref, act_ref = refs

    x = x_ref[...]
    if has_mix:
        da = a_ref.shape[1]
        x = x + _dot(a_ref[...], wo_ref[0:da, :]) + _dot(b_ref[...], wo_ref[da:, :])
    h = _rms(x, g_ref[...]).astype(BF16)
    ff = wg_ref.shape[1]
    for c in range(ff // ff_chunk):
        sl = slice(c * ff_chunk, (c + 1) * ff_chunk)
        gate = _dot(h, wg_ref[:, sl])
        up = _dot(h, wu_ref[:, sl])
        act_ref[:, sl] = (gate * jax.nn.sigmoid(gate) * up).astype(BF16)
    y = x + 0.5 * _dot(act_ref[...], wd_ref[...])
    if has_final:
        y = _rms(y, fg_ref[...])
    o_ref[...] = y


def _ffn_call(x, g, wg, wu, wd, l, j, mix=None, final_g=None):
    n, d = x.shape
    ff = wg.shape[3]
    tm = min(n, TOKEN_TILE)
    assert n % tm == 0 and ff % FF_CHUNK == 0
    row = lambda w: pl.BlockSpec((tm, w), lambda i: (i, 0))
    pick = lambda w: pl.BlockSpec((None, None) + w.shape[2:], lambda i: (l, j, 0, 0))
    args, specs = [x], [row(d)]
    if mix is not None:
        a, b, wo = mix
        args += [a, b, wo]
        specs += [row(a.shape[1]), row(b.shape[1]), _full(wo.shape)]
    args += [g.reshape(1, d), wg, wu, wd]
    specs += [_full((1, d)), pick(wg), pick(wu), pick(wd)]
    if final_g is not None:
        args.append(final_g.reshape(1, d))
        specs.append(_full((1, d)))
    est = 2 * 3 * d * ff * 2 + 4 * tm * d * 4 + tm * ff * 2 + 4 * tm * FF_CHUNK * 4 + 2 * tm * d * 4
    return pl.pallas_call(
        functools.partial(_ffn_body, has_mix=mix is not None, has_final=final_g is not None,
                          ff_chunk=FF_CHUNK),
        out_shape=jax.ShapeDtypeStruct((n, d), F32),
        grid=(n // tm,),
        in_specs=specs,
        out_specs=row(d),
        scratch_shapes=[pltpu.VMEM((tm, ff), BF16)],
        compiler_params=pltpu.CompilerParams(dimension_semantics=("parallel",),
                                             vmem_limit_bytes=_vmem_limit(est)),
        name="ffn",
    )(*args)


def _qkv_body(x_ref, g_ref, w_ref, *out_refs, prompt):
    x = x_ref[...]
    d = x.shape[1]
    h = _rms(x, g_ref[...]).astype(BF16)
    q = _dot(h, w_ref[:, 0:d])
    k = _dot(h, w_ref[:, d:2 * d])
    v = _dot(h, w_ref[:, 2 * d:3 * d])
    if not prompt:
        q_ref, k_ref, v_ref = out_refs
        q_ref[...] = q
        k_ref[...] = k
        v_ref[...] = v
        return
    q_ref, k_ref, v_ref, kb_ref, vb_ref, kbar_ref = out_refs
    q_ref[...] = q.astype(BF16)
    k_ref[...] = k
    v_ref[...] = v
    lane = lax.broadcasted_iota(jnp.int32, (1, d), 1)
    scale = jnp.where(lane < d // 2, HALF_A ** -0.5, HEAD_DIM ** -0.5).astype(F32)
    kb_ref[...] = (k * scale).astype(BF16)
    vb_ref[...] = v.astype(BF16)
    tm = x.shape[0]
    for r in range(tm // MOBA_BLOCK):
        blk = k[r * MOBA_BLOCK:(r + 1) * MOBA_BLOCK, d // 2:]
        kbar_ref[0, r:r + 1, :] = jnp.mean(blk, axis=0, keepdims=True)


def _qkv_call(x, g, w_in, prompt):
    n, d = x.shape
    tm = min(n, TOKEN_TILE)
    assert n % tm == 0
    row = lambda w: pl.BlockSpec((tm, w), lambda i: (i, 0))
    if prompt:
        assert tm % MOBA_BLOCK == 0
        nb = tm // MOBA_BLOCK
        out_shape = [jax.ShapeDtypeStruct((n, d), BF16), jax.ShapeDtypeStruct((n, d), F32),
                     jax.ShapeDtypeStruct((n, d), F32), jax.ShapeDtypeStruct((n, d), BF16),
                     jax.ShapeDtypeStruct((n, d), BF16),
                     jax.ShapeDtypeStruct((n // tm, nb, d // 2), F32)]
        out_specs = [row(d)] * 5 + [pl.BlockSpec((1, nb, d // 2), lambda i: (i, 0, 0))]
    else:
        out_shape = [jax.ShapeDtypeStruct((n, d), F32)] * 3
        out_specs = [row(d)] * 3
    est = 2 * d * 3 * d * 2 + 2 * tm * d * 4 * 4 + 3 * tm * d * 4
    return pl.pallas_call(
        functools.partial(_qkv_body, prompt=prompt),
        out_shape=out_shape,
        grid=(n // tm,),
        in_specs=[row(d), _full((1, d)), _full(w_in.shape)],
        out_specs=out_specs,
        compiler_params=pltpu.CompilerParams(dimension_semantics=("parallel",),
                                             vmem_limit_bytes=_vmem_limit(est)),
        name="qkv_prompt" if prompt else "qkv_sample",
    )(x, g.reshape(1, d), w_in)


def _lanes(x, width):
    return x if width == LANES else jnp.concatenate([x] * (width // LANES), axis=1)


def _softmax_tile(s, shift, v_tile, ones_col, m_ref, acc_ref):
    m_old = m_ref[...]
    m_new = jnp.maximum(m_old, jnp.max(s, axis=-1, keepdims=True) + shift)
    p = jnp.exp(s - _lanes(m_new - shift, s.shape[1]))
    alpha = jnp.exp(m_old - m_new)
    va = jnp.concatenate([v_tile, ones_col], axis=1)
    acc_ref[...] = _lanes(alpha, 2 * LANES) * acc_ref[...] + _dot(p.astype(BF16), va)
    m_ref[...] = m_new


def _softmax_result(acc_ref):
    acc = acc_ref[...]
    return acc[:, 0:HEAD_DIM] / acc[:, HEAD_DIM:HEAD_DIM + 1]


def _chunk_shift(slope, delta):
    return slope * (jnp.zeros((1, 1), jnp.int32) + delta).astype(F32)


def _offset_cols(tk, lo_lane, hi_lane):
    lane = lax.broadcasted_iota(jnp.int32, (tk, LANES), 1)
    c = lax.broadcasted_iota(jnp.int32, (tk, LANES), 0)
    c_lo = jnp.bitwise_and(c, 255)
    return jnp.where(lane == lo_lane, c_lo, jnp.where(lane == hi_lane, c - c_lo, 0)).astype(F32)


def _ones_col(rows):
    lane = lax.broadcasted_iota(jnp.int32, (rows, LANES), 1)
    return jnp.where(lane == 0, 1.0, 0.0).astype(BF16)


def _causal_bias(t):
    r = lax.broadcasted_iota(jnp.int32, (t, t), 0)
    c = lax.broadcasted_iota(jnp.int32, (t, t), 1)
    return jnp.where(c <= r, 0.0, NEG).astype(F32)


def _add_diagonal_bias(s, bias):
    t = bias.shape[1]
    width = s.shape[1]
    if width == t:
        return s + bias
    return jnp.concatenate([s[:, 0:width - t], s[:, width - t:] + bias], axis=1)


def _key_chunk(seq):
    return next(tk for tk in (1024, 512, 256) if seq % tk == 0)


def _diff_lambda(lq1_ref, lk1_ref, lq2_ref, lk2_ref, lam_init):
    a = jnp.exp(jnp.sum(lq1_ref[...] * lk1_ref[...], axis=-1, keepdims=True))
    b = jnp.exp(jnp.sum(lq2_ref[...] * lk2_ref[...], axis=-1, keepdims=True))
    return a - b + lam_init


def _diff_body(slope_ref, q_ref, k_ref, v_ref, lq1_ref, lk1_ref, lq2_ref, lk2_ref, subg_ref,
               o_ref, qa_ref, m_ref, acc_ref, *, t, tk, lam_init):
    i = pl.program_id(2)
    hp = qa_ref.shape[0]
    lane = lax.broadcasted_iota(jnp.int32, (t, HEAD_DIM), 1)
    consts = {tk: (_offset_cols(tk, 0, 1).astype(BF16), _ones_col(tk))}
    slopes = []
    for hh in range(hp):
        cs_h = slice(hh * HEAD_DIM, (hh + 1) * HEAD_DIM)
        q = q_ref[:, cs_h]
        zero = jnp.zeros_like(q)
        slopes.append(slope_ref[hh][:, 0:1])
        q_ext = jnp.where(lane < 2, slopes[hh], 0.0).astype(BF16)
        qa_ref[hh, 0:t, 0:HEAD_DIM] = jnp.where(lane < HALF_A, q, zero)
        qa_ref[hh, t:2 * t, 0:HEAD_DIM] = jnp.where(lane >= HALF_A, q, zero)
        qa_ref[hh, 0:t, HEAD_DIM:] = q_ext
        qa_ref[hh, t:2 * t, HEAD_DIM:] = q_ext

    m_ref[...] = jnp.full(m_ref.shape, NEG, F32)
    acc_ref[...] = jnp.zeros(acc_ref.shape, F32)
    q0 = i * t

    tri = _causal_bias(t)

    def chunk(cs, width, diagonal):
        kext, ones_col = consts.get(width) or (_offset_cols(width, 0, 1).astype(BF16), _ones_col(width))
        for hh in range(hp):
            cs_h = slice(hh * HEAD_DIM, (hh + 1) * HEAD_DIM)
            ka = jnp.concatenate([k_ref[pl.ds(cs, width), cs_h], kext], axis=1)
            s = _dot_nt(qa_ref[hh], ka)
            if diagonal:
                s = _add_diagonal_bias(s, jnp.concatenate([tri, tri], axis=0))
            _softmax_tile(s, _chunk_shift(slopes[hh], cs - q0), v_ref[pl.ds(cs, width), cs_h],
                          ones_col, m_ref.at[hh], acc_ref.at[hh])

    def body(j, carry):
        chunk(pl.multiple_of(j * tk, tk), tk, False)
        return carry

    n_full = q0 // tk
    lax.fori_loop(0, n_full, body, 0)
    rest = i - n_full * (tk // t)
    for r in range(tk // t):
        @pl.when(rest == r)
        def _():
            chunk(pl.multiple_of(n_full * tk, tk), (r + 1) * t, True)

    lam = _diff_lambda(lq1_ref, lk1_ref, lq2_ref, lk2_ref, lam_init)
    for hh in range(hp):
        o = _softmax_result(acc_ref.at[hh])
        oa = o[0:t] - lam * o[t:2 * t]
        o_ref[:, hh * HEAD_DIM:(hh + 1) * HEAD_DIM] = (
            _rms(oa, subg_ref[...]) * (1.0 - lam_init)).astype(BF16)


def _hosted_call(body, *, name, grid, in_specs, out_shape, out_spec, scratch, args, est, sample):
    if sample is None:
        return pl.pallas_call(
            body, out_shape=out_shape, grid=grid, in_specs=in_specs, out_specs=out_spec, scratch_shapes=scratch,
            compiler_params=pltpu.CompilerParams(dimension_semantics=("parallel",) * len(grid),
                                                 vmem_limit_bytes=_vmem_limit(est)),
            name=name)(*args)

    def step_index(*idx):
        g = idx[0]
        for size, i in zip(grid[1:], idx[1:]):
            g = g * size + i
        return g

    st = sample(lambda n_chunks, *idx: (step_index(*idx) // n_chunks, step_index(*idx) % n_chunks))
    n_chunks = st["n_chunks"]
    assert math.prod(grid) == st["prefetch"].shape[0] * n_chunks
    n_in, n_s_in = len(in_specs), len(st["in_specs"])

    def hosted(pt_ref, *refs):
        del pt_ref
        a_in, s_in = refs[:n_in], refs[n_in:n_in + n_s_in]
        rest = refs[n_in + n_s_in:]
        a_out, s_out = rest[:1], rest[1:1 + N_SAMPLE_OUT]
        a_scr, s_scr = rest[1 + N_SAMPLE_OUT:len(rest) - N_SAMPLE_SCRATCH], rest[len(rest) - N_SAMPLE_SCRATCH:]
        g = step_index(*(pl.program_id(ax) for ax in range(len(grid))))
        _sample_stream(lax.rem(g, n_chunks), n_chunks, s_in, s_out, s_scr, **st["kwargs"])
        body(*a_in, *a_out, *a_scr)

    return pl.pallas_call(
        hosted,
        out_shape=[out_shape] + st["out_shape"],
        grid_spec=pltpu.PrefetchScalarGridSpec(
            num_scalar_prefetch=1, grid=grid,
            in_specs=list(in_specs) + st["in_specs"],
            out_specs=[out_spec] + st["out_specs"],
            scratch_shapes=list(scratch) + st["scratch"]),
        compiler_params=pltpu.CompilerParams(dimension_semantics=("arbitrary",) * len(grid),
                                             vmem_limit_bytes=_vmem_limit(est + st["vmem"])),
        name=name + "_with_sample_stream")(st["prefetch"], *args, *st["args"])


def _diff_call(q, kb, vb, slopes, lam_vecs, subg, batch, seq, lam_init, sample=None):
    n, d = q.shape
    nh = d // 2 // HEAD_DIM
    t = ATT_TILE
    tk = _key_chunk(seq)
    assert seq % t == 0 and tk % t == 0
    nq = seq // t
    hp = ATT_HEADS_PER_STEP
    assert nh % hp == 0
    vec = lambda w: _full((1, w))
    kv_bufs = {} if sample is None else dict(pipeline_mode=pl.Buffered(1))
    est = hp * ((2 if sample is None else 1) * 2 * seq * HEAD_DIM * 2 + 2 * t * 2 * HEAD_DIM * 2
                + 2 * t * 3 * LANES * 4 + 6 * 2 * t * tk * 4)
    return _hosted_call(
        functools.partial(_diff_body, t=t, tk=tk, lam_init=lam_init),
        name="diff_attn_prompt",
        grid=(batch, nh // hp, nq),
        in_specs=[pl.BlockSpec((hp, 1, LANES), lambda b, h, i, *_: (h, 0, 0)),
                  pl.BlockSpec((t, hp * HEAD_DIM), lambda b, h, i, *_: (b * nq + i, h)),
                  pl.BlockSpec((seq, hp * HEAD_DIM), lambda b, h, i, *_: (b, h), **kv_bufs),
                  pl.BlockSpec((seq, hp * HEAD_DIM), lambda b, h, i, *_: (b, h), **kv_bufs),
                  vec(HALF_A), vec(HALF_A), vec(HALF_A), vec(HALF_A), vec(HEAD_DIM)],
        out_shape=jax.ShapeDtypeStruct((n, d // 2), BF16),
        out_spec=pl.BlockSpec((t, hp * HEAD_DIM), lambda b, h, i, *_: (b * nq + i, h)),
        scratch=[pltpu.VMEM((hp, 2 * t, 2 * HEAD_DIM), BF16), pltpu.VMEM((hp, 2 * t, LANES), F32),
                 pltpu.VMEM((hp, 2 * t, 2 * LANES), F32)],
        args=(slopes, q, kb, vb, *lam_vecs, subg),
        est=est,
        sample=sample)


def _moba_body(slope_ref, q_ref, k_ref, v_ref, kbar_ref, o_ref, qa_ref, m_ref, acc_ref,
               *, t, tk, n_blocks):
    i = pl.program_id(2)
    hp = qa_ref.shape[0]
    lane = lax.broadcasted_iota(jnp.int32, (t, LANES), 1)
    nbp = -(-n_blocks // SUBLANES) * SUBLANES
    blk = lax.broadcasted_iota(jnp.int32, (nbp, t), 0)
    slopes = []
    for hh in range(hp):
        cs_h = slice(hh * HEAD_DIM, (hh + 1) * HEAD_DIM)
        slope = slope_ref[hh][:, 0:1]
        slopes.append(slope)
        q = q_ref[:, cs_h]
        kbar_hi, kbar_lo = _split_bf16(kbar_ref[0, :, cs_h])
        gate = (_dot_nt(kbar_hi, q) + _dot_nt(kbar_lo, q))[0:nbp]
        gate = jnp.where(blk < i, gate, NEG)
        cnt = jnp.zeros((nbp, t), jnp.int32)
        for n2 in range(n_blocks):
            gn = gate[n2:n2 + 1, :]
            beats = (gn > gate) | ((gn == gate) & (blk > n2))
            cnt = cnt + jnp.where(beats & (i > n2), 1, 0)
        keep = ((blk < i) & (cnt < MOBA_TOPK)) | (blk == i)
        pen = jnp.concatenate([jnp.where(keep, 0.0, NEG), jnp.full((LANES - nbp, t), NEG, F32)], axis=0).T
        qa_ref[hh, :, 0:HEAD_DIM] = q
        qa_ref[hh, :, HEAD_DIM:] = jnp.where(lane >= LANES - 2, slope, pen).astype(BF16)

    def key_consts(width):
        lane_k = lax.broadcasted_iota(jnp.int32, (width, LANES), 1)
        row_k = lax.broadcasted_iota(jnp.int32, (width, LANES), 0)
        return _offset_cols(width, LANES - 2, LANES - 1), lane_k - row_k // MOBA_BLOCK, _ones_col(width)

    consts = {tk: key_consts(tk)}

    m_ref[...] = jnp.full(m_ref.shape, NEG, F32)
    acc_ref[...] = jnp.zeros(acc_ref.shape, F32)
    q0 = i * t

    tri = _causal_bias(t)

    def chunk(cs, width, diagonal):
        offs, blk_off, ones_col = consts.get(width) or key_consts(width)
        kext = jnp.where(blk_off == cs // MOBA_BLOCK, 1.0, offs).astype(BF16)
        for hh in range(hp):
            cs_h = slice(hh * HEAD_DIM, (hh + 1) * HEAD_DIM)
            ka = jnp.concatenate([k_ref[pl.ds(cs, width), cs_h], kext], axis=1)
            s = _dot_nt(qa_ref[hh], ka)
            if diagonal:
                s = _add_diagonal_bias(s, tri)
            _softmax_tile(s, _chunk_shift(slopes[hh], cs - q0), v_ref[pl.ds(cs, width), cs_h],
                          ones_col, m_ref.at[hh], acc_ref.at[hh])

    def body(j, carry):
        chunk(pl.multiple_of(j * tk, tk), tk, False)
        return carry

    n_full = q0 // tk
    lax.fori_loop(0, n_full, body, 0)
    rest = i - n_full * (tk // t)
    for r in range(tk // t):
        @pl.when(rest == r)
        def _():
            chunk(pl.multiple_of(n_full * tk, tk), (r + 1) * t, True)
    for hh in range(hp):
        o_ref[:, hh * HEAD_DIM:(hh + 1) * HEAD_DIM] = _softmax_result(acc_ref.at[hh]).astype(BF16)


def _moba_call(q, kb, vb, kbar, slopes, batch, seq, sample=None):
    n, d = q.shape
    nh = d // 2 // HEAD_DIM
    t = MOBA_BLOCK
    tk = _key_chunk(seq)
    assert seq % t == 0
    nq = seq // t
    assert nq <= LANES - 2
    hp = ATT_HEADS_PER_STEP
    assert nh % hp == 0
    ng = nh // hp
    kv_bufs = {} if sample is None else dict(pipeline_mode=pl.Buffered(1))
    est = hp * ((2 if sample is None else 1) * 2 * seq * HEAD_DIM * 2 + 2 * t * 2 * HEAD_DIM * 2
                + t * 3 * LANES * 4 + 8 * t * tk * 4)
    return _hosted_call(
        functools.partial(_moba_body, t=t, tk=tk, n_blocks=nq),
        name="moba_attn_prompt",
        grid=(batch, ng, nq),
        in_specs=[pl.BlockSpec((hp, 1, LANES), lambda b, h, i, *_: (h, 0, 0)),
                  pl.BlockSpec((t, hp * HEAD_DIM), lambda b, h, i, *_: (b * nq + i, ng + h)),
                  pl.BlockSpec((seq, hp * HEAD_DIM), lambda b, h, i, *_: (b, ng + h), **kv_bufs),
                  pl.BlockSpec((seq, hp * HEAD_DIM), lambda b, h, i, *_: (b, ng + h), **kv_bufs),
                  pl.BlockSpec((1, LANES, hp * HEAD_DIM), lambda b, h, i, *_: (b, 0, h))],
        out_shape=jax.ShapeDtypeStruct((n, d // 2), BF16),
        out_spec=pl.BlockSpec((t, hp * HEAD_DIM), lambda b, h, i, *_: (b * nq + i, h)),
        scratch=[pltpu.VMEM((hp, t, 2 * HEAD_DIM), BF16), pltpu.VMEM((hp, t, LANES), F32),
                 pltpu.VMEM((hp, t, 2 * LANES), F32)],
        args=(slopes, q, kb, vb, kbar),
        est=est,
        sample=sample)


def _head_rows(page_ref, h, page, n_heads):
    return page_ref[0, pl.ds(h, page, stride=n_heads), :]


def _sample_a_body(pt_ref, *refs, pps, n_pages, nh, lam_init):
    del pt_ref
    _sample_stream(pl.program_id(1), pl.num_programs(1), refs[:2 * pps + 9], refs[2 * pps + 9:2 * pps + 11],
                   refs[2 * pps + 11:], pps=pps, n_pages=n_pages, nh=nh, lam_init=lam_init)


N_SAMPLE_IN = 9
N_SAMPLE_OUT = 2
N_SAMPLE_SCRATCH = 5


def _sample_stream(c, n_chunks, ins, outs, scratch, *, pps, n_pages, nh, lam_init):
    kp = ins[:pps]
    vp = ins[pps:2 * pps]
    q_ref, kn_ref, vn_ref, slope_ref, lq1_ref, lk1_ref, lq2_ref, lk2_ref, subg_ref = ins[2 * pps:]
    oa_ref, sel_ref = outs
    w_ref, m_ref, l_ref, acc_ref, gate_ref = scratch
    n_heads = 2 * nh
    page = kp[0].shape[1] // n_heads
    past = n_pages * page
    nd = 2 * nh
    qrows = 2 * nd

    @pl.when(c == 0)
    def _():
        r = lax.broadcasted_iota(jnp.int32, (qrows, HEAD_DIM), 0)
        ln = lax.broadcasted_iota(jnp.int32, (qrows, HEAD_DIM), 1)
        w = jnp.zeros((qrows, HEAD_DIM), F32)
        for h in range(n_heads):
            qb = jnp.broadcast_to(q_ref[0, h:h + 1, :], (qrows, HEAD_DIM))
            if h < nh:
                keep = ((r == 2 * h) & (ln < HALF_A)) | ((r == 2 * h + 1) & (ln >= HALF_A))
                w = jnp.where(keep, qb * (HALF_A ** -0.5), w)
            else:
                hi = qb.astype(BF16).astype(F32)
                w = jnp.where(r == nd + 2 * (h - nh), hi, jnp.where(r == nd + 2 * (h - nh) + 1, qb - hi, w))
        w_ref[...] = w.astype(BF16)
        m_ref[...] = jnp.full(m_ref.shape, NEG, F32)
        l_ref[...] = jnp.zeros(l_ref.shape, F32)
        acc_ref[...] = jnp.zeros(acc_ref.shape, F32)
        gate_ref[...] = jnp.zeros(gate_ref.shape, F32)

    pcols = page * n_heads
    width = pps * pcols
    w = w_ref[...]
    s = jnp.concatenate([_dot_nt(w, kp[p][0].astype(BF16)) for p in range(pps)], axis=1)
    col = lax.broadcasted_iota(jnp.int32, (1, width), 1)
    row = lax.broadcasted_iota(jnp.int32, (qrows, 1), 0)
    row_head = jnp.where(row < nd, row // 2, nh + (row - nd) // 2)
    own = jnp.bitwise_and(col, n_heads - 1) == row_head
    kpos = c * (pps * page) + col // n_heads
    dist = (past - kpos).astype(F32)
    slope_rows = jnp.concatenate(
        [jnp.broadcast_to(slope_ref[h][:, 0:1], (2, 1)) for h in range(nh)], axis=0)
    sd = jnp.where(own[0:nd], s[0:nd] - slope_rows * dist, NEG)
    m_old = m_ref[...]
    m_new = jnp.maximum(m_old, jnp.max(sd, axis=-1, keepdims=True))
    pr = jnp.exp(sd - m_new[:, 0:1])
    alpha = jnp.exp(m_old - m_new)
    l_ref[...] = alpha * l_ref[...] + jnp.sum(pr, axis=-1, keepdims=True)
    pv = jnp.zeros((nd, HEAD_DIM), F32)
    for p in range(pps):
        pv = pv + _dot(pr[:, p * pcols:(p + 1) * pcols].astype(BF16), vp[p][0].astype(BF16))
    acc_ref[...] = alpha * acc_ref[...] + pv
    m_ref[...] = m_new
    sg = jnp.where(own[nd:], s[nd:], 0.0)
    bcols = MOBA_BLOCK * n_heads
    glane = lax.broadcasted_iota(jnp.int32, (qrows - nd, LANES), 1)
    gate_add = jnp.zeros((qrows - nd, LANES), F32)
    for bi in range(width // bcols):
        gsum = jnp.sum(sg[:, bi * bcols:(bi + 1) * bcols], axis=-1, keepdims=True)
        gate_add = gate_add + jnp.where(glane == c * (width // bcols) + bi, gsum, 0.0)
    gate_ref[...] += gate_add

    @pl.when(c == n_chunks - 1)
    def _():
        wf = w_ref[...].astype(F32)[0:nd]
        twice = lambda ref: jnp.concatenate(
            [jnp.broadcast_to(ref[0, h:h + 1, :], (2, HEAD_DIM)) for h in range(nh)], axis=0)
        s_new = jnp.sum(wf * twice(kn_ref), axis=-1, keepdims=True)
        m_old = m_ref[...]
        m_new = jnp.maximum(m_old, s_new)
        p_new = jnp.exp(s_new - m_new)
        alpha = jnp.exp(m_old - m_new)
        l = alpha * l_ref[...] + p_new
        o = (alpha * acc_ref[...] + p_new * twice(vn_ref)) / l
        lam = _diff_lambda(lq1_ref, lk1_ref, lq2_ref, lk2_ref, lam_init)
        for h in range(nh):
            oa = o[2 * h:2 * h + 1, :] - lam * o[2 * h + 1:2 * h + 2, :]
            oa_ref[0, :, h * HEAD_DIM:(h + 1) * HEAD_DIM] = _rms(oa, subg_ref[...]) * (1.0 - lam_init)
        n_past = past // MOBA_BLOCK
        g = jnp.concatenate([gate_ref[2 * h:2 * h + 1, :] + gate_ref[2 * h + 1:2 * h + 2, :]
                             for h in range(nh)], axis=0) * (1.0 / MOBA_BLOCK)
        gl = lax.broadcasted_iota(jnp.int32, g.shape, 1)
        g = jnp.where(gl < n_past, g, NEG)
        cnt = jnp.zeros(g.shape, jnp.int32)
        for n2 in range(n_past):
            gn = g[:, n2:n2 + 1]
            cnt = cnt + jnp.where((gn > g) | ((gn == g) & (gl > n2)), 1, 0)
        sel = jnp.zeros(g.shape, jnp.int32)
        for j in range(MOBA_TOPK):
            idx = jnp.sum(jnp.where((cnt == j) & (gl < n_past), gl, 0), axis=-1, keepdims=True)
            sel = jnp.where(gl == j, idx, sel)
        sel_ref[0] = sel


def _sample_setup(page_table, cache_k, cache_v, layer, q, k_new, v_new, slopes, lam_vecs, subg, lam_init, locate):
    db, n_pages = page_table.shape
    n_layers, n_pool, page, n_heads, hd = cache_k.shape
    d = n_heads * hd
    nh = n_heads // 2
    assert (n_pages * page) % MOBA_BLOCK == 0 and MOBA_BLOCK % page == 0
    assert n_pages * page // MOBA_BLOCK <= LANES
    pps = math.gcd(PAGES_PER_STEP, n_pages)
    assert pps % (MOBA_BLOCK // page) == 0
    ck = cache_k.reshape(n_layers * n_pool, page * n_heads, hd)
    cv = cache_v.reshape(n_layers * n_pool, page * n_heads, hd)
    off = layer * n_pool
    n_chunks = n_pages // pps

    def page_spec(p):
        def index(*a):
            seq, chunk = locate(n_chunks, *a[:-1])
            return (a[-1][seq, chunk * pps + p] + off, 0, 0)
        return pl.BlockSpec((1, page * n_heads, hd), index)

    per_seq = lambda shape: pl.BlockSpec(
        shape, lambda *a: (locate(n_chunks, *a[:-1])[0],) + (0,) * (len(shape) - 1))
    fixed = lambda shape: pl.BlockSpec(shape, lambda *a: (0,) * len(shape))
    tok = per_seq((1, n_heads, hd))
    nd = 2 * nh
    to_heads = lambda a: a.reshape(db, n_heads, hd)
    return dict(
        pps=pps,
        n_chunks=n_chunks,
        prefetch=page_table,
        args=[*([ck] * pps), *([cv] * pps), to_heads(q), to_heads(k_new), to_heads(v_new), slopes, *lam_vecs, subg],
        in_specs=[page_spec(p) for p in range(pps)] * 2 + [
            tok, tok, tok, fixed((nh, 1, LANES)), fixed((1, HALF_A)), fixed((1, HALF_A)), fixed((1, HALF_A)),
            fixed((1, HALF_A)), fixed((1, HEAD_DIM))],
        out_shape=[jax.ShapeDtypeStruct((db, 1, d // 2), F32), jax.ShapeDtypeStruct((db, nh, LANES), jnp.int32)],
        out_specs=[per_seq((1, 1, d // 2)), per_seq((1, nh, LANES))],
        scratch=[pltpu.VMEM((2 * nd, HEAD_DIM), BF16), pltpu.VMEM((nd, LANES), F32), pltpu.VMEM((nd, LANES), F32),
                 pltpu.VMEM((nd, HEAD_DIM), F32), pltpu.VMEM((nd, LANES), F32)],
        vmem=2 * 2 * pps * page * d * 4 + 8 * 2 * nd * pps * page * n_heads * 4,
        kwargs=dict(pps=pps, n_pages=n_pages, nh=nh, lam_init=lam_init),
    )


def _sample_a_call(page_table, cache_k, cache_v, layer, q, k_new, v_new, slopes, lam_vecs, subg, lam_init):
    st = _sample_setup(page_table, cache_k, cache_v, layer, q, k_new, v_new, slopes, lam_vecs, subg, lam_init,
                       locate=lambda n_chunks, b, c: (b, c))
    oa, sel = pl.pallas_call(
        functools.partial(_sample_a_body, **st["kwargs"]),
        out_shape=st["out_shape"],
        grid_spec=pltpu.PrefetchScalarGridSpec(
            num_scalar_prefetch=1,
            grid=(page_table.shape[0], st["n_chunks"]),
            in_specs=st["in_specs"],
            out_specs=st["out_specs"],
            scratch_shapes=st["scratch"]),
        compiler_params=pltpu.CompilerParams(dimension_semantics=("parallel", "arbitrary"),
                                             vmem_limit_bytes=_vmem_limit(st["vmem"])),
        name="sample_attn_a",
    )(st["prefetch"], *st["args"])
    return oa, sel


def _sample_b_body(pt_ref, sel_ref, *refs, n_sel, ppb, n_pages, nh):
    del pt_ref
    n_op = n_sel * ppb
    kp = refs[:n_op]
    vp = refs[n_op:2 * n_op]
    q_ref, kn_ref, vn_ref, slope_ref, o_ref = refs[2 * n_op:]
    b = pl.program_id(0)
    h = pl.program_id(1)
    n_heads = 2 * nh
    page = kp[0].shape[1] // n_heads
    past = n_pages * page
    slope = slope_ref[0][:, 0:1]
    scale = HEAD_DIM ** -0.5
    q = jnp.broadcast_to(q_ref[0], (SUBLANES, HEAD_DIM)).astype(BF16)
    lane = lax.broadcasted_iota(jnp.int32, (SUBLANES, page), 1)
    scores = []
    for j in range(n_sel):
        blk = sel_ref[b, h * n_sel + j]
        for r in range(ppb):
            kh = _head_rows(kp[j * ppb + r], nh + h, page, n_heads).astype(BF16)
            s = _dot_nt(q, kh) * scale
            dist = (past - (blk * MOBA_BLOCK + r * page) - lane).astype(F32)
            scores.append(s - slope * dist)
    s_new = jnp.sum(q_ref[0] * kn_ref[0], axis=-1, keepdims=True) * scale
    m = s_new
    if scores:
        s = jnp.concatenate(scores, axis=1)
        m = jnp.maximum(jnp.max(s, axis=-1, keepdims=True)[0:1], s_new)
        pr = jnp.exp(s - m)
        l = jnp.sum(pr, axis=-1, keepdims=True)[0:1]
        acc = jnp.zeros((SUBLANES, HEAD_DIM), F32)
        for j in range(n_op):
            vh = _head_rows(vp[j], nh + h, page, n_heads).astype(BF16)
            acc = acc + _dot(pr[:, j * page:(j + 1) * page].astype(BF16), vh)
        acc = acc[0:1]
    else:
        l = jnp.zeros((1, 1), F32)
        acc = jnp.zeros((1, HEAD_DIM), F32)
    p_new = jnp.exp(s_new - m)
    o_ref[0] = (acc + p_new * vn_ref[0]) / (l + p_new)


def _sample_b_call(page_table, sel, cache_k, cache_v, layer, q, k_new, v_new, slopes):
    db, n_pages = page_table.shape
    n_layers, n_pool, page, n_heads, hd = cache_k.shape
    d = n_heads * hd
    nh = n_heads // 2
    ppb = MOBA_BLOCK // page
    n_past = n_pages * page // MOBA_BLOCK
    n_sel = min(MOBA_TOPK, n_past)
    ck = cache_k.reshape(n_layers * n_pool, page * n_heads, hd)
    cv = cache_v.reshape(n_layers * n_pool, page * n_heads, hd)
    off = layer * n_pool

    def page_spec(j, r):
        return pl.BlockSpec((1, page * n_heads, hd),
                            lambda b, h, pt, sl: (pt[b, sl[b, h * n_sel + j] * ppb + r] + off, 0, 0))

    tok = pl.BlockSpec((1, 1, hd), lambda b, h, pt, sl: (b, 0, nh + h))
    specs = [page_spec(j, r) for j in range(n_sel) for r in range(ppb)]
    in_specs = specs + specs + [tok, tok, tok, pl.BlockSpec((1, 1, LANES), lambda b, h, pt, sl: (h, 0, 0))]
    n_op = n_sel * ppb
    est = 2 * 2 * n_op * page * d * 4 + 64 * 1024
    return pl.pallas_call(
        functools.partial(_sample_b_body, n_sel=n_sel, ppb=ppb, n_pages=n_pages, nh=nh),
        out_shape=jax.ShapeDtypeStruct((db, 1, d // 2), F32),
        grid_spec=pltpu.PrefetchScalarGridSpec(
            num_scalar_prefetch=2,
            grid=(db, nh),
            in_specs=in_specs,
            out_specs=pl.BlockSpec((1, 1, hd), lambda b, h, pt, sl: (b, 0, h))),
        compiler_params=pltpu.CompilerParams(dimension_semantics=("parallel", "parallel"),
                                             vmem_limit_bytes=_vmem_limit(est)),
        name="sample_attn_b",
    )(page_table, sel[:, :, :n_sel].reshape(db, nh * n_sel), *([ck] * n_op), *([cv] * n_op),
      q, k_new, v_new, slopes)


def _rec_in_body(x_ref, g_ref, w_ref, qkvo_ref, gates_ref, xr_ref, gr_ref, *, dc, drg):
    x = x_ref[...]
    h = _rms(x, g_ref[...]).astype(BF16)
    kscale = HEAD_DIM ** -0.5
    qkvo_ref[:, 0:dc] = _dot(h, w_ref[:, 0:dc]).astype(qkvo_ref.dtype)
    qkvo_ref[:, dc:2 * dc] = (_dot(h, w_ref[:, dc:2 * dc]) * kscale).astype(qkvo_ref.dtype)
    qkvo_ref[:, 2 * dc:3 * dc] = _dot(h, w_ref[:, 2 * dc:3 * dc]).astype(qkvo_ref.dtype)
    qkvo_ref[:, 3 * dc:4 * dc] = _dot(h, w_ref[:, 3 * dc:4 * dc]).astype(qkvo_ref.dtype)
    xr_ref[...] = _dot(h, w_ref[:, 4 * dc:4 * dc + drg])
    gr_ref[...] = _dot(h, w_ref[:, 4 * dc + drg:4 * dc + 2 * drg])
    gates_ref[...] = _dot(h, w_ref[:, 4 * dc + 2 * drg:])


def _rec_in_call(x, g, w_packed, dc, drg, qkvo_dtype):
    n, d = x.shape
    tm = min(n, TOKEN_TILE)
    assert n % tm == 0
    row = lambda w: pl.BlockSpec((tm, w), lambda i: (i, 0))
    est = 2 * w_packed.size * 2 + 2 * tm * d * 4 + 2 * tm * (4 * dc + 2 * drg + LANES) * 4
    return pl.pallas_call(
        functools.partial(_rec_in_body, dc=dc, drg=drg),
        out_shape=[jax.ShapeDtypeStruct((n, 4 * dc), qkvo_dtype), jax.ShapeDtypeStruct((n, LANES), F32),
                   jax.ShapeDtypeStruct((n, drg), F32), jax.ShapeDtypeStruct((n, drg), F32)],
        grid=(n // tm,),
        in_specs=[row(d), _full((1, d)), _full(w_packed.shape)],
        out_specs=[row(4 * dc), row(LANES), row(drg), row(drg)],
        compiler_params=pltpu.CompilerParams(dimension_semantics=("parallel",),
                                             vmem_limit_bytes=_vmem_limit(est)),
        name="rec_in",
    )(x, g.reshape(1, d), w_packed)


def _log_sigmoid(x):
    return jnp.minimum(x, 0.0) - jnp.log1p(jnp.exp(-jnp.abs(x)))


def _softplus(x):
    return jnp.maximum(x, 0.0) + jnp.log1p(jnp.exp(-jnp.abs(x)))


def _mlstm_body(qkvo_ref, gates_ref, bias_ref, ng_ref, hc_ref, c_out, n_out, m_out,
                c_ref, n_ref, m_ref, *, nh):
    ch = pl.program_id(1)
    nseq, L = qkvo_ref.shape[0], qkvo_ref.shape[1]

    @pl.when(ch == 0)
    def _():
        c_ref[...] = jnp.zeros(c_ref.shape, F32)
        n_ref[...] = jnp.zeros(n_ref.shape, F32)
        m_ref[...] = jnp.zeros(m_ref.shape, F32)

    lane = lax.broadcasted_iota(jnp.int32, (L, LANES), 1)
    tt = lax.broadcasted_iota(jnp.int32, (L, L), 0)
    ss = lax.broadcasted_iota(jnp.int32, (L, L), 1)
    causal = ss <= tt
    tril = jnp.where(causal, 1.0, 0.0).astype(BF16)
    for sq in range(nseq):
        g = gates_ref[sq] + bias_ref[...]
        x = jnp.where(lane < nh, g, jnp.where(lane < 2 * nh, _log_sigmoid(g), 0.0))
        x_hi, x_lo = _split_bf16(x)
        x_mid, x_lo = _split_bf16(x - x_hi.astype(F32))
        cum = _dot(tril, x_hi) + _dot(tril, x_mid) + _dot(tril, x_lo)
        colv = jnp.where(lane < nh, x, cum)
        rowv = colv.T
        for h in range(nh):
            _mlstm_head(qkvo_ref.at[sq], hc_ref.at[sq], ng_ref, c_ref.at[sq], n_ref.at[sq], m_ref.at[sq],
                        colv, rowv, causal, h, nh)

    @pl.when(ch == pl.num_programs(1) - 1)
    def _():
        c_out[...] = c_ref[...]
        n_out[...] = n_ref[...]
        m_out[...] = m_ref[...]


def _mlstm_head(qkvo_ref, hc_ref, ng_ref, c_ref, n_ref, m_ref, colv, rowv, causal, h, nh):
    L = qkvo_ref.shape[0]
    dc = nh * HEAD_DIM
    cs = slice(h * HEAD_DIM, (h + 1) * HEAD_DIM)
    q = qkvo_ref[:, cs]
    k = qkvo_ref[:, dc + h * HEAD_DIM:dc + (h + 1) * HEAD_DIM]
    v = qkvo_ref[:, 2 * dc + h * HEAD_DIM:2 * dc + (h + 1) * HEAD_DIM]
    o = qkvo_ref[:, 3 * dc + h * HEAD_DIM:3 * dc + (h + 1) * HEAD_DIM].astype(F32)
    rep = lambda col: jnp.broadcast_to(col, (L, LANES))
    ig_t = rep(colv[:, h:h + 1])
    b_t = rep(colv[:, nh + h:nh + h + 1])
    ig_row = rowv[h:h + 1, :]
    b_row = rowv[nh + h:nh + h + 1, :]
    m_prev = m_ref[h]
    c_prev = c_ref[h]
    n_prev = n_ref[h]

    dmat = jnp.where(causal, b_t - b_row + ig_row, NEG)
    inter = b_t + m_prev
    m_t = jnp.maximum(inter, rep(jnp.max(dmat, axis=-1, keepdims=True)))
    w_inter = jnp.exp(inter - m_t)
    s = _dot_nt(q, k) * jnp.exp(dmat - m_t)
    qf = q.astype(F32)
    num = w_inter * _dot(q, c_prev.astype(BF16)) + _dot(s.astype(BF16), v)
    den = (w_inter * rep(jnp.sum(qf * n_prev, axis=-1, keepdims=True))
           + rep(jnp.sum(s, axis=-1, keepdims=True)))
    hh = num / jnp.maximum(jnp.abs(den), jnp.exp(-m_t))
    m_new = m_t[L - 1:L, :]
    b_last = b_t[L - 1:L, :]
    w_old = jnp.exp(b_last + m_prev - m_new)
    w_new = jnp.exp(b_last - b_t + ig_t - m_new)
    kw = k.astype(F32) * w_new
    c_ref[h] = w_old * c_prev + _dot_tn(kw.astype(BF16), v)
    n_ref[h] = w_old * n_prev + jnp.sum(kw, axis=0, keepdims=True)
    m_ref[h] = m_new
    hc_ref[:, cs] = (_rms(hh, ng_ref[:, cs]) * jax.nn.sigmoid(o)).astype(hc_ref.dtype)


def _mlstm_call(qkvo, gates, bias, norm_g, batch, seq):
    n = qkvo.shape[0]
    dc = qkvo.shape[1] // 4
    nh = dc // HEAD_DIM
    L = MLSTM_CHUNK if seq % MLSTM_CHUNK == 0 else seq
    assert L % SUBLANES == 0 and L == LANES, "prompt mLSTM kernel needs 128-token chunks"
    nc = seq // L
    ns = math.gcd(MLSTM_SEQS_PER_STEP, batch)
    est = ns * (2 * L * 4 * dc * 2 + 2 * L * LANES * 4 + 2 * L * dc * 2 + 3 * nh * HEAD_DIM * HEAD_DIM * 4
                + 16 * L * L * 4)
    hc, c1, n1, m1 = pl.pallas_call(
        functools.partial(_mlstm_body, nh=nh),
        out_shape=[jax.ShapeDtypeStruct((batch, seq, dc), BF16),
                   jax.ShapeDtypeStruct((batch, nh, HEAD_DIM, HEAD_DIM), F32),
                   jax.ShapeDtypeStruct((batch, nh, 1, HEAD_DIM), F32),
                   jax.ShapeDtypeStruct((batch, nh, 1, LANES), F32)],
        grid=(batch // ns, nc),
        in_specs=[pl.BlockSpec((ns, L, 4 * dc), lambda b, c: (b, c, 0)),
                  pl.BlockSpec((ns, L, LANES), lambda b, c: (b, c, 0)),
                  _full((1, LANES)), _full((1, dc))],
        out_specs=[pl.BlockSpec((ns, L, dc), lambda b, c: (b, c, 0)),
                   pl.BlockSpec((ns, nh, HEAD_DIM, HEAD_DIM), lambda b, c: (b, 0, 0, 0)),
                   pl.BlockSpec((ns, nh, 1, HEAD_DIM), lambda b, c: (b, 0, 0, 0)),
                   pl.BlockSpec((ns, nh, 1, LANES), lambda b, c: (b, 0, 0, 0))],
        scratch_shapes=[pltpu.VMEM((ns, nh, HEAD_DIM, HEAD_DIM), F32), pltpu.VMEM((ns, nh, 1, HEAD_DIM), F32),
                        pltpu.VMEM((ns, nh, 1, LANES), F32)],
        compiler_params=pltpu.CompilerParams(dimension_semantics=("parallel", "arbitrary"),
                                             vmem_limit_bytes=_vmem_limit(est)),
        name="mlstm_prompt",
    )(qkvo.reshape(batch, seq, 4 * dc), gates.reshape(batch, seq, LANES), bias, norm_g)
    return hc.reshape(n, dc), c1, n1, m1


def _rg_gates(xc, wa_ref, ba_ref, wx_ref, bx_ref, lam_ref):
    xb = xc.astype(BF16)
    r = jax.nn.sigmoid(_dot(xb, wa_ref[...]) + ba_ref[...])
    i = jax.nn.sigmoid(_dot(xb, wx_ref[...]) + bx_ref[...])
    log_a = -RG_C * r * _softplus(-lam_ref[...])
    a = jnp.exp(log_a)
    u = jnp.sqrt(-jnp.tanh(log_a) * (a * a + 1.0)) * (i * xc)
    return a, u


def _rglru_body(xr_ref, gr_ref, cw_ref, cb_ref, wa_ref, ba_ref, wx_ref, bx_ref, lam_ref,
                y_ref, h_out, xbuf_ref, a_ref, u_ref, h_ref, *, width):
    tstep = pl.program_id(1)
    T, C = xr_ref.shape
    pad = SUBLANES

    @pl.when(tstep == 0)
    def _():
        xbuf_ref[0:pad, :] = jnp.zeros((pad, C), F32)
        h_ref[...] = jnp.zeros(h_ref.shape, F32)

    xbuf_ref[pad:pad + T, :] = xr_ref[...]
    xc = cb_ref[...]
    for j in range(width):
        xc = xc + xbuf_ref[pl.ds(pad - (width - 1) + j, T), :] * cw_ref[j:j + 1, :]
    xbuf_ref[0:pad, :] = xbuf_ref[T:T + pad, :]

    a, u = _rg_gates(xc, wa_ref, ba_ref, wx_ref, bx_ref, lam_ref)
    a_ref[...] = a
    u_ref[...] = u
    sub = lax.broadcasted_iota(jnp.int32, (SUBLANES, C), 0)

    def group(gi, h):
        start = pl.multiple_of(gi * SUBLANES, SUBLANES)
        ag = a_ref[pl.ds(start, SUBLANES), :]
        ug = u_ref[pl.ds(start, SUBLANES), :]
        for sh in (1, 2, 4):
            ap = jnp.where(sub >= sh, pltpu.roll(ag, sh, 0), 1.0)
            up = jnp.where(sub >= sh, pltpu.roll(ug, sh, 0), 0.0)
            ug = ag * up + ug
            ag = ag * ap
        hg = ag * h + ug
        u_ref[pl.ds(start, SUBLANES), :] = hg
        return hg[SUBLANES - 1:SUBLANES, :]

    h_last = lax.fori_loop(0, T // SUBLANES, group, h_ref[...])
    h_ref[...] = h_last
    y_ref[...] = (u_ref[...] * jax.nn.gelu(gr_ref[...], approximate=True)).astype(y_ref.dtype)

    @pl.when(tstep == pl.num_programs(1) - 1)
    def _():
        h_out[0] = h_last


def _rglru_call(xr, gr, conv_w, conv_b, wa, ba, wx, bx, lam, batch, seq):
    n, c = xr.shape
    width = conv_w.shape[0]
    t = min(RG_TILE, seq)
    assert seq % t == 0 and t % SUBLANES == 0 and width - 1 <= SUBLANES
    nt = seq // t
    row = pl.BlockSpec((t, c), lambda b, s: (b * nt + s, 0))
    est = 6 * t * c * 4 + 2 * c * c * 2 * 2 + 8 * t * c * 4
    return pl.pallas_call(
        functools.partial(_rglru_body, width=width),
        out_shape=[jax.ShapeDtypeStruct((n, c), BF16), jax.ShapeDtypeStruct((batch, 1, c), F32)],
        grid=(batch, nt),
        in_specs=[row, row, _full(conv_w.shape), _full((1, c)), _full(wa.shape), _full((1, c)),
                  _full(wx.shape), _full((1, c)), _full((1, c))],
        out_specs=[row, pl.BlockSpec((1, 1, c), lambda b, s: (b, 0, 0))],
        scratch_shapes=[pltpu.VMEM((t + SUBLANES, c), F32), pltpu.VMEM((t, c), F32),
                        pltpu.VMEM((t, c), F32), pltpu.VMEM((1, c), F32)],
        compiler_params=pltpu.CompilerParams(dimension_semantics=("parallel", "arbitrary"),
                                             vmem_limit_bytes=_vmem_limit(est)),
        name="rglru_prompt",
    )(xr, gr, conv_w, conv_b, wa, ba, wx, bx, lam)


def _to_column(row_vec):
    n = row_vec.shape[1]
    r = lax.broadcasted_iota(jnp.int32, (n, n), 0)
    c = lax.broadcasted_iota(jnp.int32, (n, n), 1)
    return jnp.sum(jnp.where(r == c, jnp.broadcast_to(row_vec, (n, n)), 0.0), axis=-1, keepdims=True)


def _rec_sample_body(qkvo_ref, gates_ref, xr_ref, gr_ref, c0_ref, n0_ref, m0_ref, h0_ref, conv0_ref,
                     bias_ref, ng_ref, cw_ref, cb_ref, wa_ref, ba_ref, wx_ref, bx_ref, lam_ref,
                     hc_ref, y_ref, c_out, n_out, m_out, h_out, conv_out, *, nh, width):
    dc = nh * HEAD_DIM
    g = gates_ref[0] + bias_ref[...]
    m0 = m0_ref[0]
    for h in range(nh):
        cs = slice(h * HEAD_DIM, (h + 1) * HEAD_DIM)
        q = qkvo_ref[0][:, cs]
        k = qkvo_ref[0][:, dc + h * HEAD_DIM:dc + (h + 1) * HEAD_DIM]
        v = qkvo_ref[0][:, 2 * dc + h * HEAD_DIM:2 * dc + (h + 1) * HEAD_DIM]
        o = qkvo_ref[0][:, 3 * dc + h * HEAD_DIM:3 * dc + (h + 1) * HEAD_DIM]
        ig = g[:, h:h + 1]
        lf = _log_sigmoid(g[:, nh + h:nh + h + 1])
        m_prev = m0[:, h:h + 1]
        c_prev = c0_ref[0, h]
        n_prev = n0_ref[0, h:h + 1, :]
        inter = lf + m_prev
        m_t = jnp.maximum(inter, ig)
        w_inter = jnp.exp(inter - m_t)
        s = jnp.sum(q * k, axis=-1, keepdims=True) * jnp.exp(ig - m_t)
        q_col = _to_column(q)
        qc = jnp.sum(q_col * c_prev, axis=0, keepdims=True)
        num = w_inter * qc + s * v
        den = w_inter * jnp.sum(q * n_prev, axis=-1, keepdims=True) + s
        hh = num / jnp.maximum(jnp.abs(den), jnp.exp(-m_t))
        w_old = jnp.exp(lf + m_prev - m_t)
        w_new = jnp.exp(ig - m_t)
        c_out[0, h] = w_old * c_prev + (_to_column(k) * w_new) * v
        n_out[0, h:h + 1, :] = w_old * n_prev + w_new * k
        m_out[0, :, h:h + 1] = m_t
        hc_ref[0, :, cs] = (_rms(hh, ng_ref[:, cs]) * jax.nn.sigmoid(o)).astype(hc_ref.dtype)

    x = xr_ref[0]
    xc = cb_ref[...]
    for j in range(width - 1):
        xc = xc + conv0_ref[0, j:j + 1, :] * cw_ref[j:j + 1, :]
    xc = xc + x * cw_ref[width - 1:width, :]
    a, u = _rg_gates(jnp.broadcast_to(xc, (SUBLANES, xc.shape[1])), wa_ref, ba_ref, wx_ref, bx_ref, lam_ref)
    h_new = a[0:1] * h0_ref[0] + u[0:1]
    h_out[0] = h_new
    y_ref[0] = (h_new * jax.nn.gelu(gr_ref[0], approximate=True)).astype(y_ref.dtype)
    for j in range(width - 2):
        conv_out[0, j:j + 1, :] = conv0_ref[0, j + 1:j + 2, :]
    conv_out[0, width - 2:width - 1, :] = x


def _rec_sample_call(qkvo, gates, xr, gr, c0, n0, m0, h0, conv0, bias, norm_g, conv_w, conv_b,
                     wa, ba, wx, bx, lam):
    db = qkvo.shape[0]
    dc = qkvo.shape[1] // 4
    nh = dc // HEAD_DIM
    c = xr.shape[1]
    width = conv_w.shape[0]
    tok = lambda w: pl.BlockSpec((1, 1, w), lambda b: (b, 0, 0))
    in_specs = [tok(4 * dc), tok(LANES), tok(c), tok(c),
                pl.BlockSpec((1, nh, HEAD_DIM, HEAD_DIM), lambda b: (b, 0, 0, 0)),
                pl.BlockSpec((1, nh, HEAD_DIM), lambda b: (b, 0, 0)),
                tok(nh), tok(c),
                pl.BlockSpec((1, width - 1, c), lambda b: (b, 0, 0)),
                _full((1, LANES)), _full((1, dc)), _full(conv_w.shape), _full((1, c)),
                _full(wa.shape), _full((1, c)), _full(wx.shape), _full((1, c)), _full((1, c))]
    out_shape = [jax.ShapeDtypeStruct((db, 1, dc), BF16), jax.ShapeDtypeStruct((db, 1, c), BF16),
                 jax.ShapeDtypeStruct((db, nh, HEAD_DIM, HEAD_DIM), F32),
                 jax.ShapeDtypeStruct((db, nh, HEAD_DIM), F32),
                 jax.ShapeDtypeStruct((db, 1, nh), F32),
                 jax.ShapeDtypeStruct((db, 1, c), F32),
                 jax.ShapeDtypeStruct((db, width - 1, c), F32)]
    out_specs = [tok(dc), tok(c),
                 pl.BlockSpec((1, nh, HEAD_DIM, HEAD_DIM), lambda b: (b, 0, 0, 0)),
                 pl.BlockSpec((1, nh, HEAD_DIM), lambda b: (b, 0, 0)),
                 tok(nh), tok(c),
                 pl.BlockSpec((1, width - 1, c), lambda b: (b, 0, 0))]
    est = 4 * nh * HEAD_DIM * HEAD_DIM * 4 + 2 * c * c * 2 * 2 + 1024 * 1024
    return pl.pallas_call(
        functools.partial(_rec_sample_body, nh=nh, width=width),
        out_shape=out_shape,
        grid=(db,),
        in_specs=in_specs,
        out_specs=out_specs,
        compiler_params=pltpu.CompilerParams(dimension_semantics=("parallel",),
                                             vmem_limit_bytes=_vmem_limit(est)),
        name="rec_sample",
    )(qkvo.reshape(db, 1, 4 * dc), gates.reshape(db, 1, LANES), xr.reshape(db, 1, c), gr.reshape(db, 1, c),
      c0, n0, m0.reshape(db, 1, nh), h0.reshape(db, 1, c), conv0, bias, norm_g, conv_w, conv_b,
      wa, ba, wx, bx, lam)


def _block_diag(w):
    n, k, j = w.shape
    eye = jnp.eye(n, dtype=w.dtype)
    return (eye[:, None, :, None] * w[:, :, None, :]).reshape(n * k, n * j)


def _lane_rows(vals):
    return jnp.broadcast_to(vals.astype(F32)[:, None, None], (vals.shape[0], 1, LANES))


def kernel(x_prompt, x_sample, cache_k, cache_v, state_mlstm_c, state_mlstm_n, state_mlstm_m, state_rglru_h, state_rglru_conv, page_table, norm_g, ffn_w_gate, ffn_w_up, ffn_w_down, att_w_in, att_w_out, diff_lambda_q1, diff_lambda_k1, diff_lambda_q2, diff_lambda_k2, diff_subln_g, rec_w_in, rec_w_out, mlstm_b_i, mlstm_b_f, mlstm_norm_g, rg_conv_w, rg_conv_b, rg_w_a, rg_b_a, rg_w_x, rg_b_x, rg_lambda, final_norm_g):
    batch, seq, d = x_prompt.shape
    db, dseq, _ = x_sample.shape
    assert dseq == 1, "the sample group holds one new token per sequence"
    depth = norm_g.shape[0]
    n_heads = cache_k.shape[3]
    dc = state_mlstm_c.shape[2] * HEAD_DIM
    nhc = dc // HEAD_DIM
    drg = state_rglru_h.shape[2]
    width = state_rglru_conv.shape[2] + 1

    xp = x_prompt.reshape(batch * seq, d)
    xs = x_sample.reshape(db, d)

    hidx = jnp.arange(n_heads, dtype=F32)
    slopes = 2.0 ** (-8.0 * (hidx + 1.0) / n_heads)
    slopes_a, slopes_b = _lane_rows(slopes[0::2]), _lane_rows(slopes[1::2])

    k_p, v_p, k_s, v_s = [], [], [], []
    rec_p, rec_s = [], []
    mix_p = mix_s = None
    wg, wu, wd = (w.astype(BF16) for w in (ffn_w_gate, ffn_w_up, ffn_w_down))
    for l in range(depth):
        j = l // 2
        xp = _ffn_call(xp, norm_g[l, 0], wg, wu, wd, l, 0)
        xs = _ffn_call(xs, norm_g[l, 0], wg, wu, wd, l, 0)
        if l % 2 == 0:
            lam_init = 0.8 - 0.6 * math.exp(-0.3 * l)
            w_in = att_w_in[j].astype(BF16)
            w_out = att_w_out[j].astype(BF16)
            lam_vecs = [v[j].reshape(1, HALF_A) for v in
                        (diff_lambda_q1, diff_lambda_k1, diff_lambda_q2, diff_lambda_k2)]
            subg = diff_subln_g[j].reshape(1, HEAD_DIM)
            q, k, v, kb, vb, kbar = _qkv_call(xp, norm_g[l, 1], w_in, prompt=True)
            kbar = kbar.reshape(batch, seq // MOBA_BLOCK, d // 2)
            kbar = jnp.pad(kbar, ((0, 0), (0, LANES - seq // MOBA_BLOCK), (0, 0)))
            qs, ks, vs = _qkv_call(xs, norm_g[l, 1], w_in, prompt=False)
            qs3, ks3, vs3 = (a.reshape(db, 1, d) for a in (qs, ks, vs))

            def stream(rows):
                return lambda locate: _sample_setup(
                    page_table[rows], cache_k, cache_v, j, qs3[rows], ks3[rows], vs3[rows], slopes_a,
                    lam_vecs, subg, lam_init, locate)

            steps = batch * (n_heads // 2 // ATT_HEADS_PER_STEP) * (seq // ATT_TILE)
            n_chunks = page_table.shape[1] // math.gcd(PAGES_PER_STEP, page_table.shape[1])
            if db % 2 == 0 and steps == (db // 2) * n_chunks:
                half = db // 2
                oa, oa_s0, sel0 = _diff_call(q, kb, vb, slopes_a, lam_vecs, subg, batch, seq, lam_init,
                                             sample=stream(slice(0, half)))
                ob, oa_s1, sel1 = _moba_call(q, kb, vb, kbar, slopes_b, batch, seq, sample=stream(slice(half, db)))
                oa_s, sel = jnp.concatenate([oa_s0, oa_s1]), jnp.concatenate([sel0, sel1])
            else:
                oa = _diff_call(q, kb, vb, slopes_a, lam_vecs, subg, batch, seq, lam_init)
                ob = _moba_call(q, kb, vb, kbar, slopes_b, batch, seq)
                oa_s, sel = _sample_a_call(page_table, cache_k, cache_v, j, qs3, ks3, vs3, slopes_a,
                                           lam_vecs, subg, lam_init)
            mix_p = (oa, ob, w_out)
            k_p.append(k.reshape(batch, seq, n_heads, HEAD_DIM))
            v_p.append(v.reshape(batch, seq, n_heads, HEAD_DIM))
            ob_s = _sample_b_call(page_table, sel, cache_k, cache_v, j, qs3, ks3, vs3, slopes_b)
            mix_s = (oa_s.reshape(db, d // 2).astype(BF16), ob_s.reshape(db, d // 2).astype(BF16), w_out)
            k_s.append(ks.reshape(db, 1, n_heads, HEAD_DIM))
            v_s.append(vs.reshape(db, 1, n_heads, HEAD_DIM))
        else:
            w = rec_w_in[j]
            n_gate = 2 * nhc
            w_packed = jnp.concatenate(
                [w[:, :4 * dc], w[:, 4 * dc + n_gate:],
                 jnp.pad(w[:, 4 * dc:4 * dc + n_gate], ((0, 0), (0, LANES - n_gate)))], axis=1).astype(BF16)
            w_out = rec_w_out[j].astype(BF16)
            bias = jnp.pad(jnp.concatenate([mlstm_b_i[j], mlstm_b_f[j]]), (0, LANES - n_gate)).reshape(1, LANES)
            ng = mlstm_norm_g[j].reshape(1, dc)
            conv_w, conv_b = rg_conv_w[j], rg_conv_b[j].reshape(1, drg)
            wa, wx = _block_diag(rg_w_a[j]).astype(BF16), _block_diag(rg_w_x[j]).astype(BF16)
            ba, bx, lam = (a[j].reshape(1, drg) for a in (rg_b_a, rg_b_x, rg_lambda))
            qkvo, gates, xr, gr = _rec_in_call(xp, norm_g[l, 1], w_packed, dc, drg, BF16)
            hc, c1, n1, m1 = _mlstm_call(qkvo, gates, bias, ng, batch, seq)
            y, h1 = _rglru_call(xr, gr, conv_w, conv_b, wa, ba, wx, bx, lam, batch, seq)
            mix_p = (hc, y, w_out)
            conv1 = xr.reshape(batch, seq, drg)[:, seq - (width - 1):, :]
            rec_p.append((c1, n1.reshape(batch, nhc, HEAD_DIM), m1[:, :, 0, 0], h1.reshape(batch, drg), conv1))
            qkvo_s, gates_s, xr_s, gr_s = _rec_in_call(xs, norm_g[l, 1], w_packed, dc, drg, F32)
            hc_s, y_s, c1s, n1s, m1s, h1s, conv1s = _rec_sample_call(
                qkvo_s, gates_s, xr_s, gr_s, state_mlstm_c[j], state_mlstm_n[j], state_mlstm_m[j],
                state_rglru_h[j], state_rglru_conv[j], bias, ng, conv_w, conv_b, wa, ba, wx, bx, lam)
            mix_s = (hc_s.reshape(db, dc), y_s.reshape(db, drg), w_out)
            rec_s.append((c1s, n1s, m1s.reshape(db, nhc), h1s.reshape(db, drg), conv1s))
        fg = final_norm_g if l == depth - 1 else None
        xp = _ffn_call(xp, norm_g[l, 2], wg, wu, wd, l, 1, mix=mix_p, final_g=fg)
        xs = _ffn_call(xs, norm_g[l, 2], wg, wu, wd, l, 1, mix=mix_s, final_g=fg)

    stack = lambda items: jnp.stack(items, axis=0)
    return (xp.reshape(batch, seq, d), xs.reshape(db, 1, d),
            stack(k_p), stack(v_p), stack(k_s), stack(v_s),
            *(stack([st[i] for st in rec_p]) for i in range(5)),
            *(stack([st[i] for st in rec_s]) for i in range(5)))
```

```python
import functools
import math

import jax
import jax.numpy as jnp
from jax import lax
from jax.experimental import pallas as pl
from jax.experimental.pallas import tpu as pltpu

F32 = jnp.float32
BF16 = jnp.bfloat16

HEAD_DIM = 128
HALF_A = HEAD_DIM // 2
MOBA_BLOCK = 256
MOBA_TOPK = 3
MLSTM_CHUNK = 128
RG_C = 8.0
RMS_EPS = 1e-6
NEG = -1e30

V7X_VMEM_BYTES = 64 * 1024 * 1024
LANES = 128
SUBLANES = 8

TOKEN_TILE = 512
FF_CHUNK = 256
ATT_TILE = 256
ATT_HEADS_PER_STEP = 4
MLSTM_SEQS_PER_STEP = 1
RG_TILE = 512
PAGES_PER_STEP = 16


def _vmem_limit(est_bytes):
    return int(min(max(2 * est_bytes, 32 * 1024 * 1024), V7X_VMEM_BYTES - 8 * 1024 * 1024))


def _rms(x, g):
    return x * lax.rsqrt(jnp.mean(x * x, axis=-1, keepdims=True) + RMS_EPS) * g


def _dot(a, b):
    return jnp.dot(a, b, preferred_element_type=F32)


def _dot_nt(a, b):
    return lax.dot_general(a, b, (((1,), (1,)), ((), ())), preferred_element_type=F32)


def _dot_tn(a, b):
    return lax.dot_general(a, b, (((0,), (0,)), ((), ())), preferred_element_type=F32)


def _split_bf16(x):
    hi = x.astype(BF16)
    lo = (x - hi.astype(F32)).astype(BF16)
    return hi, lo


def _full(shape):
    nd = len(shape)
    return pl.BlockSpec(shape, lambda *_: (0,) * nd)


def _ffn_body(*refs, has_mix, has_final, ff_chunk):
    refs = list(refs)
    x_ref = refs.pop(0)
    if has_mix:
        a_ref, b_ref, wo_ref = refs[:3]
        refs = refs[3:]
    g_ref, wg_ref, wu_ref, wd_ref = refs[:4]
    refs = refs[4:]
    if has_final:
        fg_ref = refs.pop(0)
    o_ref, act_ref = refs

    x = x_ref[...]
    if has_mix:
        da = a_ref.shape[1]
        x = x + _dot(a_ref[...], wo_ref[0:da, :]) + _dot(b_ref[...], wo_ref[da:, :])
    h = _rms(x, g_ref[...]).astype(BF16)
    ff = wg_ref.shape[1]
    for c in range(ff // ff_chunk):
        sl = slice(c * ff_chunk, (c + 1) * ff_chunk)
        gate = _dot(h, wg_ref[:, sl])
        up = _dot(h, wu_ref[:, sl])
        act_ref[:, sl] = (gate * jax.nn.sigmoid(gate) * up).astype(BF16)
    y = x + 0.5 * _dot(act_ref[...], wd_ref[...])
    if has_final:
        y = _rms(y, fg_ref[...])
    o_ref[...] = y


def _ffn_call(x, g, wg, wu, wd, l, j, mix=None, final_g=None):
    n, d = x.shape
    ff = wg.shape[3]
    tm = min(n, TOKEN_TILE)
    assert n % tm == 0 and ff % FF_CHUNK == 0
    row = lambda w: pl.BlockSpec((tm, w), lambda i: (i, 0))
    pick = lambda w: pl.BlockSpec((None, None) + w.shape[2:], lambda i: (l, j, 0, 0))
    args, specs = [x], [row(d)]
    if mix is not None:
        a, b, wo = mix
        args += [a, b, wo]
        specs += [row(a.shape[1]), row(b.shape[1]), _full(wo.shape)]
    args += [g.reshape(1, d), wg, wu, wd]
    specs += [_full((1, d)), pick(wg), pick(wu), pick(wd)]
    if final_g is not None:
        args.append(final_g.reshape(1, d))
        specs.append(_full((1, d)))
    est = 2 * 3 * d * ff * 2 + 4 * tm * d * 4 + tm * ff * 2 + 4 * tm * FF_CHUNK * 4 + 2 * tm * d * 4
    return pl.pallas_call(
        functools.partial(_ffn_body, has_mix=mix is not None, has_final=final_g is not None,
                          ff_chunk=FF_CHUNK),
        out_shape=jax.ShapeDtypeStruct((n, d), F32),
        grid=(n // tm,),
        in_specs=specs,
        out_specs=row(d),
        scratch_shapes=[pltpu.VMEM((tm, ff), BF16)],
        compiler_params=pltpu.CompilerParams(dimension_semantics=("parallel",),
                                             vmem_limit_bytes=_vmem_limit(est)),
        name="ffn",
    )(*args)


def _qkv_body(x_ref, g_ref, w_ref, *out_refs, prompt):
    x = x_ref[...]
    d = x.shape[1]
    h = _rms(x, g_ref[...]).astype(BF16)
    q = _dot(h, w_ref[:, 0:d])
    k = _dot(h, w_ref[:, d:2 * d])
    v = _dot(h, w_ref[:, 2 * d:3 * d])
    if not prompt:
        q_ref, k_ref, v_ref = out_refs
        q_ref[...] = q
        k_ref[...] = k
        v_ref[...] = v
        return
    q_ref, k_ref, v_ref, kb_ref, vb_ref, kbar_ref = out_refs
    q_ref[...] = q.astype(BF16)
    k_ref[...] = k
    v_ref[...] = v
    lane = lax.broadcasted_iota(jnp.int32, (1, d), 1)
    scale = jnp.where(lane < d // 2, HALF_A ** -0.5, HEAD_DIM ** -0.5).astype(F32)
    kb_ref[...] = (k * scale).astype(BF16)
    vb_ref[...] = v.astype(BF16)
    tm = x.shape[0]
    for r in range(tm // MOBA_BLOCK):
        blk = k[r * MOBA_BLOCK:(r + 1) * MOBA_BLOCK, d // 2:]
        kbar_ref[0, r:r + 1, :] = jnp.mean(blk, axis=0, keepdims=True)


def _qkv_call(x, g, w_in, prompt):
    n, d = x.shape
    tm = min(n, TOKEN_TILE)
    assert n % tm == 0
    row = lambda w: pl.BlockSpec((tm, w), lambda i: (i, 0))
    if prompt:
        assert tm % MOBA_BLOCK == 0
        nb = tm // MOBA_BLOCK
        out_shape = [jax.ShapeDtypeStruct((n, d), BF16), jax.ShapeDtypeStruct((n, d), F32),
                     jax.ShapeDtypeStruct((n, d), F32), jax.ShapeDtypeStruct((n, d), BF16),
                     jax.ShapeDtypeStruct((n, d), BF16),
                     jax.ShapeDtypeStruct((n // tm, nb, d // 2), F32)]
        out_specs = [row(d)] * 5 + [pl.BlockSpec((1, nb, d // 2), lambda i: (i, 0, 0))]
    else:
        out_shape = [jax.ShapeDtypeStruct((n, d), F32)] * 3
        out_specs = [row(d)] * 3
    est = 2 * d * 3 * d * 2 + 2 * tm * d * 4 * 4 + 3 * tm * d * 4
    return pl.pallas_call(
        functools.partial(_qkv_body, prompt=prompt),
        out_shape=out_shape,
        grid=(n // tm,),
        in_specs=[row(d), _full((1, d)), _full(w_in.shape)],
        out_specs=out_specs,
        compiler_params=pltpu.CompilerParams(dimension_semantics=("parallel",),
                                             vmem_limit_bytes=_vmem_limit(est)),
        name="qkv_prompt" if prompt else "qkv_sample",
    )(x, g.reshape(1, d), w_in)


def _lanes(x, width):
    return x if width == LANES else jnp.concatenate([x] * (width // LANES), axis=1)


def _softmax_tile(s, shift, v_tile, ones_col, m_ref, acc_ref):
    m_old = m_ref[...]
    m_new = jnp.maximum(m_old, jnp.max(s, axis=-1, keepdims=True) + shift)
    p = jnp.exp(s - _lanes(m_new - shift, s.shape[1]))
    alpha = jnp.exp(m_old - m_new)
    va = jnp.concatenate([v_tile, ones_col], axis=1)
    acc_ref[...] = _lanes(alpha, 2 * LANES) * acc_ref[...] + _dot(p.astype(BF16), va)
    m_ref[...] = m_new


def _softmax_result(acc_ref):
    acc = acc_ref[...]
    return acc[:, 0:HEAD_DIM] / acc[:, HEAD_DIM:HEAD_DIM + 1]


def _chunk_shift(slope, delta):
    return slope * (jnp.zeros((1, 1), jnp.int32) + delta).astype(F32)


def _offset_cols(tk, lo_lane, hi_lane):
    lane = lax.broadcasted_iota(jnp.int32, (tk, LANES), 1)
    c = lax.broadcasted_iota(jnp.int32, (tk, LANES), 0)
    c_lo = jnp.bitwise_and(c, 255)
    return jnp.where(lane == lo_lane, c_lo, jnp.where(lane == hi_lane, c - c_lo, 0)).astype(F32)


def _ones_col(rows):
    lane = lax.broadcasted_iota(jnp.int32, (rows, LANES), 1)
    return jnp.where(lane == 0, 1.0, 0.0).astype(BF16)


def _causal_bias(t):
    r = lax.broadcasted_iota(jnp.int32, (t, t), 0)
    c = lax.broadcasted_iota(jnp.int32, (t, t), 1)
    return jnp.where(c <= r, 0.0, NEG).astype(F32)


def _add_diagonal_bias(s, bias):
    t = bias.shape[1]
    width = s.shape[1]
    if width == t:
        return s + bias
    return jnp.concatenate([s[:, 0:width - t], s[:, width - t:] + bias], axis=1)


def _key_chunk(seq):
    return next(tk for tk in (1024, 512, 256) if seq % tk == 0)


def _diff_lambda(lq1_ref, lk1_ref, lq2_ref, lk2_ref, lam_init):
    a = jnp.exp(jnp.sum(lq1_ref[...] * lk1_ref[...], axis=-1, keepdims=True))
    b = jnp.exp(jnp.sum(lq2_ref[...] * lk2_ref[...], axis=-1, keepdims=True))
    return a - b + lam_init


def _diff_body(slope_ref, q_ref, k_ref, v_ref, lq1_ref, lk1_ref, lq2_ref, lk2_ref, subg_ref,
               o_ref, qa_ref, m_ref, acc_ref, *, t, tk, lam_init):
    i = pl.program_id(2)
    hp = qa_ref.shape[0]
    lane = lax.broadcasted_iota(jnp.int32, (t, HEAD_DIM), 1)
    consts = {tk: (_offset_cols(tk, 0, 1).astype(BF16), _ones_col(tk))}
    slopes = []
    for hh in range(hp):
        cs_h = slice(hh * HEAD_DIM, (hh + 1) * HEAD_DIM)
        q = q_ref[:, cs_h]
        zero = jnp.zeros_like(q)
        slopes.append(slope_ref[hh][:, 0:1])
        q_ext = jnp.where(lane < 2, slopes[hh], 0.0).astype(BF16)
        qa_ref[hh, 0:t, 0:HEAD_DIM] = jnp.where(lane < HALF_A, q, zero)
        qa_ref[hh, t:2 * t, 0:HEAD_DIM] = jnp.where(lane >= HALF_A, q, zero)
        qa_ref[hh, 0:t, HEAD_DIM:] = q_ext
        qa_ref[hh, t:2 * t, HEAD_DIM:] = q_ext

    m_ref[...] = jnp.full(m_ref.shape, NEG, F32)
    acc_ref[...] = jnp.zeros(acc_ref.shape, F32)
    q0 = i * t

    tri = _causal_bias(t)

    def chunk(cs, width, diagonal):
        kext, ones_col = consts.get(width) or (_offset_cols(width, 0, 1).astype(BF16), _ones_col(width))
        for hh in range(hp):
            cs_h = slice(hh * HEAD_DIM, (hh + 1) * HEAD_DIM)
            ka = jnp.concatenate([k_ref[pl.ds(cs, width), cs_h], kext], axis=1)
            s = _dot_nt(qa_ref[hh], ka)
            if diagonal:
                s = _add_diagonal_bias(s, jnp.concatenate([tri, tri], axis=0))
            _softmax_tile(s, _chunk_shift(slopes[hh], cs - q0), v_ref[pl.ds(cs, width), cs_h],
                          ones_col, m_ref.at[hh], acc_ref.at[hh])

    def body(j, carry):
        chunk(pl.multiple_of(j * tk, tk), tk, False)
        return carry

    n_full = q0 // tk
    lax.fori_loop(0, n_full, body, 0)
    rest = i - n_full * (tk // t)
    for r in range(tk // t):
        @pl.when(rest == r)
        def _():
            chunk(pl.multiple_of(n_full * tk, tk), (r + 1) * t, True)

    lam = _diff_lambda(lq1_ref, lk1_ref, lq2_ref, lk2_ref, lam_init)
    for hh in range(hp):
        o = _softmax_result(acc_ref.at[hh])
        oa = o[0:t] - lam * o[t:2 * t]
        o_ref[:, hh * HEAD_DIM:(hh + 1) * HEAD_DIM] = (
            _rms(oa, subg_ref[...]) * (1.0 - lam_init)).astype(BF16)


def _hosted_call(body, *, name, grid, in_specs, out_shape, out_spec, scratch, args, est, sample):
    if sample is None:
        return pl.pallas_call(
            body, out_shape=out_shape, grid=grid, in_specs=in_specs, out_specs=out_spec, scratch_shapes=scratch,
            compiler_params=pltpu.CompilerParams(dimension_semantics=("parallel",) * len(grid),
                                                 vmem_limit_bytes=_vmem_limit(est)),
            name=name)(*args)

    def step_index(*idx):
        g = idx[0]
        for size, i in zip(grid[1:], idx[1:]):
            g = g * size + i
        return g

    st = sample(lambda n_chunks, *idx: (step_index(*idx) // n_chunks, step_index(*idx) % n_chunks))
    n_chunks = st["n_chunks"]
    assert math.prod(grid) == st["prefetch"].shape[0] * n_chunks
    n_in, n_s_in = len(in_specs), len(st["in_specs"])

    def hosted(pt_ref, *refs):
        del pt_ref
        a_in, s_in = refs[:n_in], refs[n_in:n_in + n_s_in]
        rest = refs[n_in + n_s_in:]
        a_out, s_out = rest[:1], rest[1:1 + N_SAMPLE_OUT]
        a_scr, s_scr = rest[1 + N_SAMPLE_OUT:len(rest) - N_SAMPLE_SCRATCH], rest[len(rest) - N_SAMPLE_SCRATCH:]
        g = step_index(*(pl.program_id(ax) for ax in range(len(grid))))
        _sample_stream(lax.rem(g, n_chunks), n_chunks, s_in, s_out, s_scr, **st["kwargs"])
        body(*a_in, *a_out, *a_scr)

    return pl.pallas_call(
        hosted,
        out_shape=[out_shape] + st["out_shape"],
        grid_spec=pltpu.PrefetchScalarGridSpec(
            num_scalar_prefetch=1, grid=grid,
            in_specs=list(in_specs) + st["in_specs"],
            out_specs=[out_spec] + st["out_specs"],
            scratch_shapes=list(scratch) + st["scratch"]),
        compiler_params=pltpu.CompilerParams(dimension_semantics=("arbitrary",) * len(grid),
                                             vmem_limit_bytes=_vmem_limit(est + st["vmem"])),
        name=name + "_with_sample_stream")(st["prefetch"], *args, *st["args"])


def _diff_call(q, kb, vb, slopes, lam_vecs, subg, batch, seq, lam_init, sample=None):
    n, d = q.shape
    nh = d // 2 // HEAD_DIM
    t = ATT_TILE
    tk = _key_chunk(seq)
    assert seq % t == 0 and tk % t == 0
    nq = seq // t
    hp = ATT_HEADS_PER_STEP
    assert nh % hp == 0
    vec = lambda w: _full((1, w))
    kv_bufs = {} if sample is None else dict(pipeline_mode=pl.Buffered(1))
    est = hp * ((2 if sample is None else 1) * 2 * seq * HEAD_DIM * 2 + 2 * t * 2 * HEAD_DIM * 2
                + 2 * t * 3 * LANES * 4 + 6 * 2 * t * tk * 4)
    return _hosted_call(
        functools.partial(_diff_body, t=t, tk=tk, lam_init=lam_init),
        name="diff_attn_prompt",
        grid=(batch, nh // hp, nq),
        in_specs=[pl.BlockSpec((hp, 1, LANES), lambda b, h, i, *_: (h, 0, 0)),
                  pl.BlockSpec((t, hp * HEAD_DIM), lambda b, h, i, *_: (b * nq + i, h)),
                  pl.BlockSpec((seq, hp * HEAD_DIM), lambda b, h, i, *_: (b, h), **kv_bufs),
                  pl.BlockSpec((seq, hp * HEAD_DIM), lambda b, h, i, *_: (b, h), **kv_bufs),
                  vec(HALF_A), vec(HALF_A), vec(HALF_A), vec(HALF_A), vec(HEAD_DIM)],
        out_shape=jax.ShapeDtypeStruct((n, d // 2), BF16),
        out_spec=pl.BlockSpec((t, hp * HEAD_DIM), lambda b, h, i, *_: (b * nq + i, h)),
        scratch=[pltpu.VMEM((hp, 2 * t, 2 * HEAD_DIM), BF16), pltpu.VMEM((hp, 2 * t, LANES), F32),
                 pltpu.VMEM((hp, 2 * t, 2 * LANES), F32)],
        args=(slopes, q, kb, vb, *lam_vecs, subg),
        est=est,
        sample=sample)


def _moba_body(slope_ref, q_ref, k_ref, v_ref, kbar_ref, o_ref, qa_ref, m_ref, acc_ref,
               *, t, tk, n_blocks):
    i = pl.program_id(2)
    hp = qa_ref.shape[0]
    lane = lax.broadcasted_iota(jnp.int32, (t, LANES), 1)
    nbp = -(-n_blocks // SUBLANES) * SUBLANES
    blk = lax.broadcasted_iota(jnp.int32, (nbp, t), 0)
    slopes = []
    for hh in range(hp):
        cs_h = slice(hh * HEAD_DIM, (hh + 1) * HEAD_DIM)
        slope = slope_ref[hh][:, 0:1]
        slopes.append(slope)
        q = q_ref[:, cs_h]
        kbar_hi, kbar_lo = _split_bf16(kbar_ref[0, :, cs_h])
        gate = (_dot_nt(kbar_hi, q) + _dot_nt(kbar_lo, q))[0:nbp]
        gate = jnp.where(blk < i, gate, NEG)
        cnt = jnp.zeros((nbp, t), jnp.int32)
        for n2 in range(n_blocks):
            gn = gate[n2:n2 + 1, :]
            beats = (gn > gate) | ((gn == gate) & (blk > n2))
            cnt = cnt + jnp.where(beats & (i > n2), 1, 0)
        keep = ((blk < i) & (cnt < MOBA_TOPK)) | (blk == i)
        pen = jnp.concatenate([jnp.where(keep, 0.0, NEG), jnp.full((LANES - nbp, t), NEG, F32)], axis=0).T
        qa_ref[hh, :, 0:HEAD_DIM] = q
        qa_ref[hh, :, HEAD_DIM:] = jnp.where(lane >= LANES - 2, slope, pen).astype(BF16)

    def key_consts(width):
        lane_k = lax.broadcasted_iota(jnp.int32, (width, LANES), 1)
        row_k = lax.broadcasted_iota(jnp.int32, (width, LANES), 0)
        return _offset_cols(width, LANES - 2, LANES - 1), lane_k - row_k // MOBA_BLOCK, _ones_col(width)

    consts = {tk: key_consts(tk)}

    m_ref[...] = jnp.full(m_ref.shape, NEG, F32)
    acc_ref[...] = jnp.zeros(acc_ref.shape, F32)
    q0 = i * t

    tri = _causal_bias(t)

    def chunk(cs, width, diagonal):
        offs, blk_off, ones_col = consts.get(width) or key_consts(width)
        kext = jnp.where(blk_off == cs // MOBA_BLOCK, 1.0, offs).astype(BF16)
        for hh in range(hp):
            cs_h = slice(hh * HEAD_DIM, (hh + 1) * HEAD_DIM)
            ka = jnp.concatenate([k_ref[pl.ds(cs, width), cs_h], kext], axis=1)
            s = _dot_nt(qa_ref[hh], ka)
            if diagonal:
                s = _add_diagonal_bias(s, tri)
            _softmax_tile(s, _chunk_shift(slopes[hh], cs - q0), v_ref[pl.ds(cs, width), cs_h],
                          ones_col, m_ref.at[hh], acc_ref.at[hh])

    def body(j, carry):
        chunk(pl.multiple_of(j * tk, tk), tk, False)
        return carry

    n_full = q0 // tk
    lax.fori_loop(0, n_full, body, 0)
    rest = i - n_full * (tk // t)
    for r in range(tk // t):
        @pl.when(rest == r)
        def _():
            chunk(pl.multiple_of(n_full * tk, tk), (r + 1) * t, True)
    for hh in range(hp):
        o_ref[:, hh * HEAD_DIM:(hh + 1) * HEAD_DIM] = _softmax_result(acc_ref.at[hh]).astype(BF16)


def _moba_call(q, kb, vb, kbar, slopes, batch, seq, sample=None):
    n, d = q.shape
    nh = d // 2 // HEAD_DIM
    t = MOBA_BLOCK
    tk = _key_chunk(seq)
    assert seq % t == 0
    nq = seq // t
    assert nq <= LANES - 2
    hp = ATT_HEADS_PER_STEP
    assert nh % hp == 0
    ng = nh // hp
    kv_bufs = {} if sample is None else dict(pipeline_mode=pl.Buffered(1))
    est = hp * ((2 if sample is None else 1) * 2 * seq * HEAD_DIM * 2 + 2 * t * 2 * HEAD_DIM * 2
                + t * 3 * LANES * 4 + 8 * t * tk * 4)
    return _hosted_call(
        functools.partial(_moba_body, t=t, tk=tk, n_blocks=nq),
        name="moba_attn_prompt",
        grid=(batch, ng, nq),
        in_specs=[pl.BlockSpec((hp, 1, LANES), lambda b, h, i, *_: (h, 0, 0)),
                  pl.BlockSpec((t, hp * HEAD_DIM), lambda b, h, i, *_: (b * nq + i, ng + h)),
                  pl.BlockSpec((seq, hp * HEAD_DIM), lambda b, h, i, *_: (b, ng + h), **kv_bufs),
                  pl.BlockSpec((seq, hp * HEAD_DIM), lambda b, h, i, *_: (b, ng + h), **kv_bufs),
                  pl.BlockSpec((1, LANES, hp * HEAD_DIM), lambda b, h, i, *_: (b, 0, h))],
        out_shape=jax.ShapeDtypeStruct((n, d // 2), BF16),
        out_spec=pl.BlockSpec((t, hp * HEAD_DIM), lambda b, h, i, *_: (b * nq + i, h)),
        scratch=[pltpu.VMEM((hp, t, 2 * HEAD_DIM), BF16), pltpu.VMEM((hp, t, LANES), F32),
                 pltpu.VMEM((hp, t, 2 * LANES), F32)],
        args=(slopes, q, kb, vb, kbar),
        est=est,
        sample=sample)


def _sample_a_body(pt_ref, *refs, pps, n_pages, nh, lam_init):
    del pt_ref
    n_in = 2 * pps + N_SAMPLE_IN
    _sample_stream(pl.program_id(1), pl.num_programs(1), refs[:n_in], refs[n_in:n_in + N_SAMPLE_OUT],
                   refs[n_in + N_SAMPLE_OUT:], pps=pps, n_pages=n_pages, nh=nh, lam_init=lam_init)


N_SAMPLE_IN = 10
N_SAMPLE_OUT = 2
N_SAMPLE_SCRATCH = 8


def _sample_stream(c, n_chunks, ins, outs, scratch, *, pps, n_pages, nh, lam_init):
    kp = ins[:pps]
    vp = ins[pps:2 * pps]
    q_ref, kn_ref, vn_ref, slope_ref, slope_b_ref, lq1_ref, lk1_ref, lq2_ref, lk2_ref, subg_ref = ins[2 * pps:]
    oa_ref, ob_ref = outs
    w_ref, m_ref, l_ref, acc_ref, gate_ref, bm_ref, bl_ref, bacc_ref = scratch
    n_heads = 2 * nh
    page = kp[0].shape[1] // n_heads
    past = n_pages * page
    nd = 2 * nh
    qrows = 2 * nd
    scale_b = HEAD_DIM ** -0.5

    @pl.when(c == 0)
    def _():
        r = lax.broadcasted_iota(jnp.int32, (qrows, HEAD_DIM), 0)
        ln = lax.broadcasted_iota(jnp.int32, (qrows, HEAD_DIM), 1)
        w = jnp.zeros((qrows, HEAD_DIM), F32)
        for h in range(n_heads):
            qb = jnp.broadcast_to(q_ref[0, h:h + 1, :], (qrows, HEAD_DIM))
            if h < nh:
                keep = ((r == 2 * h) & (ln < HALF_A)) | ((r == 2 * h + 1) & (ln >= HALF_A))
                w = jnp.where(keep, qb * (HALF_A ** -0.5), w)
            else:
                hi = qb.astype(BF16).astype(F32)
                w = jnp.where(r == nd + (h - nh), hi, jnp.where(r == nd + nh + (h - nh), qb - hi, w))
        w_ref[...] = w.astype(BF16)
        m_ref[...] = jnp.full(m_ref.shape, NEG, F32)
        l_ref[...] = jnp.zeros(l_ref.shape, F32)
        acc_ref[...] = jnp.zeros(acc_ref.shape, F32)
        gate_ref[...] = jnp.zeros(gate_ref.shape, F32)
        bm_ref[...] = jnp.full(bm_ref.shape, NEG, F32)
        bl_ref[...] = jnp.zeros(bl_ref.shape, F32)
        bacc_ref[...] = jnp.zeros(bacc_ref.shape, F32)

    pcols = page * n_heads
    width = pps * pcols
    bcols = MOBA_BLOCK * n_heads
    nbs = width // bcols
    w = w_ref[...]
    s = jnp.concatenate([_dot_nt(w, kp[p][0].astype(BF16)) for p in range(pps)], axis=1)
    col = lax.broadcasted_iota(jnp.int32, (1, width), 1)
    row = lax.broadcasted_iota(jnp.int32, (qrows, 1), 0)
    row_head = jnp.where(row < nd, row // 2, nh + lax.rem(row - nd, nh))
    own = jnp.bitwise_and(col, n_heads - 1) == row_head
    kpos = c * (pps * page) + col // n_heads
    dist = (past - kpos).astype(F32)
    slope_rows = jnp.concatenate(
        [jnp.broadcast_to(slope_ref[h][:, 0:1], (2, 1)) for h in range(nh)], axis=0)
    sd = jnp.where(own[0:nd], s[0:nd] - slope_rows * dist, NEG)
    m_old = m_ref[...]
    m_new = jnp.maximum(m_old, jnp.max(sd, axis=-1, keepdims=True))
    pr = jnp.exp(sd - m_new[:, 0:1])
    alpha = jnp.exp(m_old - m_new)
    l_ref[...] = alpha * l_ref[...] + jnp.sum(pr, axis=-1, keepdims=True)
    m_ref[...] = m_new

    sm = s[nd:]
    full = sm + pltpu.roll(sm, nh, 0)
    own_b = own[nd:]
    slope_b_rows = jnp.concatenate(
        [jnp.broadcast_to(slope_b_ref[h][:, 0:1], (1, 1)) for h in range(nh)] * 2, axis=0)
    sb = jnp.where(own_b, full * scale_b - slope_b_rows * dist, NEG)
    sg = jnp.where(own_b, full, 0.0)
    glane = lax.broadcasted_iota(jnp.int32, (nd, LANES), 1)
    gate_add = jnp.zeros((nd, LANES), F32)
    bm = bm_ref[...]
    bl = bl_ref[...]
    p_blocks = []
    for bi in range(nbs):
        cols = slice(bi * bcols, (bi + 1) * bcols)
        blk = c * nbs + bi
        gate_add = gate_add + jnp.where(glane == blk, jnp.sum(sg[:, cols], axis=-1, keepdims=True), 0.0)
        m_b = jnp.max(sb[:, cols], axis=-1, keepdims=True)
        p_b = jnp.exp(sb[:, cols] - m_b)
        bm = jnp.where(glane == blk, m_b, bm)
        bl = jnp.where(glane == blk, jnp.sum(p_b, axis=-1, keepdims=True), bl)
        p_blocks.append(p_b)
    gate_ref[...] += gate_add
    bm_ref[...] = bm
    bl_ref[...] = bl
    pm = jnp.concatenate(p_blocks, axis=1)

    ppb = MOBA_BLOCK // page
    pv = jnp.zeros((nd, HEAD_DIM), F32)
    for bi in range(nbs):
        pv_b = jnp.zeros((nd, HEAD_DIM), F32)
        for p in range(bi * ppb, (bi + 1) * ppb):
            cols = slice(p * pcols, (p + 1) * pcols)
            both = _dot(jnp.concatenate([pr[:, cols], pm[:, cols]], axis=0).astype(BF16), vp[p][0].astype(BF16))
            pv = pv + both[0:nd]
            pv_b = pv_b + both[nd:]
        bacc_ref[pl.ds(pl.multiple_of((c * nbs + bi) * nd, nd), nd), :] = pv_b
    acc_ref[...] = alpha * acc_ref[...] + pv

    @pl.when(c == n_chunks - 1)
    def _():
        wf = w_ref[...].astype(F32)[0:nd]
        twice = lambda ref: jnp.concatenate(
            [jnp.broadcast_to(ref[0, h:h + 1, :], (2, HEAD_DIM)) for h in range(nh)], axis=0)
        s_new = jnp.sum(wf * twice(kn_ref), axis=-1, keepdims=True)
        m_old = m_ref[...]
        m_new = jnp.maximum(m_old, s_new)
        p_new = jnp.exp(s_new - m_new)
        alpha = jnp.exp(m_old - m_new)
        l = alpha * l_ref[...] + p_new
        o = (alpha * acc_ref[...] + p_new * twice(vn_ref)) / l
        lam = _diff_lambda(lq1_ref, lk1_ref, lq2_ref, lk2_ref, lam_init)
        for h in range(nh):
            oa = o[2 * h:2 * h + 1, :] - lam * o[2 * h + 1:2 * h + 2, :]
            oa_ref[0, :, h * HEAD_DIM:(h + 1) * HEAD_DIM] = _rms(oa, subg_ref[...]) * (1.0 - lam_init)

        n_past = past // MOBA_BLOCK
        g = gate_ref[0:nh, :] * (1.0 / MOBA_BLOCK)
        gl = lax.broadcasted_iota(jnp.int32, g.shape, 1)
        g = jnp.where(gl < n_past, g, NEG)
        cnt = jnp.zeros(g.shape, jnp.int32)
        for n2 in range(n_past):
            gn = g[:, n2:n2 + 1]
            cnt = cnt + jnp.where((gn > g) | ((gn == g) & (gl > n2)), 1, 0)
        keep = (gl < n_past) & (cnt < MOBA_TOPK)
        qb = q_ref[0, nh:n_heads, :]
        sb_new = jnp.sum(qb * kn_ref[0, nh:n_heads, :], axis=-1, keepdims=True) * scale_b
        bm_k = jnp.where(keep, bm_ref[0:nh, :], NEG)
        m_all = jnp.maximum(jnp.max(bm_k, axis=-1, keepdims=True), sb_new)
        wts = jnp.where(keep, jnp.exp(bm_k - m_all), 0.0)
        pb_new = jnp.exp(sb_new - m_all)
        lb = jnp.sum(wts * bl_ref[0:nh, :], axis=-1, keepdims=True) + pb_new
        for h in range(nh):
            a_h = bacc_ref[pl.ds(h, LANES, stride=nd), :]
            w_h = jnp.broadcast_to(wts[h:h + 1, :], (SUBLANES, LANES))
            w_hi, w_lo = _split_bf16(w_h)
            a_hi, a_lo = _split_bf16(a_h)
            ob = (_dot(w_hi, a_hi) + _dot(w_lo, a_hi) + _dot(w_hi, a_lo))[0:1]
            ob = (ob + pb_new[h:h + 1] * vn_ref[0, nh + h:nh + h + 1, :]) / lb[h:h + 1]
            ob_ref[0, :, h * HEAD_DIM:(h + 1) * HEAD_DIM] = ob


def _sample_setup(page_table, cache_k, cache_v, layer, q, k_new, v_new, slopes, slopes_b, lam_vecs, subg, lam_init,
                  locate):
    db, n_pages = page_table.shape
    n_layers, n_pool, page, n_heads, hd = cache_k.shape
    d = n_heads * hd
    nh = n_heads // 2
    assert (n_pages * page) % MOBA_BLOCK == 0 and MOBA_BLOCK % page == 0
    assert n_pages * page // MOBA_BLOCK <= LANES
    pps = math.gcd(PAGES_PER_STEP, n_pages)
    assert pps % (MOBA_BLOCK // page) == 0
    ck = cache_k.reshape(n_layers * n_pool, page * n_heads, hd)
    cv = cache_v.reshape(n_layers * n_pool, page * n_heads, hd)
    off = layer * n_pool
    n_chunks = n_pages // pps

    def page_spec(p):
        def index(*a):
            seq, chunk = locate(n_chunks, *a[:-1])
            return (a[-1][seq, chunk * pps + p] + off, 0, 0)
        return pl.BlockSpec((1, page * n_heads, hd), index)

    per_seq = lambda shape: pl.BlockSpec(
        shape, lambda *a: (locate(n_chunks, *a[:-1])[0],) + (0,) * (len(shape) - 1))
    fixed = lambda shape: pl.BlockSpec(shape, lambda *a: (0,) * len(shape))
    tok = per_seq((1, n_heads, hd))
    nd = 2 * nh
    to_heads = lambda a: a.reshape(db, n_heads, hd)
    stat = pltpu.VMEM((nd, LANES), F32)
    return dict(
        pps=pps,
        n_chunks=n_chunks,
        prefetch=page_table,
        args=[*([ck] * pps), *([cv] * pps), to_heads(q), to_heads(k_new), to_heads(v_new), slopes, slopes_b,
              *lam_vecs, subg],
        in_specs=[page_spec(p) for p in range(pps)] * 2 + [
            tok, tok, tok, fixed((nh, 1, LANES)), fixed((nh, 1, LANES)), fixed((1, HALF_A)), fixed((1, HALF_A)),
            fixed((1, HALF_A)), fixed((1, HALF_A)), fixed((1, HEAD_DIM))],
        out_shape=[jax.ShapeDtypeStruct((db, 1, d // 2), F32), jax.ShapeDtypeStruct((db, 1, d // 2), F32)],
        out_specs=[per_seq((1, 1, d // 2)), per_seq((1, 1, d // 2))],
        scratch=[pltpu.VMEM((2 * nd, HEAD_DIM), BF16), stat, stat, pltpu.VMEM((nd, HEAD_DIM), F32), stat, stat,
                 stat, pltpu.VMEM((LANES * nd, HEAD_DIM), F32)],
        vmem=2 * 2 * pps * page * d * 4 + 10 * 2 * nd * pps * page * n_heads * 4,
        kwargs=dict(pps=pps, n_pages=n_pages, nh=nh, lam_init=lam_init),
    )


def _sample_a_call(page_table, cache_k, cache_v, layer, q, k_new, v_new, slopes, slopes_b, lam_vecs, subg,
                   lam_init):
    st = _sample_setup(page_table, cache_k, cache_v, layer, q, k_new, v_new, slopes, slopes_b, lam_vecs, subg,
                       lam_init, locate=lambda n_chunks, b, c: (b, c))
    return pl.pallas_call(
        functools.partial(_sample_a_body, **st["kwargs"]),
        out_shape=st["out_shape"],
        grid_spec=pltpu.PrefetchScalarGridSpec(
            num_scalar_prefetch=1,
            grid=(page_table.shape[0], st["n_chunks"]),
            in_specs=st["in_specs"],
            out_specs=st["out_specs"],
            scratch_shapes=st["scratch"]),
        compiler_params=pltpu.CompilerParams(dimension_semantics=("parallel", "arbitrary"),
                                             vmem_limit_bytes=_vmem_limit(st["vmem"])),
        name="sample_attn",
    )(st["prefetch"], *st["args"])


def _rec_in_body(x_ref, g_ref, w_ref, qkvo_ref, gates_ref, xr_ref, gr_ref, *, dc, drg):
    x = x_ref[...]
    h = _rms(x, g_ref[...]).astype(BF16)
    kscale = HEAD_DIM ** -0.5
    qkvo_ref[:, 0:dc] = _dot(h, w_ref[:, 0:dc]).astype(qkvo_ref.dtype)
    qkvo_ref[:, dc:2 * dc] = (_dot(h, w_ref[:, dc:2 * dc]) * kscale).astype(qkvo_ref.dtype)
    qkvo_ref[:, 2 * dc:3 * dc] = _dot(h, w_ref[:, 2 * dc:3 * dc]).astype(qkvo_ref.dtype)
    qkvo_ref[:, 3 * dc:4 * dc] = _dot(h, w_ref[:, 3 * dc:4 * dc]).astype(qkvo_ref.dtype)
    xr_ref[...] = _dot(h, w_ref[:, 4 * dc:4 * dc + drg])
    gr_ref[...] = _dot(h, w_ref[:, 4 * dc + drg:4 * dc + 2 * drg])
    gates_ref[...] = _dot(h, w_ref[:, 4 * dc + 2 * drg:])


def _rec_in_call(x, g, w_packed, dc, drg, qkvo_dtype):
    n, d = x.shape
    tm = min(n, TOKEN_TILE)
    assert n % tm == 0
    row = lambda w: pl.BlockSpec((tm, w), lambda i: (i, 0))
    est = 2 * w_packed.size * 2 + 2 * tm * d * 4 + 2 * tm * (4 * dc + 2 * drg + LANES) * 4
    return pl.pallas_call(
        functools.partial(_rec_in_body, dc=dc, drg=drg),
        out_shape=[jax.ShapeDtypeStruct((n, 4 * dc), qkvo_dtype), jax.ShapeDtypeStruct((n, LANES), F32),
                   jax.ShapeDtypeStruct((n, drg), F32), jax.ShapeDtypeStruct((n, drg), F32)],
        grid=(n // tm,),
        in_specs=[row(d), _full((1, d)), _full(w_packed.shape)],
        out_specs=[row(4 * dc), row(LANES), row(drg), row(drg)],
        compiler_params=pltpu.CompilerParams(dimension_semantics=("parallel",),
                                             vmem_limit_bytes=_vmem_limit(est)),
        name="rec_in",
    )(x, g.reshape(1, d), w_packed)


def _log_sigmoid(x):
    return jnp.minimum(x, 0.0) - jnp.log1p(jnp.exp(-jnp.abs(x)))


def _sigmoid(x):
    return 0.5 * jnp.tanh(0.5 * x) + 0.5


def _softplus(x):
    return jnp.maximum(x, 0.0) + jnp.log1p(jnp.exp(-jnp.abs(x)))


def _mlstm_body(qkvo_ref, gates_ref, bias_ref, ng_ref, hc_ref, c_out, n_out, m_out,
                c_ref, n_ref, m_ref, *, nh):
    ch = pl.program_id(1)
    nseq, L = qkvo_ref.shape[0], qkvo_ref.shape[1]

    @pl.when(ch == 0)
    def _():
        c_ref[...] = jnp.zeros(c_ref.shape, F32)
        n_ref[...] = jnp.zeros(n_ref.shape, F32)
        m_ref[...] = jnp.zeros(m_ref.shape, F32)

    lane = lax.broadcasted_iota(jnp.int32, (L, LANES), 1)
    tt = lax.broadcasted_iota(jnp.int32, (L, L), 0)
    ss = lax.broadcasted_iota(jnp.int32, (L, L), 1)
    causal = ss <= tt
    tril = jnp.where(causal, 1.0, 0.0).astype(BF16)
    for sq in range(nseq):
        g = gates_ref[sq] + bias_ref[...]
        x = jnp.where(lane < nh, g, jnp.where(lane < 2 * nh, _log_sigmoid(g), 0.0))
        x_hi, x_lo = _split_bf16(x)
        x_mid, x_lo = _split_bf16(x - x_hi.astype(F32))
        cum = _dot(tril, x_hi) + _dot(tril, x_mid) + _dot(tril, x_lo)
        colv = jnp.where(lane < nh, x, cum)
        rowv = colv.T
        for h in range(nh):
            _mlstm_head(qkvo_ref.at[sq], hc_ref.at[sq], ng_ref, c_ref.at[sq], n_ref.at[sq], m_ref.at[sq],
                        colv, rowv, causal, h, nh)

    @pl.when(ch == pl.num_programs(1) - 1)
    def _():
        c_out[...] = c_ref[...]
        n_out[...] = n_ref[...]
        m_out[...] = m_ref[...]


def _mlstm_head(qkvo_ref, hc_ref, ng_ref, c_ref, n_ref, m_ref, colv, rowv, causal, h, nh):
    L = qkvo_ref.shape[0]
    dc = nh * HEAD_DIM
    cs = slice(h * HEAD_DIM, (h + 1) * HEAD_DIM)
    q = qkvo_ref[:, cs]
    k = qkvo_ref[:, dc + h * HEAD_DIM:dc + (h + 1) * HEAD_DIM]
    v = qkvo_ref[:, 2 * dc + h * HEAD_DIM:2 * dc + (h + 1) * HEAD_DIM]
    o = qkvo_ref[:, 3 * dc + h * HEAD_DIM:3 * dc + (h + 1) * HEAD_DIM].astype(F32)
    rep = lambda col: jnp.broadcast_to(col, (L, LANES))
    ig_t = rep(colv[:, h:h + 1])
    b_t = rep(colv[:, nh + h:nh + h + 1])
    ig_row = rowv[h:h + 1, :]
    b_row = rowv[nh + h:nh + h + 1, :]
    m_prev = m_ref[h]
    c_prev = c_ref[h]
    n_prev = n_ref[h]

    dmat = jnp.where(causal, b_t - b_row + ig_row, NEG)
    inter = b_t + m_prev
    m_t = jnp.maximum(inter, rep(jnp.max(dmat, axis=-1, keepdims=True)))
    w_inter = jnp.exp(inter - m_t)
    s = _dot_nt(q, k) * jnp.exp(dmat - m_t)
    qf = q.astype(F32)
    num = w_inter * _dot(q, c_prev.astype(BF16)) + _dot(s.astype(BF16), v)
    den = (w_inter * rep(jnp.sum(qf * n_prev, axis=-1, keepdims=True))
           + rep(jnp.sum(s, axis=-1, keepdims=True)))
    hh = num / jnp.maximum(jnp.abs(den), jnp.exp(-m_t))
    m_new = m_t[L - 1:L, :]
    b_last = b_t[L - 1:L, :]
    w_old = jnp.exp(b_last + m_prev - m_new)
    w_new = jnp.exp(b_last - b_t + ig_t - m_new)
    kw = k.astype(F32) * w_new
    c_ref[h] = w_old * c_prev + _dot_tn(kw.astype(BF16), v)
    n_ref[h] = w_old * n_prev + jnp.sum(kw, axis=0, keepdims=True)
    m_ref[h] = m_new
    hc_ref[:, cs] = (_rms(hh, ng_ref[:, cs]) * jax.nn.sigmoid(o)).astype(hc_ref.dtype)


def _mlstm_call(qkvo, gates, bias, norm_g, batch, seq):
    n = qkvo.shape[0]
    dc = qkvo.shape[1] // 4
    nh = dc // HEAD_DIM
    L = MLSTM_CHUNK if seq % MLSTM_CHUNK == 0 else seq
    assert L % SUBLANES == 0 and L == LANES, "prompt mLSTM kernel needs 128-token chunks"
    nc = seq // L
    ns = math.gcd(MLSTM_SEQS_PER_STEP, batch)
    est = ns * (2 * L * 4 * dc * 2 + 2 * L * LANES * 4 + 2 * L * dc * 2 + 3 * nh * HEAD_DIM * HEAD_DIM * 4
                + 16 * L * L * 4)
    hc, c1, n1, m1 = pl.pallas_call(
        functools.partial(_mlstm_body, nh=nh),
        out_shape=[jax.ShapeDtypeStruct((batch, seq, dc), BF16),
                   jax.ShapeDtypeStruct((batch, nh, HEAD_DIM, HEAD_DIM), F32),
                   jax.ShapeDtypeStruct((batch, nh, 1, HEAD_DIM), F32),
                   jax.ShapeDtypeStruct((batch, nh, 1, LANES), F32)],
        grid=(batch // ns, nc),
        in_specs=[pl.BlockSpec((ns, L, 4 * dc), lambda b, c: (b, c, 0)),
                  pl.BlockSpec((ns, L, LANES), lambda b, c: (b, c, 0)),
                  _full((1, LANES)), _full((1, dc))],
        out_specs=[pl.BlockSpec((ns, L, dc), lambda b, c: (b, c, 0)),
                   pl.BlockSpec((ns, nh, HEAD_DIM, HEAD_DIM), lambda b, c: (b, 0, 0, 0)),
                   pl.BlockSpec((ns, nh, 1, HEAD_DIM), lambda b, c: (b, 0, 0, 0)),
                   pl.BlockSpec((ns, nh, 1, LANES), lambda b, c: (b, 0, 0, 0))],
        scratch_shapes=[pltpu.VMEM((ns, nh, HEAD_DIM, HEAD_DIM), F32), pltpu.VMEM((ns, nh, 1, HEAD_DIM), F32),
                        pltpu.VMEM((ns, nh, 1, LANES), F32)],
        compiler_params=pltpu.CompilerParams(dimension_semantics=("parallel", "arbitrary"),
                                             vmem_limit_bytes=_vmem_limit(est)),
        name="mlstm_prompt",
    )(qkvo.reshape(batch, seq, 4 * dc), gates.reshape(batch, seq, LANES), bias, norm_g)
    return hc.reshape(n, dc), c1, n1, m1


def _rg_gates(xc, wa_ref, ba_ref, wx_ref, bx_ref, lam_ref):
    xb = xc.astype(BF16)
    r = _sigmoid(_dot(xb, wa_ref[...]) + ba_ref[...])
    i = _sigmoid(_dot(xb, wx_ref[...]) + bx_ref[...])
    log_a = -RG_C * r * _softplus(-lam_ref[...])
    a = jnp.exp(log_a)
    u = jnp.sqrt(-jnp.tanh(log_a) * (a * a + 1.0)) * (i * xc)
    return a, u


def _rglru_body(xr_ref, gr_ref, cw_ref, cb_ref, wa_ref, ba_ref, wx_ref, bx_ref, lam_ref,
                y_ref, h_out, xbuf_ref, a_ref, u_ref, h_ref, *, width):
    tstep = pl.program_id(1)
    T, C = xr_ref.shape
    pad = SUBLANES

    @pl.when(tstep == 0)
    def _():
        xbuf_ref[0:pad, :] = jnp.zeros((pad, C), F32)
        h_ref[...] = jnp.zeros(h_ref.shape, F32)

    xbuf_ref[pad:pad + T, :] = xr_ref[...]
    xc = cb_ref[...]
    for j in range(width):
        xc = xc + xbuf_ref[pl.ds(pad - (width - 1) + j, T), :] * cw_ref[j:j + 1, :]
    xbuf_ref[0:pad, :] = xbuf_ref[T:T + pad, :]

    a, u = _rg_gates(xc, wa_ref, ba_ref, wx_ref, bx_ref, lam_ref)
    a_ref[...] = a
    u_ref[...] = u
    sub = lax.broadcasted_iota(jnp.int32, (SUBLANES, C), 0)

    def group(gi, h):
        start = pl.multiple_of(gi * SUBLANES, SUBLANES)
        ag = a_ref[pl.ds(start, SUBLANES), :]
        ug = u_ref[pl.ds(start, SUBLANES), :]
        for sh in (1, 2, 4):
            ap = jnp.where(sub >= sh, pltpu.roll(ag, sh, 0), 1.0)
            up = jnp.where(sub >= sh, pltpu.roll(ug, sh, 0), 0.0)
            ug = ag * up + ug
            ag = ag * ap
        hg = ag * h + ug
        u_ref[pl.ds(start, SUBLANES), :] = hg
        return hg[SUBLANES - 1:SUBLANES, :]

    h_last = lax.fori_loop(0, T // SUBLANES, group, h_ref[...])
    h_ref[...] = h_last
    y_ref[...] = (u_ref[...] * jax.nn.gelu(gr_ref[...], approximate=True)).astype(y_ref.dtype)

    @pl.when(tstep == pl.num_programs(1) - 1)
    def _():
        h_out[0] = h_last


def _rglru_call(xr, gr, conv_w, conv_b, wa, ba, wx, bx, lam, batch, seq):
    n, c = xr.shape
    width = conv_w.shape[0]
    t = min(RG_TILE, seq)
    assert seq % t == 0 and t % SUBLANES == 0 and width - 1 <= SUBLANES
    nt = seq // t
    row = pl.BlockSpec((t, c), lambda b, s: (b * nt + s, 0))
    est = 6 * t * c * 4 + 2 * c * c * 2 * 2 + 8 * t * c * 4
    return pl.pallas_call(
        functools.partial(_rglru_body, width=width),
        out_shape=[jax.ShapeDtypeStruct((n, c), BF16), jax.ShapeDtypeStruct((batch, 1, c), F32)],
        grid=(batch, nt),
        in_specs=[row, row, _full(conv_w.shape), _full((1, c)), _full(wa.shape), _full((1, c)),
                  _full(wx.shape), _full((1, c)), _full((1, c))],
        out_specs=[row, pl.BlockSpec((1, 1, c), lambda b, s: (b, 0, 0))],
        scratch_shapes=[pltpu.VMEM((t + SUBLANES, c), F32), pltpu.VMEM((t, c), F32),
                        pltpu.VMEM((t, c), F32), pltpu.VMEM((1, c), F32)],
        compiler_params=pltpu.CompilerParams(dimension_semantics=("parallel", "arbitrary"),
                                             vmem_limit_bytes=_vmem_limit(est)),
        name="rglru_prompt",
    )(xr, gr, conv_w, conv_b, wa, ba, wx, bx, lam)


def _to_column(row_vec):
    n = row_vec.shape[1]
    r = lax.broadcasted_iota(jnp.int32, (n, n), 0)
    c = lax.broadcasted_iota(jnp.int32, (n, n), 1)
    return jnp.sum(jnp.where(r == c, jnp.broadcast_to(row_vec, (n, n)), 0.0), axis=-1, keepdims=True)


def _rec_sample_body(qkvo_ref, gates_ref, xr_ref, gr_ref, c0_ref, n0_ref, m0_ref, h0_ref, conv0_ref,
                     bias_ref, ng_ref, cw_ref, cb_ref, wa_ref, ba_ref, wx_ref, bx_ref, lam_ref,
                     hc_ref, y_ref, c_out, n_out, m_out, h_out, conv_out, *, nh, width):
    dc = nh * HEAD_DIM
    g = gates_ref[0] + bias_ref[...]
    m0 = m0_ref[0]
    for h in range(nh):
        cs = slice(h * HEAD_DIM, (h + 1) * HEAD_DIM)
        q = qkvo_ref[0][:, cs]
        k = qkvo_ref[0][:, dc + h * HEAD_DIM:dc + (h + 1) * HEAD_DIM]
        v = qkvo_ref[0][:, 2 * dc + h * HEAD_DIM:2 * dc + (h + 1) * HEAD_DIM]
        o = qkvo_ref[0][:, 3 * dc + h * HEAD_DIM:3 * dc + (h + 1) * HEAD_DIM]
        ig = g[:, h:h + 1]
        lf = _log_sigmoid(g[:, nh + h:nh + h + 1])
        m_prev = m0[:, h:h + 1]
        c_prev = c0_ref[0, h]
        n_prev = n0_ref[0, h:h + 1, :]
        inter = lf + m_prev
        m_t = jnp.maximum(inter, ig)
        w_inter = jnp.exp(inter - m_t)
        s = jnp.sum(q * k, axis=-1, keepdims=True) * jnp.exp(ig - m_t)
        q_col = _to_column(q)
        qc = jnp.sum(q_col * c_prev, axis=0, keepdims=True)
        num = w_inter * qc + s * v
        den = w_inter * jnp.sum(q * n_prev, axis=-1, keepdims=True) + s
        hh = num / jnp.maximum(jnp.abs(den), jnp.exp(-m_t))
        w_old = jnp.exp(lf + m_prev - m_t)
        w_new = jnp.exp(ig - m_t)
        c_out[0, h] = w_old * c_prev + (_to_column(k) * w_new) * v
        n_out[0, h:h + 1, :] = w_old * n_prev + w_new * k
        m_out[0, :, h:h + 1] = m_t
        hc_ref[0, :, cs] = (_rms(hh, ng_ref[:, cs]) * jax.nn.sigmoid(o)).astype(hc_ref.dtype)

    x = xr_ref[0]
    xc = cb_ref[...]
    for j in range(width - 1):
        xc = xc + conv0_ref[0, j:j + 1, :] * cw_ref[j:j + 1, :]
    xc = xc + x * cw_ref[width - 1:width, :]
    a, u = _rg_gates(jnp.broadcast_to(xc, (SUBLANES, xc.shape[1])), wa_ref, ba_ref, wx_ref, bx_ref, lam_ref)
    h_new = a[0:1] * h0_ref[0] + u[0:1]
    h_out[0] = h_new
    y_ref[0] = (h_new * jax.nn.gelu(gr_ref[0], approximate=True)).astype(y_ref.dtype)
    for j in range(width - 2):
        conv_out[0, j:j + 1, :] = conv0_ref[0, j + 1:j + 2, :]
    conv_out[0, width - 2:width - 1, :] = x


def _rec_sample_call(qkvo, gates, xr, gr, c0, n0, m0, h0, conv0, bias, norm_g, conv_w, conv_b,
                     wa, ba, wx, bx, lam):
    db = qkvo.shape[0]
    dc = qkvo.shape[1] // 4
    nh = dc // HEAD_DIM
    c = xr.shape[1]
    width = conv_w.shape[0]
    tok = lambda w: pl.BlockSpec((1, 1, w), lambda b: (b, 0, 0))
    in_specs = [tok(4 * dc), tok(LANES), tok(c), tok(c),
                pl.BlockSpec((1, nh, HEAD_DIM, HEAD_DIM), lambda b: (b, 0, 0, 0)),
                pl.BlockSpec((1, nh, HEAD_DIM), lambda b: (b, 0, 0)),
                tok(nh), tok(c),
                pl.BlockSpec((1, width - 1, c), lambda b: (b, 0, 0)),
                _full((1, LANES)), _full((1, dc)), _full(conv_w.shape), _full((1, c)),
                _full(wa.shape), _full((1, c)), _full(wx.shape), _full((1, c)), _full((1, c))]
    out_shape = [jax.ShapeDtypeStruct((db, 1, dc), BF16), jax.ShapeDtypeStruct((db, 1, c), BF16),
                 jax.ShapeDtypeStruct((db, nh, HEAD_DIM, HEAD_DIM), F32),
                 jax.ShapeDtypeStruct((db, nh, HEAD_DIM), F32),
                 jax.ShapeDtypeStruct((db, 1, nh), F32),
                 jax.ShapeDtypeStruct((db, 1, c), F32),
                 jax.ShapeDtypeStruct((db, width - 1, c), F32)]
    out_specs = [tok(dc), tok(c),
                 pl.BlockSpec((1, nh, HEAD_DIM, HEAD_DIM), lambda b: (b, 0, 0, 0)),
                 pl.BlockSpec((1, nh, HEAD_DIM), lambda b: (b, 0, 0)),
                 tok(nh), tok(c),
                 pl.BlockSpec((1, width - 1, c), lambda b: (b, 0, 0))]
    est = 4 * nh * HEAD_DIM * HEAD_DIM * 4 + 2 * c * c * 2 * 2 + 1024 * 1024
    return pl.pallas_call(
        functools.partial(_rec_sample_body, nh=nh, width=width),
        out_shape=out_shape,
        grid=(db,),
        in_specs=in_specs,
        out_specs=out_specs,
        compiler_params=pltpu.CompilerParams(dimension_semantics=("parallel",),
                                             vmem_limit_bytes=_vmem_limit(est)),
        name="rec_sample",
    )(qkvo.reshape(db, 1, 4 * dc), gates.reshape(db, 1, LANES), xr.reshape(db, 1, c), gr.reshape(db, 1, c),
      c0, n0, m0.reshape(db, 1, nh), h0.reshape(db, 1, c), conv0, bias, norm_g, conv_w, conv_b,
      wa, ba, wx, bx, lam)


def _block_diag(w):
    n, k, j = w.shape
    eye = jnp.eye(n, dtype=w.dtype)
    return (eye[:, None, :, None] * w[:, :, None, :]).reshape(n * k, n * j)


def _lane_rows(vals):
    return jnp.broadcast_to(vals.astype(F32)[:, None, None], (vals.shape[0], 1, LANES))


def kernel(x_prompt, x_sample, cache_k, cache_v, state_mlstm_c, state_mlstm_n, state_mlstm_m, state_rglru_h, state_rglru_conv, page_table, norm_g, ffn_w_gate, ffn_w_up, ffn_w_down, att_w_in, att_w_out, diff_lambda_q1, diff_lambda_k1, diff_lambda_q2, diff_lambda_k2, diff_subln_g, rec_w_in, rec_w_out, mlstm_b_i, mlstm_b_f, mlstm_norm_g, rg_conv_w, rg_conv_b, rg_w_a, rg_b_a, rg_w_x, rg_b_x, rg_lambda, final_norm_g):
    batch, seq, d = x_prompt.shape
    db, dseq, _ = x_sample.shape
    assert dseq == 1, "the sample group holds one new token per sequence"
    depth = norm_g.shape[0]
    n_heads = cache_k.shape[3]
    dc = state_mlstm_c.shape[2] * HEAD_DIM
    nhc = dc // HEAD_DIM
    drg = state_rglru_h.shape[2]
    width = state_rglru_conv.shape[2] + 1

    xp = x_prompt.reshape(batch * seq, d)
    xs = x_sample.reshape(db, d)

    hidx = jnp.arange(n_heads, dtype=F32)
    slopes = 2.0 ** (-8.0 * (hidx + 1.0) / n_heads)
    slopes_a, slopes_b = _lane_rows(slopes[0::2]), _lane_rows(slopes[1::2])

    k_p, v_p, k_s, v_s = [], [], [], []
    rec_p, rec_s = [], []
    mix_p = mix_s = None
    wg, wu, wd = (w.astype(BF16) for w in (ffn_w_gate, ffn_w_up, ffn_w_down))
    for l in range(depth):
        j = l // 2
        xp = _ffn_call(xp, norm_g[l, 0], wg, wu, wd, l, 0)
        xs = _ffn_call(xs, norm_g[l, 0], wg, wu, wd, l, 0)
        if l % 2 == 0:
            lam_init = 0.8 - 0.6 * math.exp(-0.3 * l)
            w_in = att_w_in[j].astype(BF16)
            w_out = att_w_out[j].astype(BF16)
            lam_vecs = [v[j].reshape(1, HALF_A) for v in
                        (diff_lambda_q1, diff_lambda_k1, diff_lambda_q2, diff_lambda_k2)]
            subg = diff_subln_g[j].reshape(1, HEAD_DIM)
            q, k, v, kb, vb, kbar = _qkv_call(xp, norm_g[l, 1], w_in, prompt=True)
            kbar = kbar.reshape(batch, seq // MOBA_BLOCK, d // 2)
            kbar = jnp.pad(kbar, ((0, 0), (0, LANES - seq // MOBA_BLOCK), (0, 0)))
            qs, ks, vs = _qkv_call(xs, norm_g[l, 1], w_in, prompt=False)
            qs3, ks3, vs3 = (a.reshape(db, 1, d) for a in (qs, ks, vs))

            def stream(rows):
                return lambda locate: _sample_setup(
                    page_table[rows], cache_k, cache_v, j, qs3[rows], ks3[rows], vs3[rows], slopes_a, slopes_b,
                    lam_vecs, subg, lam_init, locate)

            steps = batch * (n_heads // 2 // ATT_HEADS_PER_STEP) * (seq // ATT_TILE)
            n_chunks = page_table.shape[1] // math.gcd(PAGES_PER_STEP, page_table.shape[1])
            if db % 2 == 0 and steps == (db // 2) * n_chunks:
                half = db // 2
                oa, oa_s0, ob_s0 = _diff_call(q, kb, vb, slopes_a, lam_vecs, subg, batch, seq, lam_init,
                                              sample=stream(slice(0, half)))
                ob, oa_s1, ob_s1 = _moba_call(q, kb, vb, kbar, slopes_b, batch, seq, sample=stream(slice(half, db)))
                oa_s, ob_s = jnp.concatenate([oa_s0, oa_s1]), jnp.concatenate([ob_s0, ob_s1])
            else:
                oa = _diff_call(q, kb, vb, slopes_a, lam_vecs, subg, batch, seq, lam_init)
                ob = _moba_call(q, kb, vb, kbar, slopes_b, batch, seq)
                oa_s, ob_s = _sample_a_call(page_table, cache_k, cache_v, j, qs3, ks3, vs3, slopes_a, slopes_b,
                                            lam_vecs, subg, lam_init)
            mix_p = (oa, ob, w_out)
            k_p.append(k.reshape(batch, seq, n_heads, HEAD_DIM))
            v_p.append(v.reshape(batch, seq, n_heads, HEAD_DIM))
            mix_s = (oa_s.reshape(db, d // 2).astype(BF16), ob_s.reshape(db, d // 2).astype(BF16), w_out)
            k_s.append(ks.reshape(db, 1, n_heads, HEAD_DIM))
            v_s.append(vs.reshape(db, 1, n_heads, HEAD_DIM))
        else:
            w = rec_w_in[j]
            n_gate = 2 * nhc
            w_packed = jnp.concatenate(
                [w[:, :4 * dc], w[:, 4 * dc + n_gate:],
                 jnp.pad(w[:, 4 * dc:4 * dc + n_gate], ((0, 0), (0, LANES - n_gate)))], axis=1).astype(BF16)
            w_out = rec_w_out[j].astype(BF16)
            bias = jnp.pad(jnp.concatenate([mlstm_b_i[j], mlstm_b_f[j]]), (0, LANES - n_gate)).reshape(1, LANES)
            ng = mlstm_norm_g[j].reshape(1, dc)
            conv_w, conv_b = rg_conv_w[j], rg_conv_b[j].reshape(1, drg)
            wa, wx = _block_diag(rg_w_a[j]).astype(BF16), _block_diag(rg_w_x[j]).astype(BF16)
            ba, bx, lam = (a[j].reshape(1, drg) for a in (rg_b_a, rg_b_x, rg_lambda))
            qkvo, gates, xr, gr = _rec_in_call(xp, norm_g[l, 1], w_packed, dc, drg, BF16)
            hc, c1, n1, m1 = _mlstm_call(qkvo, gates, bias, ng, batch, seq)
            y, h1 = _rglru_call(xr, gr, conv_w, conv_b, wa, ba, wx, bx, lam, batch, seq)
            mix_p = (hc, y, w_out)
            conv1 = xr.reshape(batch, seq, drg)[:, seq - (width - 1):, :]
            rec_p.append((c1, n1.reshape(batch, nhc, HEAD_DIM), m1[:, :, 0, 0], h1.reshape(batch, drg), conv1))
            qkvo_s, gates_s, xr_s, gr_s = _rec_in_call(xs, norm_g[l, 1], w_packed, dc, drg, F32)
            hc_s, y_s, c1s, n1s, m1s, h1s, conv1s = _rec_sample_call(
                qkvo_s, gates_s, xr_s, gr_s, state_mlstm_c[j], state_mlstm_n[j], state_mlstm_m[j],
                state_rglru_h[j], state_rglru_conv[j], bias, ng, conv_w, conv_b, wa, ba, wx, bx, lam)
            mix_s = (hc_s.reshape(db, dc), y_s.reshape(db, drg), w_out)
            rec_s.append((c1s, n1s, m1s.reshape(db, nhc), h1s.reshape(db, drg), conv1s))
        fg = final_norm_g if l == depth - 1 else None
        xp = _ffn_call(xp, norm_g[l, 2], wg, wu, wd, l, 1, mix=mix_p, final_g=fg)
        xs = _ffn_call(xs, norm_g[l, 2], wg, wu, wd, l, 1, mix=mix_s, final_g=fg)

    stack = lambda items: jnp.stack(items, axis=0)
    return (xp.reshape(batch, seq, d), xs.reshape(db, 1, d),
            stack(k_p), stack(v_p), stack(k_s), stack(v_s),
            *(stack([st[i] for st in rec_p]) for i in range(5)),
            *(stack([st[i] for st in rec_s]) for i in range(5)))
```

```python
import functools
import math

import jax
import jax.numpy as jnp
from jax import lax
from jax.experimental import pallas as pl
from jax.experimental.pallas import tpu as pltpu

F32 = jnp.float32
BF16 = jnp.bfloat16

HEAD_DIM = 128
HALF_A = HEAD_DIM // 2
MOBA_BLOCK = 256
MOBA_TOPK = 3
MLSTM_CHUNK = 128
RG_C = 8.0
RMS_EPS = 1e-6
NEG = -1e30

V7X_VMEM_BYTES = 64 * 1024 * 1024
LANES = 128
SUBLANES = 8

TOKEN_TILE = 512
FF_CHUNK = 256
ATT_TILE = 256
ATT_HEADS_PER_STEP = 4
MLSTM_SEQS_PER_STEP = 1
RG_TILE = 512
PAGES_PER_STEP = 16


def _vmem_limit(est_bytes):
    return int(min(max(2 * est_bytes, 32 * 1024 * 1024), V7X_VMEM_BYTES - 8 * 1024 * 1024))


def _rms(x, g):
    return x * lax.rsqrt(jnp.mean(x * x, axis=-1, keepdims=True) + RMS_EPS) * g


def _dot(a, b):
    return jnp.dot(a, b, preferred_element_type=F32)


def _dot_nt(a, b):
    return lax.dot_general(a, b, (((1,), (1,)), ((), ())), preferred_element_type=F32)


def _dot_tn(a, b):
    return lax.dot_general(a, b, (((0,), (0,)), ((), ())), preferred_element_type=F32)


def _split_bf16(x):
    hi = x.astype(BF16)
    lo = (x - hi.astype(F32)).astype(BF16)
    return hi, lo


def _full(shape):
    nd = len(shape)
    return pl.BlockSpec(shape, lambda *_: (0,) * nd)


def _ffn_body(*refs, has_mix, has_final, ff_chunk):
    refs = list(refs)
    x_ref = refs.pop(0)
    if has_mix:
        a_ref, b_ref, wo_ref = refs[:3]
        refs = refs[3:]
    g_ref, wg_ref, wu_ref, wd_ref = refs[:4]
    refs = refs[4:]
    if has_final:
        fg_ref = refs.pop(0)
    o_ref, act_ref = refs

    x = x_ref[...]
    if has_mix:
        da = a_ref.shape[1]
        x = x + _dot(a_ref[...], wo_ref[0:da, :]) + _dot(b_ref[...], wo_ref[da:, :])
    h = _rms(x, g_ref[...]).astype(BF16)
    ff = wg_ref.shape[1]
    for c in range(ff // ff_chunk):
        sl = slice(c * ff_chunk, (c + 1) * ff_chunk)
        gate = _dot(h, wg_ref[:, sl])
        up = _dot(h, wu_ref[:, sl])
        act_ref[:, sl] = (gate * jax.nn.sigmoid(gate) * up).astype(BF16)
    y = x + 0.5 * _dot(act_ref[...], wd_ref[...])
    if has_final:
        y = _rms(y, fg_ref[...])
    o_ref[...] = y


def _ffn_call(x, g, wg, wu, wd, l, j, mix=None, final_g=None):
    n, d = x.shape
    ff = wg.shape[3]
    tm = min(n, TOKEN_TILE)
    assert n % tm == 0 and ff % FF_CHUNK == 0
    row = lambda w: pl.BlockSpec((tm, w), lambda i: (i, 0))
    pick = lambda w: pl.BlockSpec((None, None) + w.shape[2:], lambda i: (l, j, 0, 0))
    args, specs = [x], [row(d)]
    if mix is not None:
        a, b, wo = mix
        args += [a, b, wo]
        specs += [row(a.shape[1]), row(b.shape[1]), _full(wo.shape)]
    args += [g.reshape(1, d), wg, wu, wd]
    specs += [_full((1, d)), pick(wg), pick(wu), pick(wd)]
    if final_g is not None:
        args.append(final_g.reshape(1, d))
        specs.append(_full((1, d)))
    est = 2 * 3 * d * ff * 2 + 4 * tm * d * 4 + tm * ff * 2 + 4 * tm * FF_CHUNK * 4 + 2 * tm * d * 4
    return pl.pallas_call(
        functools.partial(_ffn_body, has_mix=mix is not None, has_final=final_g is not None,
                          ff_chunk=FF_CHUNK),
        out_shape=jax.ShapeDtypeStruct((n, d), F32),
        grid=(n // tm,),
        in_specs=specs,
        out_specs=row(d),
        scratch_shapes=[pltpu.VMEM((tm, ff), BF16)],
        compiler_params=pltpu.CompilerParams(dimension_semantics=("parallel",),
                                             vmem_limit_bytes=_vmem_limit(est)),
        name="ffn",
    )(*args)


def _qkv_body(x_ref, g_ref, w_ref, *out_refs, prompt):
    x = x_ref[...]
    d = x.shape[1]
    h = _rms(x, g_ref[...]).astype(BF16)
    q = _dot(h, w_ref[:, 0:d])
    k = _dot(h, w_ref[:, d:2 * d])
    v = _dot(h, w_ref[:, 2 * d:3 * d])
    if not prompt:
        q_ref, k_ref, v_ref = out_refs
        q_ref[...] = q
        k_ref[...] = k
        v_ref[...] = v
        return
    q_ref, k_ref, v_ref, kb_ref, vb_ref, kbar_ref = out_refs
    q_ref[...] = q.astype(BF16)
    k_ref[...] = k
    v_ref[...] = v
    lane = lax.broadcasted_iota(jnp.int32, (1, d), 1)
    scale = jnp.where(lane < d // 2, HALF_A ** -0.5, HEAD_DIM ** -0.5).astype(F32)
    kb_ref[...] = (k * scale).astype(BF16)
    vb_ref[...] = v.astype(BF16)
    tm = x.shape[0]
    for r in range(tm // MOBA_BLOCK):
        blk = k[r * MOBA_BLOCK:(r + 1) * MOBA_BLOCK, d // 2:]
        kbar_ref[0, r:r + 1, :] = jnp.mean(blk, axis=0, keepdims=True)


def _qkv_call(x, g, w_in, prompt):
    n, d = x.shape
    tm = min(n, TOKEN_TILE)
    assert n % tm == 0
    row = lambda w: pl.BlockSpec((tm, w), lambda i: (i, 0))
    if prompt:
        assert tm % MOBA_BLOCK == 0
        nb = tm // MOBA_BLOCK
        out_shape = [jax.ShapeDtypeStruct((n, d), BF16), jax.ShapeDtypeStruct((n, d), F32),
                     jax.ShapeDtypeStruct((n, d), F32), jax.ShapeDtypeStruct((n, d), BF16),
                     jax.ShapeDtypeStruct((n, d), BF16),
                     jax.ShapeDtypeStruct((n // tm, nb, d // 2), F32)]
        out_specs = [row(d)] * 5 + [pl.BlockSpec((1, nb, d // 2), lambda i: (i, 0, 0))]
    else:
        out_shape = [jax.ShapeDtypeStruct((n, d), F32)] * 3
        out_specs = [row(d)] * 3
    est = 2 * d * 3 * d * 2 + 2 * tm * d * 4 * 4 + 3 * tm * d * 4
    return pl.pallas_call(
        functools.partial(_qkv_body, prompt=prompt),
        out_shape=out_shape,
        grid=(n // tm,),
        in_specs=[row(d), _full((1, d)), _full(w_in.shape)],
        out_specs=out_specs,
        compiler_params=pltpu.CompilerParams(dimension_semantics=("parallel",),
                                             vmem_limit_bytes=_vmem_limit(est)),
        name="qkv_prompt" if prompt else "qkv_sample",
    )(x, g.reshape(1, d), w_in)


def _lanes(x, width):
    return x if width == LANES else jnp.concatenate([x] * (width // LANES), axis=1)


def _softmax_tile(s, shift, v_tile, ones_col, m_ref, acc_ref):
    m_old = m_ref[...]
    m_new = jnp.maximum(m_old, jnp.max(s, axis=-1, keepdims=True) + shift)
    p = jnp.exp(s - _lanes(m_new - shift, s.shape[1]))
    alpha = jnp.exp(m_old - m_new)
    va = jnp.concatenate([v_tile, ones_col], axis=1)
    acc_ref[...] = _lanes(alpha, 2 * LANES) * acc_ref[...] + _dot(p.astype(BF16), va)
    m_ref[...] = m_new


def _softmax_result(acc_ref):
    acc = acc_ref[...]
    return acc[:, 0:HEAD_DIM] / acc[:, HEAD_DIM:HEAD_DIM + 1]


def _chunk_shift(slope, delta):
    return slope * (jnp.zeros((1, 1), jnp.int32) + delta).astype(F32)


def _offset_cols(tk, lo_lane, hi_lane):
    lane = lax.broadcasted_iota(jnp.int32, (tk, LANES), 1)
    c = lax.broadcasted_iota(jnp.int32, (tk, LANES), 0)
    c_lo = jnp.bitwise_and(c, 255)
    return jnp.where(lane == lo_lane, c_lo, jnp.where(lane == hi_lane, c - c_lo, 0)).astype(F32)


def _ones_col(rows):
    lane = lax.broadcasted_iota(jnp.int32, (rows, LANES), 1)
    return jnp.where(lane == 0, 1.0, 0.0).astype(BF16)


def _causal_bias(t):
    r = lax.broadcasted_iota(jnp.int32, (t, t), 0)
    c = lax.broadcasted_iota(jnp.int32, (t, t), 1)
    return jnp.where(c <= r, 0.0, NEG).astype(F32)


def _add_diagonal_bias(s, bias):
    t = bias.shape[1]
    width = s.shape[1]
    if width == t:
        return s + bias
    return jnp.concatenate([s[:, 0:width - t], s[:, width - t:] + bias], axis=1)


def _key_chunk(seq):
    return next(tk for tk in (1024, 512, 256) if seq % tk == 0)


def _diff_lambda(lq1_ref, lk1_ref, lq2_ref, lk2_ref, lam_init):
    a = jnp.exp(jnp.sum(lq1_ref[...] * lk1_ref[...], axis=-1, keepdims=True))
    b = jnp.exp(jnp.sum(lq2_ref[...] * lk2_ref[...], axis=-1, keepdims=True))
    return a - b + lam_init


def _diff_body(slope_ref, q_ref, k_ref, v_ref, lq1_ref, lk1_ref, lq2_ref, lk2_ref, subg_ref,
               o_ref, qa_ref, m_ref, acc_ref, *, t, tk, lam_init, hook=None):
    i = pl.program_id(2)
    hp = qa_ref.shape[0]
    lane = lax.broadcasted_iota(jnp.int32, (t, HEAD_DIM), 1)
    consts = {tk: (_offset_cols(tk, 0, 1).astype(BF16), _ones_col(tk))}
    slopes = []
    for hh in range(hp):
        cs_h = slice(hh * HEAD_DIM, (hh + 1) * HEAD_DIM)
        q = q_ref[:, cs_h]
        zero = jnp.zeros_like(q)
        slopes.append(slope_ref[hh][:, 0:1])
        q_ext = jnp.where(lane < 2, slopes[hh], 0.0).astype(BF16)
        qa_ref[hh, 0:t, 0:HEAD_DIM] = jnp.where(lane < HALF_A, q, zero)
        qa_ref[hh, t:2 * t, 0:HEAD_DIM] = jnp.where(lane >= HALF_A, q, zero)
        qa_ref[hh, 0:t, HEAD_DIM:] = q_ext
        qa_ref[hh, t:2 * t, HEAD_DIM:] = q_ext

    m_ref[...] = jnp.full(m_ref.shape, NEG, F32)
    acc_ref[...] = jnp.zeros(acc_ref.shape, F32)
    q0 = i * t

    tri = _causal_bias(t)

    def chunk(cs, width, diagonal):
        kext, ones_col = consts.get(width) or (_offset_cols(width, 0, 1).astype(BF16), _ones_col(width))
        for hh in range(hp):
            cs_h = slice(hh * HEAD_DIM, (hh + 1) * HEAD_DIM)
            ka = jnp.concatenate([k_ref[pl.ds(cs, width), cs_h], kext], axis=1)
            s = _dot_nt(qa_ref[hh], ka)
            if diagonal:
                s = _add_diagonal_bias(s, jnp.concatenate([tri, tri], axis=0))
            _softmax_tile(s, _chunk_shift(slopes[hh], cs - q0), v_ref[pl.ds(cs, width), cs_h],
                          ones_col, m_ref.at[hh], acc_ref.at[hh])

    def body(j, carry):
        chunk(pl.multiple_of(j * tk, tk), tk, False)
        return carry

    n_full = q0 // tk
    lax.fori_loop(0, n_full, body, 0)
    rest = i - n_full * (tk // t)
    for r in range(tk // t):
        @pl.when(rest == r)
        def _():
            last_chunk = functools.partial(chunk, pl.multiple_of(n_full * tk, tk), (r + 1) * t, True)
            last_chunk() if hook is None else hook(last_chunk)

    lam = _diff_lambda(lq1_ref, lk1_ref, lq2_ref, lk2_ref, lam_init)
    for hh in range(hp):
        o = _softmax_result(acc_ref.at[hh])
        oa = o[0:t] - lam * o[t:2 * t]
        o_ref[:, hh * HEAD_DIM:(hh + 1) * HEAD_DIM] = (
            _rms(oa, subg_ref[...]) * (1.0 - lam_init)).astype(BF16)


def _hosted_call(body, *, name, grid, in_specs, out_shape, out_spec, scratch, args, est, sample):
    if sample is None:
        return pl.pallas_call(
            body, out_shape=out_shape, grid=grid, in_specs=in_specs, out_specs=out_spec, scratch_shapes=scratch,
            compiler_params=pltpu.CompilerParams(dimension_semantics=("parallel",) * len(grid),
                                                 vmem_limit_bytes=_vmem_limit(est)),
            name=name)(*args)

    def step_index(*idx):
        g = idx[0]
        for size, i in zip(grid[1:], idx[1:]):
            g = g * size + i
        return g

    st = sample(lambda n_chunks, *idx: (step_index(*idx) // n_chunks, step_index(*idx) % n_chunks))
    n_chunks = st["n_chunks"]
    assert math.prod(grid) == st["prefetch"].shape[0] * n_chunks
    n_in, n_s_in = len(in_specs), len(st["in_specs"])

    def hosted(pt_ref, *refs):
        del pt_ref
        a_in, s_in = refs[:n_in], refs[n_in:n_in + n_s_in]
        rest = refs[n_in + n_s_in:]
        a_out, s_out = rest[:1], rest[1:1 + N_SAMPLE_OUT]
        a_scr, s_scr = rest[1 + N_SAMPLE_OUT:len(rest) - N_SAMPLE_SCRATCH], rest[len(rest) - N_SAMPLE_SCRATCH:]
        g = step_index(*(pl.program_id(ax) for ax in range(len(grid))))
        body(*a_in, *a_out, *a_scr, hook=lambda last_chunk: _sample_stream(
            lax.rem(g, n_chunks), n_chunks, s_in, s_out, s_scr, alongside=last_chunk, **st["kwargs"]))

    return pl.pallas_call(
        hosted,
        out_shape=[out_shape] + st["out_shape"],
        grid_spec=pltpu.PrefetchScalarGridSpec(
            num_scalar_prefetch=1, grid=grid,
            in_specs=list(in_specs) + st["in_specs"],
            out_specs=[out_spec] + st["out_specs"],
            scratch_shapes=list(scratch) + st["scratch"]),
        compiler_params=pltpu.CompilerParams(dimension_semantics=("arbitrary",) * len(grid),
                                             vmem_limit_bytes=_vmem_limit(est + st["vmem"])),
        name=name + "_with_sample_stream")(st["prefetch"], *args, *st["args"])


def _diff_call(q, kb, vb, slopes, lam_vecs, subg, batch, seq, lam_init, sample=None):
    n, d = q.shape
    nh = d // 2 // HEAD_DIM
    t = ATT_TILE
    tk = _key_chunk(seq)
    assert seq % t == 0 and tk % t == 0
    nq = seq // t
    hp = ATT_HEADS_PER_STEP
    assert nh % hp == 0
    vec = lambda w: _full((1, w))
    kv_bufs = {} if sample is None else dict(pipeline_mode=pl.Buffered(1))
    est = hp * ((2 if sample is None else 1) * 2 * seq * HEAD_DIM * 2 + 2 * t * 2 * HEAD_DIM * 2
                + 2 * t * 3 * LANES * 4 + 6 * 2 * t * tk * 4)
    return _hosted_call(
        functools.partial(_diff_body, t=t, tk=tk, lam_init=lam_init),
        name="diff_attn_prompt",
        grid=(batch, nh // hp, nq),
        in_specs=[pl.BlockSpec((hp, 1, LANES), lambda b, h, i, *_: (h, 0, 0)),
                  pl.BlockSpec((t, hp * HEAD_DIM), lambda b, h, i, *_: (b * nq + i, h)),
                  pl.BlockSpec((seq, hp * HEAD_DIM), lambda b, h, i, *_: (b, h), **kv_bufs),
                  pl.BlockSpec((seq, hp * HEAD_DIM), lambda b, h, i, *_: (b, h), **kv_bufs),
                  vec(HALF_A), vec(HALF_A), vec(HALF_A), vec(HALF_A), vec(HEAD_DIM)],
        out_shape=jax.ShapeDtypeStruct((n, d // 2), BF16),
        out_spec=pl.BlockSpec((t, hp * HEAD_DIM), lambda b, h, i, *_: (b * nq + i, h)),
        scratch=[pltpu.VMEM((hp, 2 * t, 2 * HEAD_DIM), BF16), pltpu.VMEM((hp, 2 * t, LANES), F32),
                 pltpu.VMEM((hp, 2 * t, 2 * LANES), F32)],
        args=(slopes, q, kb, vb, *lam_vecs, subg),
        est=est,
        sample=sample)


def _moba_body(slope_ref, q_ref, k_ref, v_ref, kbar_ref, o_ref, qa_ref, m_ref, acc_ref,
               *, t, tk, n_blocks, hook=None):
    i = pl.program_id(2)
    hp = qa_ref.shape[0]
    lane = lax.broadcasted_iota(jnp.int32, (t, LANES), 1)
    nbp = -(-n_blocks // SUBLANES) * SUBLANES
    blk = lax.broadcasted_iota(jnp.int32, (nbp, t), 0)
    slopes = []
    for hh in range(hp):
        cs_h = slice(hh * HEAD_DIM, (hh + 1) * HEAD_DIM)
        slope = slope_ref[hh][:, 0:1]
        slopes.append(slope)
        q = q_ref[:, cs_h]
        kbar_hi, kbar_lo = _split_bf16(kbar_ref[0, :, cs_h])
        gate = (_dot_nt(kbar_hi, q) + _dot_nt(kbar_lo, q))[0:nbp]
        gate = jnp.where(blk < i, gate, NEG)
        cnt = jnp.zeros((nbp, t), jnp.int32)
        for n2 in range(n_blocks):
            gn = gate[n2:n2 + 1, :]
            beats = (gn > gate) | ((gn == gate) & (blk > n2))
            cnt = cnt + jnp.where(beats & (i > n2), 1, 0)
        keep = ((blk < i) & (cnt < MOBA_TOPK)) | (blk == i)
        pen = jnp.concatenate([jnp.where(keep, 0.0, NEG), jnp.full((LANES - nbp, t), NEG, F32)], axis=0).T
        qa_ref[hh, :, 0:HEAD_DIM] = q
        qa_ref[hh, :, HEAD_DIM:] = jnp.where(lane >= LANES - 2, slope, pen).astype(BF16)

    def key_consts(width):
        lane_k = lax.broadcasted_iota(jnp.int32, (width, LANES), 1)
        row_k = lax.broadcasted_iota(jnp.int32, (width, LANES), 0)
        return _offset_cols(width, LANES - 2, LANES - 1), lane_k - row_k // MOBA_BLOCK, _ones_col(width)

    consts = {tk: key_consts(tk)}

    m_ref[...] = jnp.full(m_ref.shape, NEG, F32)
    acc_ref[...] = jnp.zeros(acc_ref.shape, F32)
    q0 = i * t

    tri = _causal_bias(t)

    def chunk(cs, width, diagonal):
        offs, blk_off, ones_col = consts.get(width) or key_consts(width)
        kext = jnp.where(blk_off == cs // MOBA_BLOCK, 1.0, offs).astype(BF16)
        for hh in range(hp):
            cs_h = slice(hh * HEAD_DIM, (hh + 1) * HEAD_DIM)
            ka = jnp.concatenate([k_ref[pl.ds(cs, width), cs_h], kext], axis=1)
            s = _dot_nt(qa_ref[hh], ka)
            if diagonal:
                s = _add_diagonal_bias(s, tri)
            _softmax_tile(s, _chunk_shift(slopes[hh], cs - q0), v_ref[pl.ds(cs, width), cs_h],
                          ones_col, m_ref.at[hh], acc_ref.at[hh])

    def body(j, carry):
        chunk(pl.multiple_of(j * tk, tk), tk, False)
        return carry

    n_full = q0 // tk
    lax.fori_loop(0, n_full, body, 0)
    rest = i - n_full * (tk // t)
    for r in range(tk // t):
        @pl.when(rest == r)
        def _():
            last_chunk = functools.partial(chunk, pl.multiple_of(n_full * tk, tk), (r + 1) * t, True)
            last_chunk() if hook is None else hook(last_chunk)
    for hh in range(hp):
        o_ref[:, hh * HEAD_DIM:(hh + 1) * HEAD_DIM] = _softmax_result(acc_ref.at[hh]).astype(BF16)


def _moba_call(q, kb, vb, kbar, slopes, batch, seq, sample=None):
    n, d = q.shape
    nh = d // 2 // HEAD_DIM
    t = MOBA_BLOCK
    tk = _key_chunk(seq)
    assert seq % t == 0
    nq = seq // t
    assert nq <= LANES - 2
    hp = ATT_HEADS_PER_STEP
    assert nh % hp == 0
    ng = nh // hp
    kv_bufs = {} if sample is None else dict(pipeline_mode=pl.Buffered(1))
    est = hp * ((2 if sample is None else 1) * 2 * seq * HEAD_DIM * 2 + 2 * t * 2 * HEAD_DIM * 2
                + t * 3 * LANES * 4 + 8 * t * tk * 4)
    return _hosted_call(
        functools.partial(_moba_body, t=t, tk=tk, n_blocks=nq),
        name="moba_attn_prompt",
        grid=(batch, ng, nq),
        in_specs=[pl.BlockSpec((hp, 1, LANES), lambda b, h, i, *_: (h, 0, 0)),
                  pl.BlockSpec((t, hp * HEAD_DIM), lambda b, h, i, *_: (b * nq + i, ng + h)),
                  pl.BlockSpec((seq, hp * HEAD_DIM), lambda b, h, i, *_: (b, ng + h), **kv_bufs),
                  pl.BlockSpec((seq, hp * HEAD_DIM), lambda b, h, i, *_: (b, ng + h), **kv_bufs),
                  pl.BlockSpec((1, LANES, hp * HEAD_DIM), lambda b, h, i, *_: (b, 0, h))],
        out_shape=jax.ShapeDtypeStruct((n, d // 2), BF16),
        out_spec=pl.BlockSpec((t, hp * HEAD_DIM), lambda b, h, i, *_: (b * nq + i, h)),
        scratch=[pltpu.VMEM((hp, t, 2 * HEAD_DIM), BF16), pltpu.VMEM((hp, t, LANES), F32),
                 pltpu.VMEM((hp, t, 2 * LANES), F32)],
        args=(slopes, q, kb, vb, kbar),
        est=est,
        sample=sample)


def _sample_a_body(pt_ref, *refs, pps, n_pages, nh, lam_init):
    del pt_ref
    n_in = 2 * pps + N_SAMPLE_IN
    _sample_stream(pl.program_id(1), pl.num_programs(1), refs[:n_in], refs[n_in:n_in + N_SAMPLE_OUT],
                   refs[n_in + N_SAMPLE_OUT:], pps=pps, n_pages=n_pages, nh=nh, lam_init=lam_init)


N_SAMPLE_IN = 10
N_SAMPLE_OUT = 2
N_SAMPLE_SCRATCH = 8


def _sample_stream(c, n_chunks, ins, outs, scratch, *, pps, n_pages, nh, lam_init, alongside=None):
    kp = ins[:pps]
    vp = ins[pps:2 * pps]
    q_ref, kn_ref, vn_ref, slope_ref, slope_b_ref, lq1_ref, lk1_ref, lq2_ref, lk2_ref, subg_ref = ins[2 * pps:]
    oa_ref, ob_ref = outs
    w_ref, m_ref, l_ref, acc_ref, gate_ref, bm_ref, bl_ref, bacc_ref = scratch
    n_heads = 2 * nh
    page = kp[0].shape[1] // n_heads
    past = n_pages * page
    nd = 2 * nh
    qrows = 2 * nd
    scale_b = HEAD_DIM ** -0.5

    @pl.when(c == 0)
    def _():
        r = lax.broadcasted_iota(jnp.int32, (qrows, HEAD_DIM), 0)
        ln = lax.broadcasted_iota(jnp.int32, (qrows, HEAD_DIM), 1)
        w = jnp.zeros((qrows, HEAD_DIM), F32)
        for h in range(n_heads):
            qb = jnp.broadcast_to(q_ref[0, h:h + 1, :], (qrows, HEAD_DIM))
            if h < nh:
                keep = ((r == 2 * h) & (ln < HALF_A)) | ((r == 2 * h + 1) & (ln >= HALF_A))
                w = jnp.where(keep, qb * (HALF_A ** -0.5), w)
            else:
                hi = qb.astype(BF16).astype(F32)
                w = jnp.where(r == nd + (h - nh), hi, jnp.where(r == nd + nh + (h - nh), qb - hi, w))
        w_ref[...] = w.astype(BF16)
        m_ref[...] = jnp.full(m_ref.shape, NEG, F32)
        l_ref[...] = jnp.zeros(l_ref.shape, F32)
        acc_ref[...] = jnp.zeros(acc_ref.shape, F32)
        gate_ref[...] = jnp.zeros(gate_ref.shape, F32)
        bm_ref[...] = jnp.full(bm_ref.shape, NEG, F32)
        bl_ref[...] = jnp.zeros(bl_ref.shape, F32)
        bacc_ref[...] = jnp.zeros(bacc_ref.shape, F32)

    pcols = page * n_heads
    width = pps * pcols
    bcols = MOBA_BLOCK * n_heads
    nbs = width // bcols
    w = w_ref[...]
    s = jnp.concatenate([_dot_nt(w, kp[p][0].astype(BF16)) for p in range(pps)], axis=1)
    col = lax.broadcasted_iota(jnp.int32, (1, width), 1)
    row = lax.broadcasted_iota(jnp.int32, (qrows, 1), 0)
    row_head = jnp.where(row < nd, row // 2, nh + lax.rem(row - nd, nh))
    own = jnp.bitwise_and(col, n_heads - 1) == row_head
    kpos = c * (pps * page) + col // n_heads
    dist = (past - kpos).astype(F32)
    slope_rows = jnp.concatenate(
        [jnp.broadcast_to(slope_ref[h][:, 0:1], (2, 1)) for h in range(nh)], axis=0)
    sd = jnp.where(own[0:nd], s[0:nd] - slope_rows * dist, NEG)
    m_old = m_ref[...]
    m_new = jnp.maximum(m_old, jnp.max(sd, axis=-1, keepdims=True))
    pr = jnp.exp(sd - m_new[:, 0:1])
    alpha = jnp.exp(m_old - m_new)
    l_ref[...] = alpha * l_ref[...] + jnp.sum(pr, axis=-1, keepdims=True)
    m_ref[...] = m_new

    sm = s[nd:]
    full = sm + pltpu.roll(sm, nh, 0)
    own_b = own[nd:]
    slope_b_rows = jnp.concatenate(
        [jnp.broadcast_to(slope_b_ref[h][:, 0:1], (1, 1)) for h in range(nh)] * 2, axis=0)
    sb = jnp.where(own_b, full * scale_b - slope_b_rows * dist, NEG)
    sg = jnp.where(own_b, full, 0.0)
    glane = lax.broadcasted_iota(jnp.int32, (nd, LANES), 1)
    gate_add = jnp.zeros((nd, LANES), F32)
    bm = bm_ref[...]
    bl = bl_ref[...]
    p_blocks = []
    for bi in range(nbs):
        cols = slice(bi * bcols, (bi + 1) * bcols)
        blk = c * nbs + bi
        gate_add = gate_add + jnp.where(glane == blk, jnp.sum(sg[:, cols], axis=-1, keepdims=True), 0.0)
        m_b = jnp.max(sb[:, cols], axis=-1, keepdims=True)
        p_b = jnp.exp(sb[:, cols] - m_b)
        bm = jnp.where(glane == blk, m_b, bm)
        bl = jnp.where(glane == blk, jnp.sum(p_b, axis=-1, keepdims=True), bl)
        p_blocks.append(p_b)
    gate_ref[...] += gate_add
    bm_ref[...] = bm
    bl_ref[...] = bl
    pm = jnp.concatenate(p_blocks, axis=1)

    ppb = MOBA_BLOCK // page
    pv = jnp.zeros((nd, HEAD_DIM), F32)
    for bi in range(nbs):
        pv_b = jnp.zeros((nd, HEAD_DIM), F32)
        for p in range(bi * ppb, (bi + 1) * ppb):
            cols = slice(p * pcols, (p + 1) * pcols)
            both = _dot(jnp.concatenate([pr[:, cols], pm[:, cols]], axis=0).astype(BF16), vp[p][0].astype(BF16))
            pv = pv + both[0:nd]
            pv_b = pv_b + both[nd:]
        bacc_ref[pl.ds(pl.multiple_of((c * nbs + bi) * nd, nd), nd), :] = pv_b
    acc_ref[...] = alpha * acc_ref[...] + pv
    if alongside is not None:
        alongside()

    @pl.when(c == n_chunks - 1)
    def _():
        wf = w_ref[...].astype(F32)[0:nd]
        twice = lambda ref: jnp.concatenate(
            [jnp.broadcast_to(ref[0, h:h + 1, :], (2, HEAD_DIM)) for h in range(nh)], axis=0)
        s_new = jnp.sum(wf * twice(kn_ref), axis=-1, keepdims=True)
        m_old = m_ref[...]
        m_new = jnp.maximum(m_old, s_new)
        p_new = jnp.exp(s_new - m_new)
        alpha = jnp.exp(m_old - m_new)
        l = alpha * l_ref[...] + p_new
        o = (alpha * acc_ref[...] + p_new * twice(vn_ref)) / l
        lam = _diff_lambda(lq1_ref, lk1_ref, lq2_ref, lk2_ref, lam_init)
        for h in range(nh):
            oa = o[2 * h:2 * h + 1, :] - lam * o[2 * h + 1:2 * h + 2, :]
            oa_ref[0, :, h * HEAD_DIM:(h + 1) * HEAD_DIM] = _rms(oa, subg_ref[...]) * (1.0 - lam_init)

        n_past = past // MOBA_BLOCK
        g = gate_ref[0:nh, :] * (1.0 / MOBA_BLOCK)
        gl = lax.broadcasted_iota(jnp.int32, g.shape, 1)
        g = jnp.where(gl < n_past, g, NEG)
        cnt = jnp.zeros(g.shape, jnp.int32)
        for n2 in range(n_past):
            gn = g[:, n2:n2 + 1]
            cnt = cnt + jnp.where((gn > g) | ((gn == g) & (gl > n2)), 1, 0)
        keep = (gl < n_past) & (cnt < MOBA_TOPK)
        qb = q_ref[0, nh:n_heads, :]
        sb_new = jnp.sum(qb * kn_ref[0, nh:n_heads, :], axis=-1, keepdims=True) * scale_b
        bm_k = jnp.where(keep, bm_ref[0:nh, :], NEG)
        m_all = jnp.maximum(jnp.max(bm_k, axis=-1, keepdims=True), sb_new)
        wts = jnp.where(keep, jnp.exp(bm_k - m_all), 0.0)
        pb_new = jnp.exp(sb_new - m_all)
        lb = jnp.sum(wts * bl_ref[0:nh, :], axis=-1, keepdims=True) + pb_new
        for h in range(nh):
            a_h = bacc_ref[pl.ds(h, LANES, stride=nd), :]
            w_h = jnp.broadcast_to(wts[h:h + 1, :], (SUBLANES, LANES))
            w_hi, w_lo = _split_bf16(w_h)
            a_hi, a_lo = _split_bf16(a_h)
            ob = (_dot(w_hi, a_hi) + _dot(w_lo, a_hi) + _dot(w_hi, a_lo))[0:1]
            ob = (ob + pb_new[h:h + 1] * vn_ref[0, nh + h:nh + h + 1, :]) / lb[h:h + 1]
            ob_ref[0, :, h * HEAD_DIM:(h + 1) * HEAD_DIM] = ob


def _sample_setup(page_table, cache_k, cache_v, layer, q, k_new, v_new, slopes, slopes_b, lam_vecs, subg, lam_init,
                  locate):
    db, n_pages = page_table.shape
    n_layers, n_pool, page, n_heads, hd = cache_k.shape
    d = n_heads * hd
    nh = n_heads // 2
    assert (n_pages * page) % MOBA_BLOCK == 0 and MOBA_BLOCK % page == 0
    assert n_pages * page // MOBA_BLOCK <= LANES
    pps = math.gcd(PAGES_PER_STEP, n_pages)
    assert pps % (MOBA_BLOCK // page) == 0
    ck = cache_k.reshape(n_layers * n_pool, page * n_heads, hd)
    cv = cache_v.reshape(n_layers * n_pool, page * n_heads, hd)
    off = layer * n_pool
    n_chunks = n_pages // pps

    def page_spec(p):
        def index(*a):
            seq, chunk = locate(n_chunks, *a[:-1])
            return (a[-1][seq, chunk * pps + p] + off, 0, 0)
        return pl.BlockSpec((1, page * n_heads, hd), index)

    per_seq = lambda shape: pl.BlockSpec(
        shape, lambda *a: (locate(n_chunks, *a[:-1])[0],) + (0,) * (len(shape) - 1))
    fixed = lambda shape: pl.BlockSpec(shape, lambda *a: (0,) * len(shape))
    tok = per_seq((1, n_heads, hd))
    nd = 2 * nh
    to_heads = lambda a: a.reshape(db, n_heads, hd)
    stat = pltpu.VMEM((nd, LANES), F32)
    return dict(
        pps=pps,
        n_chunks=n_chunks,
        prefetch=page_table,
        args=[*([ck] * pps), *([cv] * pps), to_heads(q), to_heads(k_new), to_heads(v_new), slopes, slopes_b,
              *lam_vecs, subg],
        in_specs=[page_spec(p) for p in range(pps)] * 2 + [
            tok, tok, tok, fixed((nh, 1, LANES)), fixed((nh, 1, LANES)), fixed((1, HALF_A)), fixed((1, HALF_A)),
            fixed((1, HALF_A)), fixed((1, HALF_A)), fixed((1, HEAD_DIM))],
        out_shape=[jax.ShapeDtypeStruct((db, 1, d // 2), F32), jax.ShapeDtypeStruct((db, 1, d // 2), F32)],
        out_specs=[per_seq((1, 1, d // 2)), per_seq((1, 1, d // 2))],
        scratch=[pltpu.VMEM((2 * nd, HEAD_DIM), BF16), stat, stat, pltpu.VMEM((nd, HEAD_DIM), F32), stat, stat,
                 stat, pltpu.VMEM((LANES * nd, HEAD_DIM), F32)],
        vmem=2 * 2 * pps * page * d * 4 + 10 * 2 * nd * pps * page * n_heads * 4,
        kwargs=dict(pps=pps, n_pages=n_pages, nh=nh, lam_init=lam_init),
    )


def _sample_a_call(page_table, cache_k, cache_v, layer, q, k_new, v_new, slopes, slopes_b, lam_vecs, subg,
                   lam_init):
    st = _sample_setup(page_table, cache_k, cache_v, layer, q, k_new, v_new, slopes, slopes_b, lam_vecs, subg,
                       lam_init, locate=lambda n_chunks, b, c: (b, c))
    return pl.pallas_call(
        functools.partial(_sample_a_body, **st["kwargs"]),
        out_shape=st["out_shape"],
        grid_spec=pltpu.PrefetchScalarGridSpec(
            num_scalar_prefetch=1,
            grid=(page_table.shape[0], st["n_chunks"]),
            in_specs=st["in_specs"],
            out_specs=st["out_specs"],
            scratch_shapes=st["scratch"]),
        compiler_params=pltpu.CompilerParams(dimension_semantics=("parallel", "arbitrary"),
                                             vmem_limit_bytes=_vmem_limit(st["vmem"])),
        name="sample_attn",
    )(st["prefetch"], *st["args"])


def _rec_in_body(x_ref, g_ref, w_ref, qkvo_ref, gates_ref, xr_ref, gr_ref, *, dc, drg):
    x = x_ref[...]
    h = _rms(x, g_ref[...]).astype(BF16)
    kscale = HEAD_DIM ** -0.5
    qkvo_ref[:, 0:dc] = _dot(h, w_ref[:, 0:dc]).astype(qkvo_ref.dtype)
    qkvo_ref[:, dc:2 * dc] = (_dot(h, w_ref[:, dc:2 * dc]) * kscale).astype(qkvo_ref.dtype)
    qkvo_ref[:, 2 * dc:3 * dc] = _dot(h, w_ref[:, 2 * dc:3 * dc]).astype(qkvo_ref.dtype)
    qkvo_ref[:, 3 * dc:4 * dc] = _dot(h, w_ref[:, 3 * dc:4 * dc]).astype(qkvo_ref.dtype)
    xr_ref[...] = _dot(h, w_ref[:, 4 * dc:4 * dc + drg])
    gr_ref[...] = _dot(h, w_ref[:, 4 * dc + drg:4 * dc + 2 * drg])
    gates_ref[...] = _dot(h, w_ref[:, 4 * dc + 2 * drg:])


def _rec_in_call(x, g, w_packed, dc, drg, qkvo_dtype):
    n, d = x.shape
    tm = min(n, TOKEN_TILE)
    assert n % tm == 0
    row = lambda w: pl.BlockSpec((tm, w), lambda i: (i, 0))
    est = 2 * w_packed.size * 2 + 2 * tm * d * 4 + 2 * tm * (4 * dc + 2 * drg + LANES) * 4
    return pl.pallas_call(
        functools.partial(_rec_in_body, dc=dc, drg=drg),
        out_shape=[jax.ShapeDtypeStruct((n, 4 * dc), qkvo_dtype), jax.ShapeDtypeStruct((n, LANES), F32),
                   jax.ShapeDtypeStruct((n, drg), F32), jax.ShapeDtypeStruct((n, drg), F32)],
        grid=(n // tm,),
        in_specs=[row(d), _full((1, d)), _full(w_packed.shape)],
        out_specs=[row(4 * dc), row(LANES), row(drg), row(drg)],
        compiler_params=pltpu.CompilerParams(dimension_semantics=("parallel",),
                                             vmem_limit_bytes=_vmem_limit(est)),
        name="rec_in",
    )(x, g.reshape(1, d), w_packed)


def _log_sigmoid(x):
    return jnp.minimum(x, 0.0) - jnp.log1p(jnp.exp(-jnp.abs(x)))


def _sigmoid(x):
    return 0.5 * jnp.tanh(0.5 * x) + 0.5


def _softplus(x):
    return jnp.maximum(x, 0.0) + jnp.log1p(jnp.exp(-jnp.abs(x)))


def _mlstm_body(qkvo_ref, gates_ref, bias_ref, ng_ref, hc_ref, c_out, n_out, m_out,
                c_ref, n_ref, m_ref, *, nh):
    ch = pl.program_id(1)
    nseq, L = qkvo_ref.shape[0], qkvo_ref.shape[1]

    @pl.when(ch == 0)
    def _():
        c_ref[...] = jnp.zeros(c_ref.shape, F32)
        n_ref[...] = jnp.zeros(n_ref.shape, F32)
        m_ref[...] = jnp.zeros(m_ref.shape, F32)

    lane = lax.broadcasted_iota(jnp.int32, (L, LANES), 1)
    tt = lax.broadcasted_iota(jnp.int32, (L, L), 0)
    ss = lax.broadcasted_iota(jnp.int32, (L, L), 1)
    causal = ss <= tt
    tril = jnp.where(causal, 1.0, 0.0).astype(BF16)
    for sq in range(nseq):
        g = gates_ref[sq] + bias_ref[...]
        x = jnp.where(lane < nh, g, jnp.where(lane < 2 * nh, _log_sigmoid(g), 0.0))
        x_hi, x_lo = _split_bf16(x)
        x_mid, x_lo = _split_bf16(x - x_hi.astype(F32))
        cum = _dot(tril, x_hi) + _dot(tril, x_mid) + _dot(tril, x_lo)
        colv = jnp.where(lane < nh, x, cum)
        rowv = colv.T
        for h in range(nh):
            _mlstm_head(qkvo_ref.at[sq], hc_ref.at[sq], ng_ref, c_ref.at[sq], n_ref.at[sq], m_ref.at[sq],
                        colv, rowv, causal, h, nh)

    @pl.when(ch == pl.num_programs(1) - 1)
    def _():
        c_out[...] = c_ref[...]
        n_out[...] = n_ref[...]
        m_out[...] = m_ref[...]


def _mlstm_head(qkvo_ref, hc_ref, ng_ref, c_ref, n_ref, m_ref, colv, rowv, causal, h, nh):
    L = qkvo_ref.shape[0]
    dc = nh * HEAD_DIM
    cs = slice(h * HEAD_DIM, (h + 1) * HEAD_DIM)
    q = qkvo_ref[:, cs]
    k = qkvo_ref[:, dc + h * HEAD_DIM:dc + (h + 1) * HEAD_DIM]
    v = qkvo_ref[:, 2 * dc + h * HEAD_DIM:2 * dc + (h + 1) * HEAD_DIM]
    o = qkvo_ref[:, 3 * dc + h * HEAD_DIM:3 * dc + (h + 1) * HEAD_DIM].astype(F32)
    rep = lambda col: jnp.broadcast_to(col, (L, LANES))
    ig_t = rep(colv[:, h:h + 1])
    b_t = rep(colv[:, nh + h:nh + h + 1])
    ig_row = rowv[h:h + 1, :]
    b_row = rowv[nh + h:nh + h + 1, :]
    m_prev = m_ref[h]
    c_prev = c_ref[h]
    n_prev = n_ref[h]

    dmat = jnp.where(causal, b_t - b_row + ig_row, NEG)
    inter = b_t + m_prev
    m_t = jnp.maximum(inter, rep(jnp.max(dmat, axis=-1, keepdims=True)))
    w_inter = jnp.exp(inter - m_t)
    s = _dot_nt(q, k) * jnp.exp(dmat - m_t)
    qf = q.astype(F32)
    num = w_inter * _dot(q, c_prev.astype(BF16)) + _dot(s.astype(BF16), v)
    den = (w_inter * rep(jnp.sum(qf * n_prev, axis=-1, keepdims=True))
           + rep(jnp.sum(s, axis=-1, keepdims=True)))
    hh = num / jnp.maximum(jnp.abs(den), jnp.exp(-m_t))
    m_new = m_t[L - 1:L, :]
    b_last = b_t[L - 1:L, :]
    w_old = jnp.exp(b_last + m_prev - m_new)
    w_new = jnp.exp(b_last - b_t + ig_t - m_new)
    kw = k.astype(F32) * w_new
    c_ref[h] = w_old * c_prev + _dot_tn(kw.astype(BF16), v)
    n_ref[h] = w_old * n_prev + jnp.sum(kw, axis=0, keepdims=True)
    m_ref[h] = m_new
    hc_ref[:, cs] = (_rms(hh, ng_ref[:, cs]) * jax.nn.sigmoid(o)).astype(hc_ref.dtype)


def _mlstm_call(qkvo, gates, bias, norm_g, batch, seq):
    n = qkvo.shape[0]
    dc = qkvo.shape[1] // 4
    nh = dc // HEAD_DIM
    L = MLSTM_CHUNK if seq % MLSTM_CHUNK == 0 else seq
    assert L % SUBLANES == 0 and L == LANES, "prompt mLSTM kernel needs 128-token chunks"
    nc = seq // L
    ns = math.gcd(MLSTM_SEQS_PER_STEP, batch)
    est = ns * (2 * L * 4 * dc * 2 + 2 * L * LANES * 4 + 2 * L * dc * 2 + 3 * nh * HEAD_DIM * HEAD_DIM * 4
                + 16 * L * L * 4)
    hc, c1, n1, m1 = pl.pallas_call(
        functools.partial(_mlstm_body, nh=nh),
        out_shape=[jax.ShapeDtypeStruct((batch, seq, dc), BF16),
                   jax.ShapeDtypeStruct((batch, nh, HEAD_DIM, HEAD_DIM), F32),
                   jax.ShapeDtypeStruct((batch, nh, 1, HEAD_DIM), F32),
                   jax.ShapeDtypeStruct((batch, nh, 1, LANES), F32)],
        grid=(batch // ns, nc),
        in_specs=[pl.BlockSpec((ns, L, 4 * dc), lambda b, c: (b, c, 0)),
                  pl.BlockSpec((ns, L, LANES), lambda b, c: (b, c, 0)),
                  _full((1, LANES)), _full((1, dc))],
        out_specs=[pl.BlockSpec((ns, L, dc), lambda b, c: (b, c, 0)),
                   pl.BlockSpec((ns, nh, HEAD_DIM, HEAD_DIM), lambda b, c: (b, 0, 0, 0)),
                   pl.BlockSpec((ns, nh, 1, HEAD_DIM), lambda b, c: (b, 0, 0, 0)),
                   pl.BlockSpec((ns, nh, 1, LANES), lambda b, c: (b, 0, 0, 0))],
        scratch_shapes=[pltpu.VMEM((ns, nh, HEAD_DIM, HEAD_DIM), F32), pltpu.VMEM((ns, nh, 1, HEAD_DIM), F32),
                        pltpu.VMEM((ns, nh, 1, LANES), F32)],
        compiler_params=pltpu.CompilerParams(dimension_semantics=("parallel", "arbitrary"),
                                             vmem_limit_bytes=_vmem_limit(est)),
        name="mlstm_prompt",
    )(qkvo.reshape(batch, seq, 4 * dc), gates.reshape(batch, seq, LANES), bias, norm_g)
    return hc.reshape(n, dc), c1, n1, m1


def _rg_gates(xc, wa_ref, ba_ref, wx_ref, bx_ref, lam_ref):
    xb = xc.astype(BF16)
    r = _sigmoid(_dot(xb, wa_ref[...]) + ba_ref[...])
    i = _sigmoid(_dot(xb, wx_ref[...]) + bx_ref[...])
    log_a = -RG_C * r * _softplus(-lam_ref[...])
    a = jnp.exp(log_a)
    u = jnp.sqrt(-jnp.tanh(log_a) * (a * a + 1.0)) * (i * xc)
    return a, u


def _rglru_body(xr_ref, gr_ref, cw_ref, cb_ref, wa_ref, ba_ref, wx_ref, bx_ref, lam_ref,
                y_ref, h_out, xbuf_ref, a_ref, u_ref, h_ref, *, width):
    tstep = pl.program_id(1)
    T, C = xr_ref.shape
    pad = SUBLANES

    @pl.when(tstep == 0)
    def _():
        xbuf_ref[0:pad, :] = jnp.zeros((pad, C), F32)
        h_ref[...] = jnp.zeros(h_ref.shape, F32)

    xbuf_ref[pad:pad + T, :] = xr_ref[...]
    xc = cb_ref[...]
    for j in range(width):
        xc = xc + xbuf_ref[pl.ds(pad - (width - 1) + j, T), :] * cw_ref[j:j + 1, :]
    xbuf_ref[0:pad, :] = xbuf_ref[T:T + pad, :]

    a, u = _rg_gates(xc, wa_ref, ba_ref, wx_ref, bx_ref, lam_ref)
    a_ref[...] = a
    u_ref[...] = u
    sub = lax.broadcasted_iota(jnp.int32, (SUBLANES, C), 0)

    def group(gi, h):
        start = pl.multiple_of(gi * SUBLANES, SUBLANES)
        ag = a_ref[pl.ds(start, SUBLANES), :]
        ug = u_ref[pl.ds(start, SUBLANES), :]
        for sh in (1, 2, 4):
            ap = jnp.where(sub >= sh, pltpu.roll(ag, sh, 0), 1.0)
            up = jnp.where(sub >= sh, pltpu.roll(ug, sh, 0), 0.0)
            ug = ag * up + ug
            ag = ag * ap
        hg = ag * h + ug
        u_ref[pl.ds(start, SUBLANES), :] = hg
        return hg[SUBLANES - 1:SUBLANES, :]

    h_last = lax.fori_loop(0, T // SUBLANES, group, h_ref[...])
    h_ref[...] = h_last
    y_ref[...] = (u_ref[...] * jax.nn.gelu(gr_ref[...], approximate=True)).astype(y_ref.dtype)

    @pl.when(tstep == pl.num_programs(1) - 1)
    def _():
        h_out[0] = h_last


def _rglru_call(xr, gr, conv_w, conv_b, wa, ba, wx, bx, lam, batch, seq):
    n, c = xr.shape
    width = conv_w.shape[0]
    t = min(RG_TILE, seq)
    assert seq % t == 0 and t % SUBLANES == 0 and width - 1 <= SUBLANES
    nt = seq // t
    row = pl.BlockSpec((t, c), lambda b, s: (b * nt + s, 0))
    est = 6 * t * c * 4 + 2 * c * c * 2 * 2 + 8 * t * c * 4
    return pl.pallas_call(
        functools.partial(_rglru_body, width=width),
        out_shape=[jax.ShapeDtypeStruct((n, c), BF16), jax.ShapeDtypeStruct((batch, 1, c), F32)],
        grid=(batch, nt),
        in_specs=[row, row, _full(conv_w.shape), _full((1, c)), _full(wa.shape), _full((1, c)),
                  _full(wx.shape), _full((1, c)), _full((1, c))],
        out_specs=[row, pl.BlockSpec((1, 1, c), lambda b, s: (b, 0, 0))],
        scratch_shapes=[pltpu.VMEM((t + SUBLANES, c), F32), pltpu.VMEM((t, c), F32),
                        pltpu.VMEM((t, c), F32), pltpu.VMEM((1, c), F32)],
        compiler_params=pltpu.CompilerParams(dimension_semantics=("parallel", "arbitrary"),
                                             vmem_limit_bytes=_vmem_limit(est)),
        name="rglru_prompt",
    )(xr, gr, conv_w, conv_b, wa, ba, wx, bx, lam)


def _to_column(row_vec):
    n = row_vec.shape[1]
    r = lax.broadcasted_iota(jnp.int32, (n, n), 0)
    c = lax.broadcasted_iota(jnp.int32, (n, n), 1)
    return jnp.sum(jnp.where(r == c, jnp.broadcast_to(row_vec, (n, n)), 0.0), axis=-1, keepdims=True)


def _rec_sample_body(qkvo_ref, gates_ref, xr_ref, gr_ref, c0_ref, n0_ref, m0_ref, h0_ref, conv0_ref,
                     bias_ref, ng_ref, cw_ref, cb_ref, wa_ref, ba_ref, wx_ref, bx_ref, lam_ref,
                     hc_ref, y_ref, c_out, n_out, m_out, h_out, conv_out, *, nh, width):
    dc = nh * HEAD_DIM
    g = gates_ref[0] + bias_ref[...]
    m0 = m0_ref[0]
    for h in range(nh):
        cs = slice(h * HEAD_DIM, (h + 1) * HEAD_DIM)
        q = qkvo_ref[0][:, cs]
        k = qkvo_ref[0][:, dc + h * HEAD_DIM:dc + (h + 1) * HEAD_DIM]
        v = qkvo_ref[0][:, 2 * dc + h * HEAD_DIM:2 * dc + (h + 1) * HEAD_DIM]
        o = qkvo_ref[0][:, 3 * dc + h * HEAD_DIM:3 * dc + (h + 1) * HEAD_DIM]
        ig = g[:, h:h + 1]
        lf = _log_sigmoid(g[:, nh + h:nh + h + 1])
        m_prev = m0[:, h:h + 1]
        c_prev = c0_ref[0, h]
        n_prev = n0_ref[0, h:h + 1, :]
        inter = lf + m_prev
        m_t = jnp.maximum(inter, ig)
        w_inter = jnp.exp(inter - m_t)
        s = jnp.sum(q * k, axis=-1, keepdims=True) * jnp.exp(ig - m_t)
        q_col = _to_column(q)
        qc = jnp.sum(q_col * c_prev, axis=0, keepdims=True)
        num = w_inter * qc + s * v
        den = w_inter * jnp.sum(q * n_prev, axis=-1, keepdims=True) + s
        hh = num / jnp.maximum(jnp.abs(den), jnp.exp(-m_t))
        w_old = jnp.exp(lf + m_prev - m_t)
        w_new = jnp.exp(ig - m_t)
        c_out[0, h] = w_old * c_prev + (_to_column(k) * w_new) * v
        n_out[0, h:h + 1, :] = w_old * n_prev + w_new * k
        m_out[0, :, h:h + 1] = m_t
        hc_ref[0, :, cs] = (_rms(hh, ng_ref[:, cs]) * jax.nn.sigmoid(o)).astype(hc_ref.dtype)

    x = xr_ref[0]
    xc = cb_ref[...]
    for j in range(width - 1):
        xc = xc + conv0_ref[0, j:j + 1, :] * cw_ref[j:j + 1, :]
    xc = xc + x * cw_ref[width - 1:width, :]
    a, u = _rg_gates(jnp.broadcast_to(xc, (SUBLANES, xc.shape[1])), wa_ref, ba_ref, wx_ref, bx_ref, lam_ref)
    h_new = a[0:1] * h0_ref[0] + u[0:1]
    h_out[0] = h_new
    y_ref[0] = (h_new * jax.nn.gelu(gr_ref[0], approximate=True)).astype(y_ref.dtype)
    for j in range(width - 2):
        conv_out[0, j:j + 1, :] = conv0_ref[0, j + 1:j + 2, :]
    conv_out[0, width - 2:width - 1, :] = x


def _rec_sample_call(qkvo, gates, xr, gr, c0, n0, m0, h0, conv0, bias, norm_g, conv_w, conv_b,
                     wa, ba, wx, bx, lam):
    db = qkvo.shape[0]
    dc = qkvo.shape[1] // 4
    nh = dc // HEAD_DIM
    c = xr.shape[1]
    width = conv_w.shape[0]
    tok = lambda w: pl.BlockSpec((1, 1, w), lambda b: (b, 0, 0))
    in_specs = [tok(4 * dc), tok(LANES), tok(c), tok(c),
                pl.BlockSpec((1, nh, HEAD_DIM, HEAD_DIM), lambda b: (b, 0, 0, 0)),
                pl.BlockSpec((1, nh, HEAD_DIM), lambda b: (b, 0, 0)),
                tok(nh), tok(c),
                pl.BlockSpec((1, width - 1, c), lambda b: (b, 0, 0)),
                _full((1, LANES)), _full((1, dc)), _full(conv_w.shape), _full((1, c)),
                _full(wa.shape), _full((1, c)), _full(wx.shape), _full((1, c)), _full((1, c))]
    out_shape = [jax.ShapeDtypeStruct((db, 1, dc), BF16), jax.ShapeDtypeStruct((db, 1, c), BF16),
                 jax.ShapeDtypeStruct((db, nh, HEAD_DIM, HEAD_DIM), F32),
                 jax.ShapeDtypeStruct((db, nh, HEAD_DIM), F32),
                 jax.ShapeDtypeStruct((db, 1, nh), F32),
                 jax.ShapeDtypeStruct((db, 1, c), F32),
                 jax.ShapeDtypeStruct((db, width - 1, c), F32)]
    out_specs = [tok(dc), tok(c),
                 pl.BlockSpec((1, nh, HEAD_DIM, HEAD_DIM), lambda b: (b, 0, 0, 0)),
                 pl.BlockSpec((1, nh, HEAD_DIM), lambda b: (b, 0, 0)),
                 tok(nh), tok(c),
                 pl.BlockSpec((1, width - 1, c), lambda b: (b, 0, 0))]
    est = 4 * nh * HEAD_DIM * HEAD_DIM * 4 + 2 * c * c * 2 * 2 + 1024 * 1024
    return pl.pallas_call(
        functools.partial(_rec_sample_body, nh=nh, width=width),
        out_shape=out_shape,
        grid=(db,),
        in_specs=in_specs,
        out_specs=out_specs,
        compiler_params=pltpu.CompilerParams(dimension_semantics=("parallel",),
                                             vmem_limit_bytes=_vmem_limit(est)),
        name="rec_sample",
    )(qkvo.reshape(db, 1, 4 * dc), gates.reshape(db, 1, LANES), xr.reshape(db, 1, c), gr.reshape(db, 1, c),
      c0, n0, m0.reshape(db, 1, nh), h0.reshape(db, 1, c), conv0, bias, norm_g, conv_w, conv_b,
      wa, ba, wx, bx, lam)


def _block_diag(w):
    n, k, j = w.shape
    eye = jnp.eye(n, dtype=w.dtype)
    return (eye[:, None, :, None] * w[:, :, None, :]).reshape(n * k, n * j)


def _lane_rows(vals):
    return jnp.broadcast_to(vals.astype(F32)[:, None, None], (vals.shape[0], 1, LANES))


def kernel(x_prompt, x_sample, cache_k, cache_v, state_mlstm_c, state_mlstm_n, state_mlstm_m, state_rglru_h, state_rglru_conv, page_table, norm_g, ffn_w_gate, ffn_w_up, ffn_w_down, att_w_in, att_w_out, diff_lambda_q1, diff_lambda_k1, diff_lambda_q2, diff_lambda_k2, diff_subln_g, rec_w_in, rec_w_out, mlstm_b_i, mlstm_b_f, mlstm_norm_g, rg_conv_w, rg_conv_b, rg_w_a, rg_b_a, rg_w_x, rg_b_x, rg_lambda, final_norm_g):
    batch, seq, d = x_prompt.shape
    db, dseq, _ = x_sample.shape
    assert dseq == 1, "the sample group holds one new token per sequence"
    depth = norm_g.shape[0]
    n_heads = cache_k.shape[3]
    dc = state_mlstm_c.shape[2] * HEAD_DIM
    nhc = dc // HEAD_DIM
    drg = state_rglru_h.shape[2]
    width = state_rglru_conv.shape[2] + 1

    xp = x_prompt.reshape(batch * seq, d)
    xs = x_sample.reshape(db, d)

    hidx = jnp.arange(n_heads, dtype=F32)
    slopes = 2.0 ** (-8.0 * (hidx + 1.0) / n_heads)
    slopes_a, slopes_b = _lane_rows(slopes[0::2]), _lane_rows(slopes[1::2])

    k_p, v_p, k_s, v_s = [], [], [], []
    rec_p, rec_s = [], []
    mix_p = mix_s = None
    wg, wu, wd = (w.astype(BF16) for w in (ffn_w_gate, ffn_w_up, ffn_w_down))
    for l in range(depth):
        j = l // 2
        xp = _ffn_call(xp, norm_g[l, 0], wg, wu, wd, l, 0)
        xs = _ffn_call(xs, norm_g[l, 0], wg, wu, wd, l, 0)
        if l % 2 == 0:
            lam_init = 0.8 - 0.6 * math.exp(-0.3 * l)
            w_in = att_w_in[j].astype(BF16)
            w_out = att_w_out[j].astype(BF16)
            lam_vecs = [v[j].reshape(1, HALF_A) for v in
                        (diff_lambda_q1, diff_lambda_k1, diff_lambda_q2, diff_lambda_k2)]
            subg = diff_subln_g[j].reshape(1, HEAD_DIM)
            q, k, v, kb, vb, kbar = _qkv_call(xp, norm_g[l, 1], w_in, prompt=True)
            kbar = kbar.reshape(batch, seq // MOBA_BLOCK, d // 2)
            kbar = jnp.pad(kbar, ((0, 0), (0, LANES - seq // MOBA_BLOCK), (0, 0)))
            qs, ks, vs = _qkv_call(xs, norm_g[l, 1], w_in, prompt=False)
            qs3, ks3, vs3 = (a.reshape(db, 1, d) for a in (qs, ks, vs))

            def stream(rows):
                return lambda locate: _sample_setup(
                    page_table[rows], cache_k, cache_v, j, qs3[rows], ks3[rows], vs3[rows], slopes_a, slopes_b,
                    lam_vecs, subg, lam_init, locate)

            steps = batch * (n_heads // 2 // ATT_HEADS_PER_STEP) * (seq // ATT_TILE)
            n_chunks = page_table.shape[1] // math.gcd(PAGES_PER_STEP, page_table.shape[1])
            if db % 2 == 0 and steps == (db // 2) * n_chunks:
                half = db // 2
                oa, oa_s0, ob_s0 = _diff_call(q, kb, vb, slopes_a, lam_vecs, subg, batch, seq, lam_init,
                                              sample=stream(slice(0, half)))
                ob, oa_s1, ob_s1 = _moba_call(q, kb, vb, kbar, slopes_b, batch, seq, sample=stream(slice(half, db)))
                oa_s, ob_s = jnp.concatenate([oa_s0, oa_s1]), jnp.concatenate([ob_s0, ob_s1])
            else:
                oa = _diff_call(q, kb, vb, slopes_a, lam_vecs, subg, batch, seq, lam_init)
                ob = _moba_call(q, kb, vb, kbar, slopes_b, batch, seq)
                oa_s, ob_s = _sample_a_call(page_table, cache_k, cache_v, j, qs3, ks3, vs3, slopes_a, slopes_b,
                                            lam_vecs, subg, lam_init)
            mix_p = (oa, ob, w_out)
            k_p.append(k.reshape(batch, seq, n_heads, HEAD_DIM))
            v_p.append(v.reshape(batch, seq, n_heads, HEAD_DIM))
            mix_s = (oa_s.reshape(db, d // 2).astype(BF16), ob_s.reshape(db, d // 2).astype(BF16), w_out)
            k_s.append(ks.reshape(db, 1, n_heads, HEAD_DIM))
            v_s.append(vs.reshape(db, 1, n_heads, HEAD_DIM))
        else:
            w = rec_w_in[j]
            n_gate = 2 * nhc
            w_packed = jnp.concatenate(
                [w[:, :4 * dc], w[:, 4 * dc + n_gate:],
                 jnp.pad(w[:, 4 * dc:4 * dc + n_gate], ((0, 0), (0, LANES - n_gate)))], axis=1).astype(BF16)
            w_out = rec_w_out[j].astype(BF16)
            bias = jnp.pad(jnp.concatenate([mlstm_b_i[j], mlstm_b_f[j]]), (0, LANES - n_gate)).reshape(1, LANES)
            ng = mlstm_norm_g[j].reshape(1, dc)
            conv_w, conv_b = rg_conv_w[j], rg_conv_b[j].reshape(1, drg)
            wa, wx = _block_diag(rg_w_a[j]).astype(BF16), _block_diag(rg_w_x[j]).astype(BF16)
            ba, bx, lam = (a[j].reshape(1, drg) for a in (rg_b_a, rg_b_x, rg_lambda))
            qkvo, gates, xr, gr = _rec_in_call(xp, norm_g[l, 1], w_packed, dc, drg, BF16)
            hc, c1, n1, m1 = _mlstm_call(qkvo, gates, bias, ng, batch, seq)
            y, h1 = _rglru_call(xr, gr, conv_w, conv_b, wa, ba, wx, bx, lam, batch, seq)
            mix_p = (hc, y, w_out)
            conv1 = xr.reshape(batch, seq, drg)[:, seq - (width - 1):, :]
            rec_p.append((c1, n1.reshape(batch, nhc, HEAD_DIM), m1[:, :, 0, 0], h1.reshape(batch, drg), conv1))
            qkvo_s, gates_s, xr_s, gr_s = _rec_in_call(xs, norm_g[l, 1], w_packed, dc, drg, F32)
            hc_s, y_s, c1s, n1s, m1s, h1s, conv1s = _rec_sample_call(
                qkvo_s, gates_s, xr_s, gr_s, state_mlstm_c[j], state_mlstm_n[j], state_mlstm_m[j],
                state_rglru_h[j], state_rglru_conv[j], bias, ng, conv_w, conv_b, wa, ba, wx, bx, lam)
            mix_s = (hc_s.reshape(db, dc), y_s.reshape(db, drg), w_out)
            rec_s.append((c1s, n1s, m1s.reshape(db, nhc), h1s.reshape(db, drg), conv1s))
        fg = final_norm_g if l == depth - 1 else None
        xp = _ffn_call(xp, norm_g[l, 2], wg, wu, wd, l, 1, mix=mix_p, final_g=fg)
        xs = _ffn_call(xs, norm_g[l, 2], wg, wu, wd, l, 1, mix=mix_s, final_g=fg)

    stack = lambda items: jnp.stack(items, axis=0)
    return (xp.reshape(batch, seq, d), xs.reshape(db, 1, d),
            stack(k_p), stack(v_p), stack(k_s), stack(v_s),
            *(stack([st[i] for st in rec_p]) for i in range(5)),
            *(stack([st[i] for st in rec_s]) for i in range(5)))
```
